```python
import math
import jax, jax.numpy as jnp
from jax import lax
import numpy as np

D_MODEL = 1024
BATCH = 8
SEQ = 2048
DEPTH = 1

HYENA_WIDTH = D_MODEL
POOL_WIDTH = D_MODEL
HYENA_ORDER = 2
SHORT_CONV = 3
FILTER_EMB = 33
FILTER_HIDDEN = 64
FAST_DECAY_PCT = 0.3
SLOW_DECAY_PCT = 1.5
DECAY_TARGET = 1e-2
MAX_DECAY = math.log(DECAY_TARGET) / FAST_DECAY_PCT
MIN_DECAY = math.log(DECAY_TARGET) / SLOW_DECAY_PCT
POOL_WINDOWS = (2, 4, 8, 16)
POOL_GROUPS = len(POOL_WINDOWS)
POOL_GROUP_WIDTH = POOL_WIDTH // POOL_GROUPS
N_BRANCHES = 2
HY_IN = (HYENA_ORDER + 1) * HYENA_WIDTH
PROJ_COLS = HY_IN + HYENA_WIDTH + 2 * POOL_WIDTH + N_BRANCHES * D_MODEL
NORM_EPS = 1e-6

kernel_name = "hyena_pool_gated_hybrid_encoder"


def rms_norm(x, g):
    xf = x.astype(jnp.float32)
    y = xf * lax.rsqrt(jnp.mean(xf * xf, axis=-1, keepdims=True) + NORM_EPS)
    return (y * g.astype(jnp.float32)).astype(x.dtype)


def centred_short_conv(u, w, b):
    L = u.shape[1]
    half = SHORT_CONV // 2
    up = jnp.pad(u, ((0, 0), (half, half), (0, 0)))
    out = up[:, 0:L] * w[0]
    for k in range(1, SHORT_CONV):
        out = out + up[:, k:k + L] * w[k]
    return out + b


def filter_features(L):
    t = jnp.linspace(0.0, 1.0, L, dtype=jnp.float32)[:, None]
    bands = (FILTER_EMB - 1) // 2
    w = 2.0 * math.pi * jnp.arange(L, dtype=jnp.float32) / L
    f = jnp.linspace(1e-4, bands - 1, bands, dtype=jnp.float32)
    ang = w[:, None] * f[None, :]
    return jnp.concatenate([t, jnp.cos(ang), -jnp.sin(ang)], axis=-1), t


def implicit_filters(L, w1, b1, w2, b2, w3, b3, w4, freq):
    z, t = filter_features(L)
    fr = freq.astype(jnp.float32)
    h = jnp.sin(fr * (z @ w1.astype(jnp.float32) + b1.astype(jnp.float32)))
    h = jnp.sin(fr * (h @ w2.astype(jnp.float32) + b2.astype(jnp.float32)))
    h = jnp.sin(fr * (h @ w3.astype(jnp.float32) + b3.astype(jnp.float32)))
    k = h @ w4.astype(jnp.float32)
    deltas = jnp.abs(jnp.linspace(MIN_DECAY, MAX_DECAY, HYENA_WIDTH, dtype=jnp.float32))
    decay = jnp.exp(-t * deltas[None, :])
    k = k.reshape(L, 2, HYENA_WIDTH) * decay[:, None, :]
    return k[:, 0], k[:, 1]


def bidir_long_conv(u, k_fwd, k_bwd, d_skip):
    B, L, C = u.shape
    n = 2 * L
    k_two = jnp.concatenate([
        k_fwd.at[0].add(k_bwd[0]),
        jnp.zeros((1, C), jnp.float32),
        k_bwd[1:][::-1],
    ], axis=0)
    uf32 = u.astype(jnp.float32)
    u_hat = jnp.fft.rfft(uf32, n=n, axis=1)
    k_hat = jnp.fft.rfft(k_two, n=n, axis=0)
    y = jnp.fft.irfft(u_hat * k_hat[None], n=n, axis=1)[:, :L]
    return (y + uf32 * d_skip.astype(jnp.float32)).astype(u.dtype)


def multiscale_pool_mixer(u, w_grp, b_grp, scale):
    B, L, C = u.shape
    uf = u.astype(jnp.float32)
    csum = jnp.concatenate([jnp.zeros((B, 1, C), jnp.float32), jnp.cumsum(uf, axis=1)], axis=1)
    pos = jnp.arange(L)
    outs = []
    for g, win in enumerate(POOL_WINDOWS):
        sl = slice(g * POOL_GROUP_WIDTH, (g + 1) * POOL_GROUP_WIDTH)
        lo = jnp.clip(pos - win // 2, 0, L)
        hi = jnp.clip(pos + (win - win // 2), 0, L)
        c = csum[:, :, sl]
        s = jnp.take(c, hi, axis=1) - jnp.take(c, lo, axis=1)
        cnt = (hi - lo).astype(jnp.float32)[None, :, None]
        outs.append(s / cnt - uf[:, :, sl])
    pooled = jnp.concatenate(outs, axis=-1).reshape(B, L, POOL_GROUPS, POOL_GROUP_WIDTH).astype(u.dtype)
    y = jnp.einsum('blgc,gcd->blgd', pooled, w_grp).reshape(B, L, C) + b_grp
    return y * scale


def setup_inputs(seed: int = 0) -> dict:
    key = jax.random.key(seed)
    ks = jax.random.split(key, 24)
    f32 = jnp.float32
    n = lambda k, shape, s: jax.random.normal(k, shape, f32) * s
    Dh, Dp, D, Hf = HYENA_WIDTH, POOL_WIDTH, D_MODEL, FILTER_HIDDEN
    return {
        "x": jax.random.normal(ks[0], (BATCH, SEQ, D), f32),
        "g_norm": 1.0 + n(ks[1], (DEPTH, D), 0.05),
        "w_in": n(ks[2], (DEPTH, D, PROJ_COLS), D ** -0.5),
        "b_in": n(ks[3], (DEPTH, PROJ_COLS), 0.02),
        "conv_w": n(ks[4], (DEPTH, SHORT_CONV, HY_IN), SHORT_CONV ** -0.5),
        "conv_b": n(ks[5], (DEPTH, HY_IN), 0.02),
        "filt_w1": n(ks[6], (DEPTH, FILTER_EMB, Hf), FILTER_EMB ** -0.5),
        "filt_b1": n(ks[7], (DEPTH, Hf), 0.1),
        "filt_w2": n(ks[8], (DEPTH, Hf, Hf), Hf ** -0.5),
        "filt_b2": n(ks[9], (DEPTH, Hf), 0.1),
        "filt_w3": n(ks[10], (DEPTH, Hf, Hf), Hf ** -0.5),
        "filt_b3": n(ks[11], (DEPTH, Hf), 0.1),
        "filt_w4": n(ks[12], (DEPTH, Hf, 2 * Dh), 0.05 * Hf ** -0.5),
        "filt_freq": 1.0 + n(ks[13], (DEPTH, Hf), 0.1),
        "hyena_d": n(ks[14], (DEPTH, Dh), 1.0),
        "w_hyena_out": n(ks[15], (DEPTH, Dh, D), Dh ** -0.5),
        "pool_w": n(ks[16], (DEPTH, POOL_GROUPS, POOL_GROUP_WIDTH, POOL_GROUP_WIDTH), POOL_GROUP_WIDTH ** -0.5),
        "pool_b": n(ks[17], (DEPTH, Dp), 0.02),
        "pool_scale": 1.0 + n(ks[18], (DEPTH, Dp), 0.1),
        "w_pool_out": n(ks[19], (DEPTH, Dp, D), Dp ** -0.5),
        "w_out": n(ks[20], (DEPTH, D, D), D ** -0.5),
        "g_final": 1.0 + n(ks[21], (D,), 0.05),
    }


def reference(x, g_norm, w_in, b_in, conv_w, conv_b, filt_w1, filt_b1, filt_w2, filt_b2,
              filt_w3, filt_b3, filt_w4, filt_freq, hyena_d, w_hyena_out, pool_w, pool_b,
              pool_scale, w_pool_out, w_out, g_final):
    B, L, D = x.shape
    Dh, Dp = HYENA_WIDTH, POOL_WIDTH
    o1 = HY_IN
    o2 = o1 + Dh
    o3 = o2 + Dp
    o4 = o3 + Dp
    for l in range(DEPTH):
        h = rms_norm(x, g_norm[l])
        proj = h @ w_in[l] + b_in[l]
        hy_in, hy_z = proj[..., :o1], proj[..., o1:o2]
        pl_in, pl_z = proj[..., o2:o3], proj[..., o3:o4]
        gates = jax.nn.sigmoid(proj[..., o4:].reshape(B, L, N_BRANCHES, D))

        uc = centred_short_conv(hy_in, conv_w[l], conv_b[l])
        x0, x1, v = uc[..., :Dh], uc[..., Dh:2 * Dh], uc[..., 2 * Dh:]
        k_fwd, k_bwd = implicit_filters(L, filt_w1[l], filt_b1[l], filt_w2[l], filt_b2[l],
                                        filt_w3[l], filt_b3[l], filt_w4[l], filt_freq[l])
        v = bidir_long_conv(v * x1, k_fwd, k_bwd, hyena_d[l])
        y_h = v * x0 * jax.nn.silu(hy_z)
        out_h = y_h @ w_hyena_out[l]

        y_p = multiscale_pool_mixer(pl_in, pool_w[l], pool_b[l], pool_scale[l]) * jax.nn.silu(pl_z)
        out_p = y_p @ w_pool_out[l]

        merged = gates[..., 0, :] * out_h + gates[..., 1, :] * out_p
        x = x + merged @ w_out[l]
    return rms_norm(x, g_final)
```

```python
import functools
import math

import jax
import jax.numpy as jnp
from jax import lax
from jax.experimental import pallas as pl
from jax.experimental.pallas import tpu as pltpu

SHORT_CONV = 3
FILTER_EMB = 33
FAST_DECAY_PCT = 0.3
SLOW_DECAY_PCT = 1.5
DECAY_TARGET = 1e-2
MAX_DECAY = math.log(DECAY_TARGET) / FAST_DECAY_PCT
MIN_DECAY = math.log(DECAY_TARGET) / SLOW_DECAY_PCT
POOL_WINDOWS = (2, 4, 8, 16)
NORM_EPS = 1e-6

LANES = 128
CH_BLOCK = 256
VMEM_LIMIT = 56 * 1024 * 1024

F32 = jnp.float32
BF16 = jnp.bfloat16


def _dot(a, b):
    return jnp.dot(a, b, preferred_element_type=F32)


def _dot_hi(a, b):
    return jnp.dot(a, b, preferred_element_type=F32, precision=lax.Precision.HIGHEST)


def _rms_scale(x):
    return x * lax.rsqrt(jnp.mean(x * x, axis=-1, keepdims=True) + NORM_EPS)


def _shift_down(x, w, row):
    return jnp.where(row >= w, pltpu.roll(x, w, 0), 0.0)


def _shift_up(x, w, row):
    n_rows = x.shape[0]
    return jnp.where(row < n_rows - w, pltpu.roll(x, n_rows - w, 0), 0.0)


def _inproj_kernel(x_ref, g_ref, w_ref, b_ref, o_ref, h_ref):
    @pl.when(pl.program_id(1) == 0)
    def _():
        h_ref[...] = (_rms_scale(x_ref[...]) * g_ref[...]).astype(BF16)

    o_ref[...] = (_dot(h_ref[...], w_ref[...]) + b_ref[...]).astype(BF16)


def _inproj(x2, g_norm, w_in_bf, b_in, tm=1024, tn=1024):
    n_tok, d = x2.shape
    cols = w_in_bf.shape[1]
    return pl.pallas_call(
        _inproj_kernel,
        grid=(n_tok // tm, cols // tn),
        in_specs=[
            pl.BlockSpec((tm, d), lambda i, j: (i, 0)),
            pl.BlockSpec((1, d), lambda i, j: (0, 0)),
            pl.BlockSpec((d, tn), lambda i, j: (0, j)),
            pl.BlockSpec((1, tn), lambda i, j: (0, j)),
        ],
        out_specs=pl.BlockSpec((tm, tn), lambda i, j: (i, j)),
        out_shape=jax.ShapeDtypeStruct((n_tok, cols), BF16),
        scratch_shapes=[pltpu.VMEM((tm, d), BF16)],
        compiler_params=pltpu.CompilerParams(
            dimension_semantics=("parallel", "arbitrary"),
            vmem_limit_bytes=VMEM_LIMIT),
        name="inproj",
    )(x2, g_norm, w_in_bf, b_in)


def _filter_kernel(z_ref, w1_ref, b1_ref, w2_ref, b2_ref, w3_ref, b3_ref, fr_ref,
                   w4f_ref, w4b_ref, dl_ref, cq_ref, sq_ref, kre_ref, kim_ref):
    seq = z_ref.shape[0]
    n = 2 * seq
    z = z_ref[...]
    fr = fr_ref[...]
    h = jnp.sin(fr * (_dot_hi(z, w1_ref[...]) + b1_ref[...]))
    h = jnp.sin(fr * (_dot_hi(h, w2_ref[...]) + b2_ref[...]))
    h = jnp.sin(fr * (_dot_hi(h, w3_ref[...]) + b3_ref[...]))
    t = z[:, 0:1]
    decay = jnp.exp(-t * dl_ref[...])
    kf = _dot_hi(h, w4f_ref[...]) * decay
    kb = _dot_hi(h, w4b_ref[...]) * decay
    ks = (kf + kb).astype(BF16)
    kd = (kf - kb).astype(BF16)
    cq = cq_ref[...]
    sq = sq_ref[...]
    f_idx = lax.broadcasted_iota(jnp.int32, kf.shape, 0).astype(F32)
    phi = (f_idx + 0.5) * (math.pi / n)
    c_phi = jnp.cos(phi) * (2.0 / n)
    s_phi = jnp.sin(phi) * (2.0 / n)
    kre_ref[...] = c_phi * _dot(cq, ks) + s_phi * _dot(sq, ks)
    kim_ref[...] = s_phi * _dot(cq, kd) - c_phi * _dot(sq, kd)


def _filter_spectra(z_pad, w1_pad, b1, w2, b2, w3, b3, freq, w4, deltas, cq, sq):
    seq = z_pad.shape[0]
    hf = w2.shape[0]
    dh = deltas.shape[1]
    nblk = dh // CH_BLOCK
    full = lambda a: pl.BlockSpec(a.shape, lambda c: (0,) * a.ndim)
    resident = lambda a: pl.BlockSpec(a.shape, lambda c: (0, 0), pipeline_mode=pl.Buffered(1))
    return pl.pallas_call(
        _filter_kernel,
        grid=(nblk,),
        in_specs=[
            full(z_pad), full(w1_pad), full(b1), full(w2), full(b2), full(w3), full(b3), full(freq),
            pl.BlockSpec((hf, CH_BLOCK), lambda c: (0, c)),
            pl.BlockSpec((hf, CH_BLOCK), lambda c: (0, nblk + c)),
            pl.BlockSpec((1, CH_BLOCK), lambda c: (0, c)),
            resident(cq), resident(sq),
        ],
        out_specs=[pl.BlockSpec((seq, CH_BLOCK), lambda c: (0, c))] * 2,
        out_shape=[jax.ShapeDtypeStruct((seq, dh), F32)] * 2,
        compiler_params=pltpu.CompilerParams(
            dimension_semantics=("arbitrary",), vmem_limit_bytes=VMEM_LIMIT),
        name="filter_spectra",
    )(z_pad, w1_pad, b1, w2, b2, w3, b3, freq, w4, w4, deltas, cq, sq)


def _hyena_kernel(x0_ref, x1_ref, v_ref, z_ref, cw0_ref, cw1_ref, cw2_ref,
                  cb0_ref, cb1_ref, cb2_ref, d_ref, kre_ref, kim_ref, cq_ref, sq_ref, y_ref):
    row = lax.broadcasted_iota(jnp.int32, x0_ref.shape, 0)

    def short_conv(u_ref, w_ref, b_ref):
        u = u_ref[...].astype(F32)
        w = w_ref[...]
        return (_shift_down(u, 1, row) * w[0:1] + u * w[1:2] + _shift_up(u, 1, row) * w[2:3]
                + b_ref[...])

    x0 = short_conv(x0_ref, cw0_ref, cb0_ref)
    x1 = short_conv(x1_ref, cw1_ref, cb1_ref)
    v = short_conv(v_ref, cw2_ref, cb2_ref)
    u = v * x1
    ub = u.astype(BF16)
    cq = cq_ref[...]
    sq = sq_ref[...]
    ur = _dot(cq, ub)
    vi = _dot(sq, ub)
    kre = kre_ref[...]
    kim = kim_ref[...]
    p = (kre * ur + kim * vi).astype(BF16)
    q = (kre * vi - kim * ur).astype(BF16)
    y = _dot(cq, p) + _dot(sq, q)
    y = y + u * d_ref[...]
    y_ref[...] = (y * x0 * jax.nn.silu(z_ref[...].astype(F32))).astype(BF16)


def _hyena(proj, conv_w, conv_b, hyena_d, kre, kim, cq, sq, batch, seq):
    dh = hyena_d.shape[1]
    nblk = dh // CH_BLOCK
    taps = conv_w.shape[0]
    act = lambda part: pl.BlockSpec((seq, CH_BLOCK), lambda c, b: (b, part * nblk + c))
    cwspec = lambda part: pl.BlockSpec((taps, CH_BLOCK), lambda c, b: (0, part * nblk + c))
    cbspec = lambda part: pl.BlockSpec((1, CH_BLOCK), lambda c, b: (0, part * nblk + c))
    chan = lambda rows: pl.BlockSpec((rows, CH_BLOCK), lambda c, b: (0, c))
    resident = lambda a: pl.BlockSpec(a.shape, lambda c, b: (0, 0), pipeline_mode=pl.Buffered(1))
    return pl.pallas_call(
        _hyena_kernel,
        grid=(nblk, batch),
        in_specs=[
            act(0), act(1), act(2), act(3),
            cwspec(0), cwspec(1), cwspec(2),
            cbspec(0), cbspec(1), cbspec(2),
            chan(1), chan(seq), chan(seq),
            resident(cq), resident(sq),
        ],
        out_specs=pl.BlockSpec((seq, CH_BLOCK), lambda c, b: (b, c)),
        out_shape=jax.ShapeDtypeStruct((batch * seq, dh), BF16),
        compiler_params=pltpu.CompilerParams(
            dimension_semantics=("arbitrary", "arbitrary"), vmem_limit_bytes=VMEM_LIMIT),
        name="hyena",
    )(proj, proj, proj, proj, conv_w, conv_w, conv_w, conv_b, conv_b, conv_b,
      hyena_d, kre, kim, cq, sq)


def _pool_kernel(u_ref, z_ref, w_ref, b_ref, s_ref, y_ref):
    seq = u_ref.shape[0]
    gw = w_ref.shape[1]
    row = lax.broadcasted_iota(jnp.int32, (seq, gw), 0)
    for g, win in enumerate(POOL_WINDOWS):
        cols = slice(g * gw, (g + 1) * gw)
        half = win // 2
        u = u_ref[:, cols].astype(F32)
        fwd = u
        bwd = u
        w = 1
        while w < half:
            fwd = fwd + _shift_up(fwd, w, row)
            bwd = bwd + _shift_down(bwd, w, row)
            w *= 2
        total = fwd + _shift_down(bwd, 1, row)
        cnt = (jnp.minimum(row + half, seq) - jnp.maximum(row - half, 0)).astype(F32)
        pooled = (total / cnt - u).astype(BF16)
        y = _dot(pooled, w_ref[g]) + b_ref[:, cols]
        y = y * s_ref[:, cols] * jax.nn.silu(z_ref[:, cols].astype(F32))
        y_ref[:, cols] = y.astype(BF16)


def _pool(proj, pool_w_bf, pool_b, pool_scale, batch, seq, col_in, col_z):
    dp = pool_b.shape[1]
    full = lambda a: pl.BlockSpec(a.shape, lambda b: (0,) * a.ndim)
    return pl.pallas_call(
        _pool_kernel,
        grid=(batch,),
        in_specs=[
            pl.BlockSpec((seq, dp), lambda b: (b, col_in)),
            pl.BlockSpec((seq, dp), lambda b: (b, col_z)),
            full(pool_w_bf), full(pool_b), full(pool_scale),
        ],
        out_specs=pl.BlockSpec((seq, dp), lambda b: (b, 0)),
        out_shape=jax.ShapeDtypeStruct((batch * seq, dp), BF16),
        compiler_params=pltpu.CompilerParams(
            dimension_semantics=("parallel",), vmem_limit_bytes=VMEM_LIMIT),
        name="pool",
    )(proj, proj, pool_w_bf, pool_b, pool_scale)


def _merge_kernel(yh_ref, yp_ref, g0_ref, g1_ref, x_ref, wh_ref, wp_ref, wo_ref, gf_ref, o_ref):
    out_h = _dot(yh_ref[...], wh_ref[...])
    out_p = _dot(yp_ref[...], wp_ref[...])
    merged = (jax.nn.sigmoid(g0_ref[...].astype(F32)) * out_h
              + jax.nn.sigmoid(g1_ref[...].astype(F32)) * out_p)
    res = x_ref[...] + _dot(merged.astype(BF16), wo_ref[...])
    o_ref[...] = _rms_scale(res) * gf_ref[...]


def _merge(yh, yp, proj, x2, wh_bf, wp_bf, wo_bf, g_final, col_g0, tm=512):
    n_tok, d = x2.shape
    tile = lambda col: pl.BlockSpec((tm, d), lambda i: (i, col))
    resident = lambda a: pl.BlockSpec(a.shape, lambda i: (0, 0), pipeline_mode=pl.Buffered(1))
    return pl.pallas_call(
        _merge_kernel,
        grid=(n_tok // tm,),
        in_specs=[
            tile(0), tile(0), tile(col_g0), tile(col_g0 + 1), tile(0),
            resident(wh_bf), resident(wp_bf), resident(wo_bf),
            pl.BlockSpec((1, d), lambda i: (0, 0)),
        ],
        out_specs=tile(0),
        out_shape=jax.ShapeDtypeStruct((n_tok, d), F32),
        compiler_params=pltpu.CompilerParams(
            dimension_semantics=("parallel",), vmem_limit_bytes=VMEM_LIMIT),
        name="merge",
    )(yh, yp, proj, proj, x2, wh_bf, wp_bf, wo_bf, g_final)


def _transform_matrices(seq):
    period = 8 * seq
    odd = 2 * jnp.arange(seq, dtype=jnp.int32) + 1
    m = (odd[:, None] * odd[None, :]) % period
    ang = m.astype(F32) * (2.0 * math.pi / period)
    return jnp.cos(ang).astype(BF16), jnp.sin(ang).astype(BF16)


def _filter_features(seq):
    t = jnp.linspace(0.0, 1.0, seq, dtype=F32)[:, None]
    bands = (FILTER_EMB - 1) // 2
    w = 2.0 * math.pi * jnp.arange(seq, dtype=F32) / seq
    f = jnp.linspace(1e-4, bands - 1, bands, dtype=F32)
    ang = w[:, None] * f[None, :]
    return jnp.concatenate([t, jnp.cos(ang), -jnp.sin(ang)], axis=-1)


def kernel(x, g_norm, w_in, b_in, conv_w, conv_b, filt_w1, filt_b1, filt_w2, filt_b2, filt_w3, filt_b3, filt_w4, filt_freq, hyena_d, w_hyena_out, pool_w, pool_b, pool_scale, w_pool_out, w_out, g_final):
    batch, seq, d = x.shape
    depth = g_norm.shape[0]
    dh = hyena_d.shape[1]
    dp = pool_b.shape[1]
    hy_in = conv_b.shape[1]
    assert d == dh == dp and d % CH_BLOCK == 0 and hy_in == 3 * dh
    col_pool_in = (hy_in + dh) // dp
    col_gate = (hy_in + dh + 2 * dp) // d

    cq, sq = _transform_matrices(seq)
    z_pad = jnp.pad(_filter_features(seq), ((0, 0), (0, LANES - FILTER_EMB)))
    deltas = jnp.abs(jnp.linspace(MIN_DECAY, MAX_DECAY, dh, dtype=F32))[None, :]

    x2 = x.reshape(batch * seq, d)
    for l in range(depth):
        row = lambda a: a[l][None, :]
        w1_pad = jnp.pad(filt_w1[l], ((0, LANES - FILTER_EMB), (0, 0)))
        kre, kim = _filter_spectra(
            z_pad, w1_pad, row(filt_b1), filt_w2[l], row(filt_b2), filt_w3[l], row(filt_b3),
            row(filt_freq), filt_w4[l], deltas, cq, sq)
        proj = _inproj(x2, row(g_norm), w_in[l].astype(BF16), row(b_in))
        yh = _hyena(proj, conv_w[l], row(conv_b), row(hyena_d), kre, kim, cq, sq, batch, seq)
        yp = _pool(proj, pool_w[l].astype(BF16), row(pool_b), row(pool_scale), batch, seq,
                   col_pool_in, col_pool_in + 1)
        last = l == depth - 1
        gf = g_final[None, :] if last else jnp.ones((1, d), F32)
        x2 = _merge(yh, yp, proj, x2, w_hyena_out[l].astype(BF16), w_pool_out[l].astype(BF16),
                    w_out[l].astype(BF16), gf, col_gate)
        assert last
    return x2.reshape(batch, seq, d)
```

```python
import functools
import math

import jax
import jax.numpy as jnp
from jax import lax
from jax.experimental import pallas as pl
from jax.experimental.pallas import tpu as pltpu

FILTER_EMB = 33
FAST_DECAY_PCT = 0.3
SLOW_DECAY_PCT = 1.5
DECAY_TARGET = 1e-2
MAX_DECAY = math.log(DECAY_TARGET) / FAST_DECAY_PCT
MIN_DECAY = math.log(DECAY_TARGET) / SLOW_DECAY_PCT
POOL_WINDOWS = (2, 4, 8, 16)
NORM_EPS = 1e-6

LANES = 128
SUBLANES = 8
MXU_DIM = 256
HYENA_COLS = 2 * MXU_DIM
ROW_CHUNK = 512
DFT_ROW_BLOCK = 64
VMEM_LIMIT = 56 * 1024 * 1024

F32 = jnp.float32
BF16 = jnp.bfloat16


def _dot(a, b):
    return jnp.dot(a, b, preferred_element_type=F32)


def _dot_hi(a, b):
    return jnp.dot(a, b, preferred_element_type=F32, precision=lax.Precision.HIGHEST)


def _rms_scale(x):
    return x * lax.rsqrt(jnp.mean(x * x, axis=-1, keepdims=True) + NORM_EPS)


def _shift_down(x, w, row):
    return jnp.where(row >= w, pltpu.roll(x, w, 0), 0.0)


def _shift_up(x, w, row):
    n_rows = x.shape[0]
    return jnp.where(row < n_rows - w, pltpu.roll(x, n_rows - w, 0), 0.0)


def _chunk(i, size):
    return pl.ds(pl.multiple_of(i * size, size), size)


def _dft_kernel(ca_ref, sa_ref, cb_ref, sb_ref, o_ref):
    seq = cb_ref.shape[1]
    ca = ca_ref[0]
    sa = sa_ref[0]
    cb = cb_ref[...]
    sb = sb_ref[...]
    o_ref[:, 0:seq] = (ca * cb - sa * sb).astype(BF16)
    o_ref[:, seq:2 * seq] = (sa * cb + ca * sb).astype(BF16)


def _transform_matrices(seq):
    period = 8 * seq
    rb = DFT_ROW_BLOCK
    odd_t = 2 * jnp.arange(seq, dtype=jnp.int32) + 1
    hi = (2 * rb) * jnp.arange(seq // rb, dtype=jnp.int32)
    lo = 2 * jnp.arange(rb, dtype=jnp.int32) + 1
    ang = lambda m: (m % period).astype(F32) * (2.0 * math.pi / period)
    ang_hi = ang(hi[:, None] * odd_t[None, :])[:, None, :]
    ang_lo = ang(lo[:, None] * odd_t[None, :])
    hi_spec = pl.BlockSpec((1, 1, seq), lambda i: (i, 0, 0))
    lo_spec = pl.BlockSpec((rb, seq), lambda i: (0, 0))
    return pl.pallas_call(
        _dft_kernel,
        grid=(seq // rb,),
        in_specs=[hi_spec, hi_spec, lo_spec, lo_spec],
        out_specs=pl.BlockSpec((rb, 2 * seq), lambda i: (i, 0)),
        out_shape=jax.ShapeDtypeStruct((seq, 2 * seq), BF16),
        compiler_params=pltpu.CompilerParams(dimension_semantics=("parallel",)),
        name="dft_matrices",
    )(jnp.cos(ang_hi), jnp.sin(ang_hi), jnp.cos(ang_lo), jnp.sin(ang_lo))


def _filter_kernel(z_ref, w1_ref, b1_ref, w2_ref, b2_ref, w3_ref, b3_ref, fr_ref,
                   w4f_ref, w4b_ref, dl_ref, cs_ref, kre_ref, kim_ref,
                   h_ref, cphi_ref, sphi_ref, kk_ref):
    seq = z_ref.shape[0]
    n = 2 * seq
    cb = dl_ref.shape[1]

    @pl.when(pl.program_id(0) == 0)
    def _():
        fr = fr_ref[...]
        h = jnp.sin(fr * (_dot_hi(z_ref[...], w1_ref[...]) + b1_ref[...]))
        h = jnp.sin(fr * (_dot_hi(h, w2_ref[...]) + b2_ref[...]))
        h_ref[...] = jnp.sin(fr * (_dot_hi(h, w3_ref[...]) + b3_ref[...]))
        f_idx = lax.broadcasted_iota(jnp.int32, cphi_ref.shape, 0).astype(F32)
        phi = (f_idx + 0.5) * (math.pi / n)
        cphi_ref[...] = jnp.cos(phi) * (2.0 / n)
        sphi_ref[...] = jnp.sin(phi) * (2.0 / n)

    h = h_ref[...]
    decay = jnp.exp(-z_ref[:, 0:1] * dl_ref[...])
    kf = _dot_hi(h, w4f_ref[...]) * decay
    kb = _dot_hi(h, w4b_ref[...]) * decay
    kk_ref[:, 0:cb] = (kf + kb).astype(BF16)
    kk_ref[:, cb:2 * cb] = (kf - kb).astype(BF16)

    def body(i, carry):
        rows = _chunk(i, ROW_CHUNK)
        kk = kk_ref[...]
        a = _dot(cs_ref[rows, 0:seq], kk)
        b = _dot(cs_ref[rows, seq:n], kk)
        reps = cb // LANES
        c_phi = jnp.concatenate([cphi_ref[rows, :]] * reps, axis=1)
        s_phi = jnp.concatenate([sphi_ref[rows, :]] * reps, axis=1)
        kre_ref[rows, :] = c_phi * a[:, 0:cb] + s_phi * b[:, 0:cb]
        kim_ref[rows, :] = s_phi * a[:, cb:2 * cb] - c_phi * b[:, cb:2 * cb]
        return carry

    lax.fori_loop(0, seq // ROW_CHUNK, body, 0)


def _filter_spectra(z_pad, w1_pad, b1, w2, b2, w3, b3, freq, w4, deltas, cs):
    seq = z_pad.shape[0]
    hf = w2.shape[0]
    dh = deltas.shape[1]
    cb = MXU_DIM
    nblk = dh // cb
    full = lambda a: pl.BlockSpec(a.shape, lambda c: (0,) * a.ndim)
    return pl.pallas_call(
        _filter_kernel,
        grid=(nblk,),
        in_specs=[
            full(z_pad), full(w1_pad), full(b1), full(w2), full(b2), full(w3), full(b3), full(freq),
            pl.BlockSpec((hf, cb), lambda c: (0, c)),
            pl.BlockSpec((hf, cb), lambda c: (0, nblk + c)),
            pl.BlockSpec((1, cb), lambda c: (0, c)),
            pl.BlockSpec(cs.shape, lambda c: (0, 0), pipeline_mode=pl.Buffered(1)),
        ],
        out_specs=[pl.BlockSpec((seq, cb), lambda c: (0, c))] * 2,
        out_shape=[jax.ShapeDtypeStruct((seq, dh), F32)] * 2,
        scratch_shapes=[
            pltpu.VMEM((seq, hf), F32),
            pltpu.VMEM((seq, LANES), F32),
            pltpu.VMEM((seq, LANES), F32),
            pltpu.VMEM((seq, 2 * cb), BF16),
        ],
        compiler_params=pltpu.CompilerParams(
            dimension_semantics=("arbitrary",), vmem_limit_bytes=VMEM_LIMIT),
        name="filter_spectra",
    )(z_pad, w1_pad, b1, w2, b2, w3, b3, freq, w4, w4, deltas, cs)


def _conv3(p, w, bias):
    n_rows = p.shape[0]
    dn = pltpu.roll(p, 1, 0)
    up = pltpu.roll(p, n_rows - 1, 0)
    w0, w1, w2 = w[0:1], w[1:2], w[2:3]
    mid = dn * w0 + p * w1 + up * w2 + bias
    edge = lax.broadcasted_iota(jnp.int32, (SUBLANES, p.shape[1]), 0)
    s = SUBLANES
    first = (jnp.where(edge == 0, 0.0, dn[0:s]) * w0 + p[0:s] * w1 + up[0:s] * w2 + bias)
    last = (dn[n_rows - s:] * w0 + p[n_rows - s:] * w1
            + jnp.where(edge == s - 1, 0.0, up[n_rows - s:]) * w2 + bias)
    return jnp.concatenate([first, mid[s:n_rows - s], last], axis=0)


def _inproj_kernel(x_ref, g_ref, wa_ref, wb_ref, ba_ref, bb_ref, cwa_ref, cwb_ref,
                   cba_ref, cbb_ref, u_ref, gate_ref, p_ref, h_ref, *, nblk):
    j = pl.program_id(1)
    cb = wa_ref.shape[1]

    @pl.when(j == 0)
    def _():
        h_ref[...] = (_rms_scale(x_ref[...]) * g_ref[...]).astype(BF16)

    def project():
        w = jnp.concatenate([wa_ref[...].astype(BF16), wb_ref[...].astype(BF16)], axis=1)
        res = _dot(h_ref[...], w)
        return res[:, 0:cb] + ba_ref[...], res[:, cb:2 * cb] + bb_ref[...]

    @pl.when(j < nblk)
    def _():
        pa, pb = project()
        u_ref[...] = (_conv3(pa, cwa_ref[...], cba_ref[...])
                      * _conv3(pb, cwb_ref[...], cbb_ref[...]))

    @pl.when(jnp.logical_and(j >= nblk, j < 2 * nblk))
    def _():
        pa, pb = project()
        gate_ref[...] = (_conv3(pa, cwa_ref[...], cba_ref[...]) * jax.nn.silu(pb)).astype(BF16)

    @pl.when(j >= 2 * nblk)
    def _():
        pa, pb = project()
        p_ref[:, 0:cb] = pa.astype(BF16)
        p_ref[:, cb:2 * cb] = pb.astype(BF16)


def _inproj(x2, g_norm, w_in, b_in, conv_w, conv_b, batch, seq):
    d = x2.shape[1]
    cols = w_in.shape[1]
    cb = MXU_DIM
    nblk = conv_b.shape[1] // (3 * cb)
    n_plain = (cols - 4 * nblk * cb) // (2 * cb)
    n_steps = 2 * nblk + n_plain
    taps = conv_w.shape[0]

    def col_a(j):
        return jnp.where(j < nblk, nblk + j,
                         jnp.where(j < 2 * nblk, j - nblk, 4 * nblk + 2 * (j - 2 * nblk)))

    def col_b(j):
        return jnp.where(j < nblk, 2 * nblk + j,
                         jnp.where(j < 2 * nblk, 2 * nblk + j, 4 * nblk + 2 * (j - 2 * nblk) + 1))

    def conv_a(j):
        return jnp.where(j < nblk, nblk + j, jnp.where(j < 2 * nblk, j - nblk, 0))

    def conv_b_idx(j):
        return jnp.where(j < nblk, 2 * nblk + j, 0)

    clip = lambda v, hi: jnp.minimum(jnp.maximum(v, 0), hi)
    return pl.pallas_call(
        functools.partial(_inproj_kernel, nblk=nblk),
        grid=(batch, n_steps),
        in_specs=[
            pl.BlockSpec((seq, d), lambda b, j: (b, 0)),
            pl.BlockSpec((1, d), lambda b, j: (0, 0)),
            pl.BlockSpec((d, cb), lambda b, j: (0, col_a(j))),
            pl.BlockSpec((d, cb), lambda b, j: (0, col_b(j))),
            pl.BlockSpec((1, cb), lambda b, j: (0, col_a(j))),
            pl.BlockSpec((1, cb), lambda b, j: (0, col_b(j))),
            pl.BlockSpec((taps, cb), lambda b, j: (0, conv_a(j))),
            pl.BlockSpec((taps, cb), lambda b, j: (0, conv_b_idx(j))),
            pl.BlockSpec((1, cb), lambda b, j: (0, conv_a(j))),
            pl.BlockSpec((1, cb), lambda b, j: (0, conv_b_idx(j))),
        ],
        out_specs=[
            pl.BlockSpec((seq, cb), lambda b, j: (b, clip(j, nblk - 1))),
            pl.BlockSpec((seq, cb), lambda b, j: (b, clip(j - nblk, nblk - 1))),
            pl.BlockSpec((seq, 2 * cb), lambda b, j: (b, clip(j - 2 * nblk, n_plain - 1))),
        ],
        out_shape=[
            jax.ShapeDtypeStruct((batch * seq, nblk * cb), F32),
            jax.ShapeDtypeStruct((batch * seq, nblk * cb), BF16),
            jax.ShapeDtypeStruct((batch * seq, n_plain * 2 * cb), BF16),
        ],
        scratch_shapes=[pltpu.VMEM((seq, d), BF16)],
        compiler_params=pltpu.CompilerParams(
            dimension_semantics=("arbitrary", "arbitrary"), vmem_limit_bytes=VMEM_LIMIT),
        name="inproj",
    )(x2, g_norm, w_in, w_in, b_in, b_in, conv_w, conv_w, conv_b, conv_b)


def _hyena_kernel(u_ref, gate_ref, d_ref, kre_ref, kim_ref, cs_ref, y_ref, ub_ref, pq_ref):
    seq = u_ref.shape[0]
    n = 2 * seq
    n_chunks = seq // ROW_CHUNK

    def to_bf16(i, carry):
        rows = _chunk(i, ROW_CHUNK)
        ub_ref[rows, :] = u_ref[rows, :].astype(BF16)
        return carry

    lax.fori_loop(0, n_chunks, to_bf16, 0)

    def forward(i, carry):
        rows = _chunk(i, ROW_CHUNK)
        ub = ub_ref[...]
        ur = _dot(cs_ref[rows, 0:seq], ub)
        vi = _dot(cs_ref[rows, seq:n], ub)
        kre = kre_ref[rows, :]
        kim = kim_ref[rows, :]
        pq_ref[rows, :] = (kre * ur + kim * vi).astype(BF16)
        pq_ref[pl.ds(pl.multiple_of(seq + i * ROW_CHUNK, ROW_CHUNK), ROW_CHUNK), :] = (
            kre * vi - kim * ur).astype(BF16)
        return carry

    lax.fori_loop(0, n_chunks, forward, 0)

    def inverse(i, carry):
        rows = _chunk(i, ROW_CHUNK)
        y = _dot(cs_ref[rows, :], pq_ref[...])
        y = y + u_ref[rows, :] * d_ref[...]
        y_ref[rows, :] = (y * gate_ref[rows, :].astype(F32)).astype(BF16)
        return carry

    lax.fori_loop(0, n_chunks, inverse, 0)


def _hyena(u, gate, hyena_d, kre, kim, cs, batch, seq):
    dh = hyena_d.shape[1]
    cw = HYENA_COLS
    act = pl.BlockSpec((seq, cw), lambda c, b: (b, c))
    once = lambda a, imap: pl.BlockSpec(a, imap, pipeline_mode=pl.Buffered(1))
    return pl.pallas_call(
        _hyena_kernel,
        grid=(dh // cw, batch),
        in_specs=[
            act, act,
            pl.BlockSpec((1, cw), lambda c, b: (0, c)),
            once((seq, cw), lambda c, b: (0, c)),
            once((seq, cw), lambda c, b: (0, c)),
            once(cs.shape, lambda c, b: (0, 0)),
        ],
        out_specs=act,
        out_shape=jax.ShapeDtypeStruct((batch * seq, dh), BF16),
        scratch_shapes=[pltpu.VMEM((seq, cw), BF16), pltpu.VMEM((2 * seq, cw), BF16)],
        compiler_params=pltpu.CompilerParams(
            dimension_semantics=("arbitrary", "arbitrary"), vmem_limit_bytes=VMEM_LIMIT),
        name="hyena",
    )(u, gate, hyena_d, kre, kim, cs)


def _pool_kernel(u_ref, z_ref, w_ref, b_ref, s_ref, y_ref):
    seq = u_ref.shape[0]
    gw = w_ref.shape[1]
    row = lax.broadcasted_iota(jnp.int32, (seq, gw), 0)
    for g, win in enumerate(POOL_WINDOWS):
        cols = slice(g * gw, (g + 1) * gw)
        half = win // 2
        u = u_ref[:, cols].astype(F32)
        fwd = u
        bwd = u
        w = 1
        while w < half:
            fwd = fwd + _shift_up(fwd, w, row)
            bwd = bwd + _shift_down(bwd, w, row)
            w *= 2
        total = fwd + _shift_down(bwd, 1, row)
        cnt = (jnp.minimum(row + half, seq) - jnp.maximum(row - half, 0)).astype(F32)
        pooled = (total / cnt - u).astype(BF16)
        y = _dot(pooled, w_ref[g]) + b_ref[:, cols]
        y = y * s_ref[:, cols] * jax.nn.silu(z_ref[:, cols].astype(F32))
        y_ref[:, cols] = y.astype(BF16)


def _pool(plain, pool_w_bf, pool_b, pool_scale, batch, seq):
    dp = pool_b.shape[1]
    full = lambda a: pl.BlockSpec(a.shape, lambda b: (0,) * a.ndim)
    return pl.pallas_call(
        _pool_kernel,
        grid=(batch,),
        in_specs=[
            pl.BlockSpec((seq, dp), lambda b: (b, 0)),
            pl.BlockSpec((seq, dp), lambda b: (b, 1)),
            full(pool_w_bf), full(pool_b), full(pool_scale),
        ],
        out_specs=pl.BlockSpec((seq, dp), lambda b: (b, 0)),
        out_shape=jax.ShapeDtypeStruct((batch * seq, dp), BF16),
        compiler_params=pltpu.CompilerParams(
            dimension_semantics=("parallel",), vmem_limit_bytes=VMEM_LIMIT),
        name="pool",
    )(plain, plain, pool_w_bf, pool_b, pool_scale)


def _merge_kernel(yh_ref, yp_ref, g0_ref, g1_ref, x_ref, wh_ref, wp_ref, wo_ref, gf_ref, o_ref):
    out_h = _dot(yh_ref[...], wh_ref[...])
    out_p = _dot(yp_ref[...], wp_ref[...])
    merged = (jax.nn.sigmoid(g0_ref[...].astype(F32)) * out_h
              + jax.nn.sigmoid(g1_ref[...].astype(F32)) * out_p)
    res = x_ref[...] + _dot(merged.astype(BF16), wo_ref[...])
    o_ref[...] = _rms_scale(res) * gf_ref[...]


def _merge(yh, yp, plain, x2, wh_bf, wp_bf, wo_bf, g_final, tm=512):
    n_tok, d = x2.shape
    tile = lambda col: pl.BlockSpec((tm, d), lambda i: (i, col))
    resident = lambda a: pl.BlockSpec(a.shape, lambda i: (0, 0), pipeline_mode=pl.Buffered(1))
    return pl.pallas_call(
        _merge_kernel,
        grid=(n_tok // tm,),
        in_specs=[
            tile(0), tile(0), tile(2), tile(3), tile(0),
            resident(wh_bf), resident(wp_bf), resident(wo_bf),
            pl.BlockSpec((1, d), lambda i: (0, 0)),
        ],
        out_specs=tile(0),
        out_shape=jax.ShapeDtypeStruct((n_tok, d), F32),
        compiler_params=pltpu.CompilerParams(
            dimension_semantics=("parallel",), vmem_limit_bytes=VMEM_LIMIT),
        name="merge",
    )(yh, yp, plain, plain, x2, wh_bf, wp_bf, wo_bf, g_final)


def _filter_features(seq):
    t = jnp.linspace(0.0, 1.0, seq, dtype=F32)[:, None]
    bands = (FILTER_EMB - 1) // 2
    w = 2.0 * math.pi * jnp.arange(seq, dtype=F32) / seq
    f = jnp.linspace(1e-4, bands - 1, bands, dtype=F32)
    ang = w[:, None] * f[None, :]
    return jnp.concatenate([t, jnp.cos(ang), -jnp.sin(ang)], axis=-1)


def kernel(x, g_norm, w_in, b_in, conv_w, conv_b, filt_w1, filt_b1, filt_w2, filt_b2, filt_w3, filt_b3, filt_w4, filt_freq, hyena_d, w_hyena_out, pool_w, pool_b, pool_scale, w_pool_out, w_out, g_final):
    batch, seq, d = x.shape
    dh = hyena_d.shape[1]
    dp = pool_b.shape[1]
    assert g_norm.shape[0] == 1, "single-layer block"
    assert d == dh == dp and conv_b.shape[1] == 3 * dh and w_in.shape[2] == 4 * dh + 2 * dp + 2 * d
    assert dh % HYENA_COLS == 0 and seq % ROW_CHUNK == 0 and seq % DFT_ROW_BLOCK == 0

    cs = _transform_matrices(seq)
    z_pad = jnp.pad(_filter_features(seq), ((0, 0), (0, LANES - FILTER_EMB)))
    w1_pad = jnp.pad(filt_w1[0], ((0, LANES - FILTER_EMB), (0, 0)))
    deltas = jnp.abs(jnp.linspace(MIN_DECAY, MAX_DECAY, dh, dtype=F32))[None, :]
    kre, kim = _filter_spectra(z_pad, w1_pad, filt_b1, filt_w2[0], filt_b2, filt_w3[0], filt_b3,
                               filt_freq, filt_w4[0], deltas, cs)

    x2 = x.reshape(batch * seq, d)
    u, gate, plain = _inproj(x2, g_norm, w_in[0], b_in, conv_w[0], conv_b, batch, seq)
    yh = _hyena(u, gate, hyena_d, kre, kim, cs, batch, seq)
    yp = _pool(plain, pool_w[0].astype(BF16), pool_b, pool_scale, batch, seq)
    out = _merge(yh, yp, plain, x2, w_hyena_out[0].astype(BF16), w_pool_out[0].astype(BF16),
                 w_out[0].astype(BF16), g_final[None, :])
    return out.reshape(batch, seq, d)
```

```python
import functools
import math

import jax
import jax.numpy as jnp
from jax import lax
from jax.experimental import pallas as pl
from jax.experimental.pallas import tpu as pltpu

FILTER_EMB = 33
FAST_DECAY_PCT = 0.3
SLOW_DECAY_PCT = 1.5
DECAY_TARGET = 1e-2
MAX_DECAY = math.log(DECAY_TARGET) / FAST_DECAY_PCT
MIN_DECAY = math.log(DECAY_TARGET) / SLOW_DECAY_PCT
POOL_WINDOWS = (2, 4, 8, 16)
NORM_EPS = 1e-6

LANES = 128
SUBLANES = 8
MXU_DIM = 256
HYENA_COLS = 2 * MXU_DIM
ROW_CHUNK = 512
DFT_ROW_BLOCK = 64
VMEM_LIMIT = 56 * 1024 * 1024

F32 = jnp.float32
BF16 = jnp.bfloat16


def _dot(a, b):
    return jnp.dot(a, b, preferred_element_type=F32)


def _dot_hi(a, b):
    return jnp.dot(a, b, preferred_element_type=F32, precision=lax.Precision.HIGHEST)


def _rms_scale(x):
    return x * lax.rsqrt(jnp.mean(x * x, axis=-1, keepdims=True) + NORM_EPS)


def _chunk(i, size):
    return pl.ds(pl.multiple_of(i * size, size), size)


def _dft_kernel(ca_ref, sa_ref, cb_ref, sb_ref, o_ref):
    seq = cb_ref.shape[1]
    ca = ca_ref[0]
    sa = sa_ref[0]
    cb = cb_ref[...]
    sb = sb_ref[...]
    o_ref[:, 0:seq] = (ca * cb - sa * sb).astype(BF16)
    o_ref[:, seq:2 * seq] = (sa * cb + ca * sb).astype(BF16)


def _transform_matrices(seq):
    period = 8 * seq
    rb = DFT_ROW_BLOCK
    odd_t = 2 * jnp.arange(seq, dtype=jnp.int32) + 1
    hi = (2 * rb) * jnp.arange(seq // rb, dtype=jnp.int32)
    lo = 2 * jnp.arange(rb, dtype=jnp.int32) + 1
    ang = lambda m: (m % period).astype(F32) * (2.0 * math.pi / period)
    ang_hi = ang(hi[:, None] * odd_t[None, :])[:, None, :]
    ang_lo = ang(lo[:, None] * odd_t[None, :])
    hi_spec = pl.BlockSpec((1, 1, seq), lambda i: (i, 0, 0))
    lo_spec = pl.BlockSpec((rb, seq), lambda i: (0, 0))
    return pl.pallas_call(
        _dft_kernel,
        grid=(seq // rb,),
        in_specs=[hi_spec, hi_spec, lo_spec, lo_spec],
        out_specs=pl.BlockSpec((rb, 2 * seq), lambda i: (i, 0)),
        out_shape=jax.ShapeDtypeStruct((seq, 2 * seq), BF16),
        compiler_params=pltpu.CompilerParams(dimension_semantics=("parallel",)),
        name="dft_matrices",
    )(jnp.cos(ang_hi), jnp.sin(ang_hi), jnp.cos(ang_lo), jnp.sin(ang_lo))


def _filter_kernel(z_ref, w1_ref, b1_ref, w2_ref, b2_ref, w3_ref, b3_ref, fr_ref,
                   w4f_ref, w4b_ref, dl_ref, cs_ref, kre_ref, kim_ref,
                   h_ref, cphi_ref, sphi_ref, kk_ref):
    seq = z_ref.shape[0]
    n = 2 * seq
    cb = dl_ref.shape[1]

    @pl.when(pl.program_id(0) == 0)
    def _():
        fr = fr_ref[...]
        h = jnp.sin(fr * (_dot_hi(z_ref[...], w1_ref[...]) + b1_ref[...]))
        h = jnp.sin(fr * (_dot_hi(h, w2_ref[...]) + b2_ref[...]))
        h_ref[...] = jnp.sin(fr * (_dot_hi(h, w3_ref[...]) + b3_ref[...]))
        f_idx = lax.broadcasted_iota(jnp.int32, cphi_ref.shape, 0).astype(F32)
        phi = (f_idx + 0.5) * (math.pi / n)
        cphi_ref[...] = jnp.cos(phi) * (2.0 / n)
        sphi_ref[...] = jnp.sin(phi) * (2.0 / n)

    h = h_ref[...]
    decay = jnp.exp(-z_ref[:, 0:1] * dl_ref[...])
    kf = _dot_hi(h, w4f_ref[...]) * decay
    kb = _dot_hi(h, w4b_ref[...]) * decay
    kk_ref[:, 0:cb] = (kf + kb).astype(BF16)
    kk_ref[:, cb:2 * cb] = (kf - kb).astype(BF16)

    def body(i, carry):
        rows = _chunk(i, ROW_CHUNK)
        kk = kk_ref[...]
        a = _dot(cs_ref[rows, 0:seq], kk)
        b = _dot(cs_ref[rows, seq:n], kk)
        reps = cb // LANES
        c_phi = jnp.concatenate([cphi_ref[rows, :]] * reps, axis=1)
        s_phi = jnp.concatenate([sphi_ref[rows, :]] * reps, axis=1)
        kre_ref[rows, :] = c_phi * a[:, 0:cb] + s_phi * b[:, 0:cb]
        kim_ref[rows, :] = s_phi * a[:, cb:2 * cb] - c_phi * b[:, cb:2 * cb]
        return carry

    lax.fori_loop(0, seq // ROW_CHUNK, body, 0, unroll=True)


def _filter_spectra(z_pad, w1_pad, b1, w2, b2, w3, b3, freq, w4, deltas, cs):
    seq = z_pad.shape[0]
    hf = w2.shape[0]
    dh = deltas.shape[1]
    cb = MXU_DIM
    nblk = dh // cb
    full = lambda a: pl.BlockSpec(a.shape, lambda c: (0,) * a.ndim)
    return pl.pallas_call(
        _filter_kernel,
        grid=(nblk,),
        in_specs=[
            full(z_pad), full(w1_pad), full(b1), full(w2), full(b2), full(w3), full(b3), full(freq),
            pl.BlockSpec((hf, cb), lambda c: (0, c)),
            pl.BlockSpec((hf, cb), lambda c: (0, nblk + c)),
            pl.BlockSpec((1, cb), lambda c: (0, c)),
            pl.BlockSpec(cs.shape, lambda c: (0, 0), pipeline_mode=pl.Buffered(1)),
        ],
        out_specs=[pl.BlockSpec((seq, cb), lambda c: (0, c))] * 2,
        out_shape=[jax.ShapeDtypeStruct((seq, dh), F32)] * 2,
        scratch_shapes=[
            pltpu.VMEM((seq, hf), F32),
            pltpu.VMEM((seq, LANES), F32),
            pltpu.VMEM((seq, LANES), F32),
            pltpu.VMEM((seq, 2 * cb), BF16),
        ],
        compiler_params=pltpu.CompilerParams(
            dimension_semantics=("arbitrary",), vmem_limit_bytes=VMEM_LIMIT),
        name="filter_spectra",
    )(z_pad, w1_pad, b1, w2, b2, w3, b3, freq, w4, w4, deltas, cs)


def _conv3(p, w, bias, lo, n, seq_start, seq_end):
    n_rows = p.shape[0]
    s = SUBLANES
    dn = pltpu.roll(p, 1, 0)
    up = pltpu.roll(p, n_rows - 1, 0)
    w0, w1, w2 = w[0:1], w[1:2], w[2:3]
    out = dn * w0 + p * w1 + up * w2 + bias
    edge = lax.broadcasted_iota(jnp.int32, (s, p.shape[1]), 0)
    pieces = []
    body_lo, body_hi = lo, lo + n
    if seq_start:
        assert lo == 0
        pieces.append(jnp.where(edge == 0, 0.0, dn[0:s]) * w0 + p[0:s] * w1 + up[0:s] * w2 + bias)
        body_lo = s
    else:
        assert lo >= 1
    if seq_end:
        assert lo + n == n_rows
        body_hi = n_rows - s
    else:
        assert lo + n < n_rows
    pieces.append(out[body_lo:body_hi])
    if seq_end:
        pieces.append(dn[n_rows - s:] * w0 + p[n_rows - s:] * w1
                      + jnp.where(edge == s - 1, 0.0, up[n_rows - s:]) * w2 + bias)
    return jnp.concatenate(pieces, axis=0)


def _inproj_kernel(x_ref, g_ref, wa_ref, wb_ref, ba_ref, bb_ref, cwa_ref, cwb_ref,
                   cba_ref, cbb_ref, u_ref, gate_ref, p_ref, h_ref, *, nblk):
    j = pl.program_id(1)
    seq = h_ref.shape[0]
    cb = wa_ref.shape[1]
    halo = 2 * SUBLANES

    @pl.when(j == 0)
    def _():
        h_ref[...] = (_rms_scale(x_ref[...]) * g_ref[...]).astype(BF16)

    def weights():
        return jnp.concatenate([wa_ref[...].astype(BF16), wb_ref[...].astype(BF16)], axis=1)

    def conv_chunks(epilogue):
        w = weights()
        for k in range(seq // ROW_CHUNK):
            lo = max(k * ROW_CHUNK - halo, 0)
            hi = min((k + 1) * ROW_CHUNK + halo, seq)
            res = _dot(h_ref[lo:hi, :], w)
            take = (k * ROW_CHUNK - lo, ROW_CHUNK, lo == 0, hi == seq)
            epilogue(slice(k * ROW_CHUNK, (k + 1) * ROW_CHUNK),
                     res[:, 0:cb] + ba_ref[...], res[:, cb:2 * cb] + bb_ref[...], take)

    @pl.when(j < nblk)
    def _():
        def epilogue(rows, pa, pb, take):
            u_ref[rows, :] = (_conv3(pa, cwa_ref[...], cba_ref[...], *take)
                              * _conv3(pb, cwb_ref[...], cbb_ref[...], *take))
        conv_chunks(epilogue)

    @pl.when(jnp.logical_and(j >= nblk, j < 2 * nblk))
    def _():
        def epilogue(rows, pa, pb, take):
            lo, n = take[0], take[1]
            gate_ref[rows, :] = (_conv3(pa, cwa_ref[...], cba_ref[...], *take)
                                 * jax.nn.silu(pb[lo:lo + n])).astype(BF16)
        conv_chunks(epilogue)

    @pl.when(j >= 2 * nblk)
    def _():
        res = _dot(h_ref[...], weights())
        p_ref[:, 0:cb] = (res[:, 0:cb] + ba_ref[...]).astype(BF16)
        p_ref[:, cb:2 * cb] = (res[:, cb:2 * cb] + bb_ref[...]).astype(BF16)


def _inproj(x2, g_norm, w_in, b_in, conv_w, conv_b, batch, seq):
    d = x2.shape[1]
    cols = w_in.shape[1]
    cb = MXU_DIM
    nblk = conv_b.shape[1] // (3 * cb)
    n_plain = (cols - 4 * nblk * cb) // (2 * cb)
    n_steps = 2 * nblk + n_plain
    taps = conv_w.shape[0]

    def col_a(j):
        return jnp.where(j < nblk, nblk + j,
                         jnp.where(j < 2 * nblk, j - nblk, 4 * nblk + 2 * (j - 2 * nblk)))

    def col_b(j):
        return jnp.where(j < nblk, 2 * nblk + j,
                         jnp.where(j < 2 * nblk, 2 * nblk + j, 4 * nblk + 2 * (j - 2 * nblk) + 1))

    def conv_a(j):
        return jnp.where(j < nblk, nblk + j, jnp.where(j < 2 * nblk, j - nblk, 0))

    def conv_b_idx(j):
        return jnp.where(j < nblk, 2 * nblk + j, 0)

    clip = lambda v, hi: jnp.minimum(jnp.maximum(v, 0), hi)
    return pl.pallas_call(
        functools.partial(_inproj_kernel, nblk=nblk),
        grid=(batch, n_steps),
        in_specs=[
            pl.BlockSpec((seq, d), lambda b, j: (b, 0)),
            pl.BlockSpec((1, d), lambda b, j: (0, 0)),
            pl.BlockSpec((d, cb), lambda b, j: (0, col_a(j))),
            pl.BlockSpec((d, cb), lambda b, j: (0, col_b(j))),
            pl.BlockSpec((1, cb), lambda b, j: (0, col_a(j))),
            pl.BlockSpec((1, cb), lambda b, j: (0, col_b(j))),
            pl.BlockSpec((taps, cb), lambda b, j: (0, conv_a(j))),
            pl.BlockSpec((taps, cb), lambda b, j: (0, conv_b_idx(j))),
            pl.BlockSpec((1, cb), lambda b, j: (0, conv_a(j))),
            pl.BlockSpec((1, cb), lambda b, j: (0, conv_b_idx(j))),
        ],
        out_specs=[
            pl.BlockSpec((seq, cb), lambda b, j: (b, clip(j, nblk - 1))),
            pl.BlockSpec((seq, cb), lambda b, j: (b, clip(j - nblk, nblk - 1))),
            pl.BlockSpec((seq, 2 * cb), lambda b, j: (b, clip(j - 2 * nblk, n_plain - 1))),
        ],
        out_shape=[
            jax.ShapeDtypeStruct((batch * seq, nblk * cb), F32),
            jax.ShapeDtypeStruct((batch * seq, nblk * cb), BF16),
            jax.ShapeDtypeStruct((batch * seq, n_plain * 2 * cb), BF16),
        ],
        scratch_shapes=[pltpu.VMEM((seq, d), BF16)],
        compiler_params=pltpu.CompilerParams(
            dimension_semantics=("arbitrary", "arbitrary"), vmem_limit_bytes=VMEM_LIMIT),
        name="inproj",
    )(x2, g_norm, w_in, w_in, b_in, b_in, conv_w, conv_w, conv_b, conv_b)


def _hyena_kernel(u_ref, gate_ref, d_ref, kre_ref, kim_ref, cs_ref, y_ref, ub_ref, pq_ref):
    seq = u_ref.shape[0]
    n = 2 * seq
    n_chunks = seq // ROW_CHUNK

    def to_bf16(i, carry):
        rows = _chunk(i, ROW_CHUNK)
        ub_ref[rows, :] = u_ref[rows, :].astype(BF16)
        return carry

    lax.fori_loop(0, n_chunks, to_bf16, 0)

    def forward(i, carry):
        rows = _chunk(i, ROW_CHUNK)
        ub = ub_ref[...]
        ur = _dot(cs_ref[rows, 0:seq], ub)
        vi = _dot(cs_ref[rows, seq:n], ub)
        kre = kre_ref[rows, :]
        kim = kim_ref[rows, :]
        pq_ref[rows, :] = (kre * ur + kim * vi).astype(BF16)
        pq_ref[pl.ds(pl.multiple_of(seq + i * ROW_CHUNK, ROW_CHUNK), ROW_CHUNK), :] = (
            kre * vi - kim * ur).astype(BF16)
        return carry

    lax.fori_loop(0, n_chunks, forward, 0, unroll=True)

    def inverse(i, carry):
        rows = _chunk(i, ROW_CHUNK)
        y = _dot(cs_ref[rows, :], pq_ref[...])
        y = y + u_ref[rows, :] * d_ref[...]
        y_ref[rows, :] = (y * gate_ref[rows, :].astype(F32)).astype(BF16)
        return carry

    lax.fori_loop(0, n_chunks, inverse, 0, unroll=True)


def _hyena(u, gate, hyena_d, kre, kim, cs, batch, seq):
    dh = hyena_d.shape[1]
    cw = HYENA_COLS
    act = pl.BlockSpec((seq, cw), lambda c, b: (b, c))
    once = lambda a, imap: pl.BlockSpec(a, imap, pipeline_mode=pl.Buffered(1))
    return pl.pallas_call(
        _hyena_kernel,
        grid=(dh // cw, batch),
        in_specs=[
            act, act,
            pl.BlockSpec((1, cw), lambda c, b: (0, c)),
            once((seq, cw), lambda c, b: (0, c)),
            once((seq, cw), lambda c, b: (0, c)),
            once(cs.shape, lambda c, b: (0, 0)),
        ],
        out_specs=act,
        out_shape=jax.ShapeDtypeStruct((batch * seq, dh), BF16),
        scratch_shapes=[pltpu.VMEM((seq, cw), BF16), pltpu.VMEM((2 * seq, cw), BF16)],
        compiler_params=pltpu.CompilerParams(
            dimension_semantics=("arbitrary", "arbitrary"), vmem_limit_bytes=VMEM_LIMIT),
        name="hyena",
    )(u, gate, hyena_d, kre, kim, cs)


def _pool_kernel(u_ref, z_ref, w_ref, b_ref, s_ref, y_ref):
    seq = u_ref.shape[0]
    gw = w_ref.shape[1]
    sl = SUBLANES
    pad = 2 * sl
    ext = seq + pad
    edge = lax.broadcasted_iota(jnp.int32, (sl, gw), 0)
    assert max(POOL_WINDOWS) // 2 <= sl
    for g, win in enumerate(POOL_WINDOWS):
        cols = slice(g * gw, (g + 1) * gw)
        half = win // 2
        u = u_ref[:, cols].astype(F32)
        s = jnp.concatenate([u, jnp.zeros((pad, gw), F32)], axis=0)
        w = 1
        while w < win:
            s = s + pltpu.roll(s, w, 0)
            w *= 2
        if half > 1:
            s = pltpu.roll(s, ext - (half - 1), 0)
        total = s[0:seq]
        cnt_first = jnp.minimum(edge + half, win).astype(F32)
        cnt_last = jnp.minimum(sl - edge + half, win).astype(F32)
        mean = jnp.concatenate([total[0:sl] / cnt_first,
                                total[sl:seq - sl] * (1.0 / win),
                                total[seq - sl:] / cnt_last], axis=0)
        pooled = (mean - u).astype(BF16)
        y = _dot(pooled, w_ref[g]) + b_ref[:, cols]
        y = y * s_ref[:, cols] * jax.nn.silu(z_ref[:, cols].astype(F32))
        y_ref[:, cols] = y.astype(BF16)


def _pool(plain, pool_w_bf, pool_b, pool_scale, batch, seq):
    dp = pool_b.shape[1]
    full = lambda a: pl.BlockSpec(a.shape, lambda b: (0,) * a.ndim)
    return pl.pallas_call(
        _pool_kernel,
        grid=(batch,),
        in_specs=[
            pl.BlockSpec((seq, dp), lambda b: (b, 0)),
            pl.BlockSpec((seq, dp), lambda b: (b, 1)),
            full(pool_w_bf), full(pool_b), full(pool_scale),
        ],
        out_specs=pl.BlockSpec((seq, dp), lambda b: (b, 0)),
        out_shape=jax.ShapeDtypeStruct((batch * seq, dp), BF16),
        compiler_params=pltpu.CompilerParams(
            dimension_semantics=("parallel",), vmem_limit_bytes=VMEM_LIMIT),
        name="pool",
    )(plain, plain, pool_w_bf, pool_b, pool_scale)


def _merge_kernel(yh_ref, yp_ref, g0_ref, g1_ref, x_ref, wh_ref, wp_ref, wo_ref, gf_ref, o_ref):
    out_h = _dot(yh_ref[...], wh_ref[...])
    out_p = _dot(yp_ref[...], wp_ref[...])
    merged = (jax.nn.sigmoid(g0_ref[...].astype(F32)) * out_h
              + jax.nn.sigmoid(g1_ref[...].astype(F32)) * out_p)
    res = x_ref[...] + _dot(merged.astype(BF16), wo_ref[...])
    o_ref[...] = _rms_scale(res) * gf_ref[...]


def _merge(yh, yp, plain, x2, wh_bf, wp_bf, wo_bf, g_final, tm=1024):
    n_tok, d = x2.shape
    tile = lambda col: pl.BlockSpec((tm, d), lambda i: (i, col))
    resident = lambda a: pl.BlockSpec(a.shape, lambda i: (0, 0), pipeline_mode=pl.Buffered(1))
    return pl.pallas_call(
        _merge_kernel,
        grid=(n_tok // tm,),
        in_specs=[
            tile(0), tile(0), tile(2), tile(3), tile(0),
            resident(wh_bf), resident(wp_bf), resident(wo_bf),
            pl.BlockSpec((1, d), lambda i: (0, 0)),
        ],
        out_specs=tile(0),
        out_shape=jax.ShapeDtypeStruct((n_tok, d), F32),
        compiler_params=pltpu.CompilerParams(
            dimension_semantics=("parallel",), vmem_limit_bytes=VMEM_LIMIT),
        name="merge",
    )(yh, yp, plain, plain, x2, wh_bf, wp_bf, wo_bf, g_final)


def _filter_features(seq):
    t = jnp.linspace(0.0, 1.0, seq, dtype=F32)[:, None]
    bands = (FILTER_EMB - 1) // 2
    w = 2.0 * math.pi * jnp.arange(seq, dtype=F32) / seq
    f = jnp.linspace(1e-4, bands - 1, bands, dtype=F32)
    ang = w[:, None] * f[None, :]
    return jnp.concatenate([t, jnp.cos(ang), -jnp.sin(ang)], axis=-1)


def kernel(x, g_norm, w_in, b_in, conv_w, conv_b, filt_w1, filt_b1, filt_w2, filt_b2, filt_w3, filt_b3, filt_w4, filt_freq, hyena_d, w_hyena_out, pool_w, pool_b, pool_scale, w_pool_out, w_out, g_final):
    batch, seq, d = x.shape
    dh = hyena_d.shape[1]
    dp = pool_b.shape[1]
    assert g_norm.shape[0] == 1, "single-layer block"
    assert d == dh == dp and conv_b.shape[1] == 3 * dh and w_in.shape[2] == 4 * dh + 2 * dp + 2 * d
    assert dh % HYENA_COLS == 0 and seq % ROW_CHUNK == 0 and seq % DFT_ROW_BLOCK == 0

    cs = _transform_matrices(seq)
    z_pad = jnp.pad(_filter_features(seq), ((0, 0), (0, LANES - FILTER_EMB)))
    w1_pad = jnp.pad(filt_w1[0], ((0, LANES - FILTER_EMB), (0, 0)))
    deltas = jnp.abs(jnp.linspace(MIN_DECAY, MAX_DECAY, dh, dtype=F32))[None, :]
    kre, kim = _filter_spectra(z_pad, w1_pad, filt_b1, filt_w2[0], filt_b2, filt_w3[0], filt_b3,
                               filt_freq, filt_w4[0], deltas, cs)

    x2 = x.reshape(batch * seq, d)
    u, gate, plain = _inproj(x2, g_norm, w_in[0], b_in, conv_w[0], conv_b, batch, seq)
    yh = _hyena(u, gate, hyena_d, kre, kim, cs, batch, seq)
    yp = _pool(plain, pool_w[0].astype(BF16), pool_b, pool_scale, batch, seq)
    out = _merge(yh, yp, plain, x2, w_hyena_out[0].astype(BF16), w_pool_out[0].astype(BF16),
                 w_out[0].astype(BF16), g_final[None, :])
    return out.reshape(batch, seq, d)
```

```python
import functools
import math

import jax
import jax.numpy as jnp
from jax import lax
from jax.experimental import pallas as pl
from jax.experimental.pallas import tpu as pltpu

FILTER_EMB = 33
FAST_DECAY_PCT = 0.3
SLOW_DECAY_PCT = 1.5
DECAY_TARGET = 1e-2
MAX_DECAY = math.log(DECAY_TARGET) / FAST_DECAY_PCT
MIN_DECAY = math.log(DECAY_TARGET) / SLOW_DECAY_PCT
POOL_WINDOWS = (2, 4, 8, 16)
NORM_EPS = 1e-6

LANES = 128
SUBLANES = 8
MXU_DIM = 256
HYENA_COLS = 2 * MXU_DIM
ROW_CHUNK = 512
CONV_BLOCKS = 4
SPEC_ROWS = 64
VMEM_LIMIT = 56 * 1024 * 1024

F32 = jnp.float32
BF16 = jnp.bfloat16


def _dot(a, b):
    return jnp.dot(a, b, preferred_element_type=F32)


def _dot_hi(a, b):
    return jnp.dot(a, b, preferred_element_type=F32, precision=lax.Precision.HIGHEST)


def _rms_scale(x):
    return x * lax.rsqrt(jnp.mean(x * x, axis=-1, keepdims=True) + NORM_EPS)


def _chunk(i, size):
    return pl.ds(pl.multiple_of(i * size, size), size)


def _transform_matrices(lb):
    period = 8 * lb
    odd = 2 * jnp.arange(lb, dtype=jnp.int32) + 1
    ang = ((odd[:, None] * odd[None, :]) % period).astype(F32) * (2.0 * math.pi / period)
    return jnp.concatenate([jnp.cos(ang), jnp.sin(ang)], axis=1).astype(BF16)


def _filter_kernel(z_ref, w1_ref, b1_ref, w2_ref, b2_ref, w3_ref, b3_ref, fr_ref,
                   w4f_ref, w4b_ref, dl_ref, cs_ref, kre_ref, kim_ref,
                   h_ref, cphi_ref, sphi_ref, sgn_ref, kk_ref, xr_ref, xi_ref):
    seq = z_ref.shape[0]
    lb = cs_ref.shape[0]
    nb = seq // lb
    cb = dl_ref.shape[1]
    scale = 1.0 / lb

    @pl.when(pl.program_id(0) == 0)
    def _():
        fr = fr_ref[...]
        h = jnp.sin(fr * (_dot_hi(z_ref[...], w1_ref[...]) + b1_ref[...]))
        h = jnp.sin(fr * (_dot_hi(h, w2_ref[...]) + b2_ref[...]))
        h_ref[...] = jnp.sin(fr * (_dot_hi(h, w3_ref[...]) + b3_ref[...]))
        f_idx = lax.broadcasted_iota(jnp.int32, cphi_ref.shape, 0)
        phi = (f_idx.astype(F32) + 0.5) * (math.pi / (2 * lb))
        cphi_ref[...] = jnp.cos(phi) * scale
        sphi_ref[...] = jnp.sin(phi) * scale
        sgn_ref[...] = (1 - 2 * (f_idx & 1)).astype(F32)

    h = h_ref[...]
    decay = jnp.exp(-z_ref[:, 0:1] * dl_ref[...])
    taps = (_dot_hi(h, w4f_ref[...]) * decay, _dot_hi(h, w4b_ref[...]) * decay)
    taps = tuple(k.astype(BF16) for k in taps)
    for side, k in enumerate(taps):
        for s_ in range(nb):
            c0 = (side * nb + s_) * cb
            kk_ref[:, c0:c0 + cb] = k[s_ * lb:(s_ + 1) * lb]

    group = 2 * cb
    tile = lambda ref, width: jnp.concatenate([ref[...]] * (width // LANES), axis=1)
    c_phi, s_phi = tile(cphi_ref, group), tile(sphi_ref, group)
    for g in range(2 * nb * cb // group):
        cols = slice(g * group, (g + 1) * group)
        kk = kk_ref[:, cols]
        a = _dot(cs_ref[:, 0:lb], kk)
        b = _dot(cs_ref[:, lb:2 * lb], kk)
        xr_ref[:, cols] = c_phi * a + s_phi * b
        xi_ref[:, cols] = s_phi * a - c_phi * b

    sgn = tile(sgn_ref, cb)

    def seg(ref, side, s_):
        c0 = (side * nb + s_) * cb
        return ref[:, c0:c0 + cb]

    def causal_block(side, delta):
        r0 = (delta - 1) * lb
        first = taps[side][r0:r0 + 1].astype(F32) * scale
        re = seg(xr_ref, side, delta) - sgn * seg(xi_ref, side, delta - 1)
        im = seg(xi_ref, side, delta) + sgn * (seg(xr_ref, side, delta - 1) - first)
        return re, im

    mid = nb - 1
    kre_ref[mid] = seg(xr_ref, 0, 0) + seg(xr_ref, 1, 0)
    kim_ref[mid] = seg(xi_ref, 0, 0) - seg(xi_ref, 1, 0)
    for delta in range(1, nb):
        re, im = causal_block(0, delta)
        kre_ref[mid + delta] = re
        kim_ref[mid + delta] = im
        re, im = causal_block(1, delta)
        kre_ref[mid - delta] = re
        kim_ref[mid - delta] = -im


def _filter_spectra(z_pad, w1_pad, b1, w2, b2, w3, b3, freq, w4, deltas, cs):
    seq = z_pad.shape[0]
    lb = cs.shape[0]
    nb = seq // lb
    hf = w2.shape[0]
    dh = deltas.shape[1]
    cb = MXU_DIM
    nblk = dh // cb
    full = lambda a: pl.BlockSpec(a.shape, lambda c: (0,) * a.ndim)
    return pl.pallas_call(
        _filter_kernel,
        grid=(nblk,),
        in_specs=[
            full(z_pad), full(w1_pad), full(b1), full(w2), full(b2), full(w3), full(b3), full(freq),
            pl.BlockSpec((hf, cb), lambda c: (0, c)),
            pl.BlockSpec((hf, cb), lambda c: (0, nblk + c)),
            pl.BlockSpec((1, cb), lambda c: (0, c)),
            full(cs),
        ],
        out_specs=[pl.BlockSpec((2 * nb - 1, lb, cb), lambda c: (0, 0, c))] * 2,
        out_shape=[jax.ShapeDtypeStruct((2 * nb - 1, lb, dh), F32)] * 2,
        scratch_shapes=[
            pltpu.VMEM((seq, hf), F32),
            pltpu.VMEM((lb, LANES), F32),
            pltpu.VMEM((lb, LANES), F32),
            pltpu.VMEM((lb, LANES), F32),
            pltpu.VMEM((lb, 2 * nb * cb), BF16),
            pltpu.VMEM((lb, 2 * nb * cb), F32),
            pltpu.VMEM((lb, 2 * nb * cb), F32),
        ],
        compiler_params=pltpu.CompilerParams(
            dimension_semantics=("arbitrary",), vmem_limit_bytes=VMEM_LIMIT),
        name="filter_spectra",
    )(z_pad, w1_pad, b1, w2, b2, w3, b3, freq, w4, w4, deltas, cs)


def _conv3(p, w, bias, lo, n, seq_start, seq_end):
    n_rows = p.shape[0]
    s = SUBLANES
    dn = pltpu.roll(p, 1, 0)
    up = pltpu.roll(p, n_rows - 1, 0)
    w0, w1, w2 = w[0:1], w[1:2], w[2:3]
    out = dn * w0 + p * w1 + up * w2 + bias
    edge = lax.broadcasted_iota(jnp.int32, (s, p.shape[1]), 0)
    pieces = []
    body_lo, body_hi = lo, lo + n
    if seq_start:
        assert lo == 0
        pieces.append(jnp.where(edge == 0, 0.0, dn[0:s]) * w0 + p[0:s] * w1 + up[0:s] * w2 + bias)
        body_lo = s
    else:
        assert lo >= 1
    if seq_end:
        assert lo + n == n_rows
        body_hi = n_rows - s
    else:
        assert lo + n < n_rows
    pieces.append(out[body_lo:body_hi])
    if seq_end:
        pieces.append(dn[n_rows - s:] * w0 + p[n_rows - s:] * w1
                      + jnp.where(edge == s - 1, 0.0, up[n_rows - s:]) * w2 + bias)
    return jnp.concatenate(pieces, axis=0)


def _inproj_kernel(x_ref, g_ref, wa_ref, wb_ref, ba_ref, bb_ref, cwa_ref, cwb_ref,
                   cba_ref, cbb_ref, u_ref, gate_ref, p_ref, h_ref, *, nblk):
    j = pl.program_id(1)
    seq = h_ref.shape[0]
    cb = wa_ref.shape[1]
    halo = 2 * SUBLANES

    @pl.when(j == 0)
    def _():
        h_ref[...] = (_rms_scale(x_ref[...]) * g_ref[...]).astype(BF16)

    def weights():
        return jnp.concatenate([wa_ref[...].astype(BF16), wb_ref[...].astype(BF16)], axis=1)

    def conv_chunks(epilogue):
        w = weights()
        for k in range(seq // ROW_CHUNK):
            lo = max(k * ROW_CHUNK - halo, 0)
            hi = min((k + 1) * ROW_CHUNK + halo, seq)
            res = _dot(h_ref[lo:hi, :], w)
            take = (k * ROW_CHUNK - lo, ROW_CHUNK, lo == 0, hi == seq)
            epilogue(slice(k * ROW_CHUNK, (k + 1) * ROW_CHUNK),
                     res[:, 0:cb] + ba_ref[...], res[:, cb:2 * cb] + bb_ref[...], take)

    @pl.when(j < nblk)
    def _():
        def epilogue(rows, pa, pb, take):
            u_ref[rows, :] = (_conv3(pa, cwa_ref[...], cba_ref[...], *take)
                              * _conv3(pb, cwb_ref[...], cbb_ref[...], *take))
        conv_chunks(epilogue)

    @pl.when(jnp.logical_and(j >= nblk, j < 2 * nblk))
    def _():
        def epilogue(rows, pa, pb, take):
            lo, n = take[0], take[1]
            gate_ref[rows, :] = (_conv3(pa, cwa_ref[...], cba_ref[...], *take)
                                 * jax.nn.silu(pb[lo:lo + n])).astype(BF16)
        conv_chunks(epilogue)

    @pl.when(j >= 2 * nblk)
    def _():
        res = _dot(h_ref[...], weights())
        p_ref[:, 0:cb] = (res[:, 0:cb] + ba_ref[...]).astype(BF16)
        p_ref[:, cb:2 * cb] = (res[:, cb:2 * cb] + bb_ref[...]).astype(BF16)


def _inproj(x2, g_norm, w_in, b_in, conv_w, conv_b, batch, seq):
    d = x2.shape[1]
    cols = w_in.shape[1]
    cb = MXU_DIM
    nblk = conv_b.shape[1] // (3 * cb)
    n_plain = (cols - 4 * nblk * cb) // (2 * cb)
    n_steps = 2 * nblk + n_plain
    taps = conv_w.shape[0]

    def col_a(j):
        return jnp.where(j < nblk, nblk + j,
                         jnp.where(j < 2 * nblk, j - nblk, 4 * nblk + 2 * (j - 2 * nblk)))

    def col_b(j):
        return jnp.where(j < nblk, 2 * nblk + j,
                         jnp.where(j < 2 * nblk, 2 * nblk + j, 4 * nblk + 2 * (j - 2 * nblk) + 1))

    def conv_a(j):
        return jnp.where(j < nblk, nblk + j, jnp.where(j < 2 * nblk, j - nblk, 0))

    def conv_b_idx(j):
        return jnp.where(j < nblk, 2 * nblk + j, 0)

    clip = lambda v, hi: jnp.minimum(jnp.maximum(v, 0), hi)
    return pl.pallas_call(
        functools.partial(_inproj_kernel, nblk=nblk),
        grid=(batch, n_steps),
        in_specs=[
            pl.BlockSpec((seq, d), lambda b, j: (b, 0)),
            pl.BlockSpec((1, d), lambda b, j: (0, 0)),
            pl.BlockSpec((d, cb), lambda b, j: (0, col_a(j))),
            pl.BlockSpec((d, cb), lambda b, j: (0, col_b(j))),
            pl.BlockSpec((1, cb), lambda b, j: (0, col_a(j))),
            pl.BlockSpec((1, cb), lambda b, j: (0, col_b(j))),
            pl.BlockSpec((taps, cb), lambda b, j: (0, conv_a(j))),
            pl.BlockSpec((taps, cb), lambda b, j: (0, conv_b_idx(j))),
            pl.BlockSpec((1, cb), lambda b, j: (0, conv_a(j))),
            pl.BlockSpec((1, cb), lambda b, j: (0, conv_b_idx(j))),
        ],
        out_specs=[
            pl.BlockSpec((seq, cb), lambda b, j: (b, clip(j, nblk - 1))),
            pl.BlockSpec((seq, cb), lambda b, j: (b, clip(j - nblk, nblk - 1))),
            pl.BlockSpec((seq, 2 * cb), lambda b, j: (b, clip(j - 2 * nblk, n_plain - 1))),
        ],
        out_shape=[
            jax.ShapeDtypeStruct((batch * seq, nblk * cb), F32),
            jax.ShapeDtypeStruct((batch * seq, nblk * cb), BF16),
            jax.ShapeDtypeStruct((batch * seq, n_plain * 2 * cb), BF16),
        ],
        scratch_shapes=[pltpu.VMEM((seq, d), BF16)],
        compiler_params=pltpu.CompilerParams(
            dimension_semantics=("arbitrary", "arbitrary"), vmem_limit_bytes=VMEM_LIMIT),
        name="inproj",
    )(x2, g_norm, w_in, w_in, b_in, b_in, conv_w, conv_w, conv_b, conv_b)


def _hyena_kernel(u_ref, gate_ref, d_ref, kre_ref, kim_ref, cs_ref, y_ref, ub_ref, av_ref, pq_ref):
    seq = u_ref.shape[0]
    lb = cs_ref.shape[0]
    nb = seq // lb
    mid = nb - 1
    blocks = [slice(b * lb, (b + 1) * lb) for b in range(nb)]

    for rows in blocks:
        ub_ref[rows, :] = u_ref[rows, :].astype(BF16)

    for b, rows in enumerate(blocks):
        ub = ub_ref[rows, :]
        av_ref[2 * b] = _dot(cs_ref[:, 0:lb], ub)
        av_ref[2 * b + 1] = _dot(cs_ref[:, lb:2 * lb], ub)

    def spectral(i, carry):
        rows = _chunk(i, SPEC_ROWS)
        a_blk = [av_ref[2 * b, rows, :] for b in range(nb)]
        v_blk = [av_ref[2 * b + 1, rows, :] for b in range(nb)]
        for a in range(nb):
            p = q = None
            for b in range(nb):
                kr = kre_ref[mid + a - b, rows, :]
                ki = kim_ref[mid + a - b, rows, :]
                tp = kr * a_blk[b] + ki * v_blk[b]
                tq = kr * v_blk[b] - ki * a_blk[b]
                p = tp if p is None else p + tp
                q = tq if q is None else q + tq
            pq_ref[a, rows, :] = p.astype(BF16)
            pq_ref[a, pl.ds(pl.multiple_of(lb + i * SPEC_ROWS, SPEC_ROWS), SPEC_ROWS), :] = (
                q.astype(BF16))
        return carry

    lax.fori_loop(0, lb // SPEC_ROWS, spectral, 0)

    for a, rows in enumerate(blocks):
        y = _dot(cs_ref[...], pq_ref[a])
        y = y + u_ref[rows, :] * d_ref[...]
        y_ref[rows, :] = (y * gate_ref[rows, :].astype(F32)).astype(BF16)


def _hyena(u, gate, hyena_d, kre, kim, cs, batch, seq):
    dh = hyena_d.shape[1]
    cw = HYENA_COLS
    lb = cs.shape[0]
    nb = seq // lb
    act = pl.BlockSpec((seq, cw), lambda c, b: (b, c))
    spectra = pl.BlockSpec((2 * nb - 1, lb, cw), lambda c, b: (0, 0, c),
                           pipeline_mode=pl.Buffered(1))
    return pl.pallas_call(
        _hyena_kernel,
        grid=(dh // cw, batch),
        in_specs=[
            act, act,
            pl.BlockSpec((1, cw), lambda c, b: (0, c)),
            spectra, spectra,
            pl.BlockSpec(cs.shape, lambda c, b: (0, 0)),
        ],
        out_specs=act,
        out_shape=jax.ShapeDtypeStruct((batch * seq, dh), BF16),
        scratch_shapes=[
            pltpu.VMEM((seq, cw), BF16),
            pltpu.VMEM((2 * nb, lb, cw), F32),
            pltpu.VMEM((nb, 2 * lb, cw), BF16),
        ],
        compiler_params=pltpu.CompilerParams(
            dimension_semantics=("arbitrary", "arbitrary"), vmem_limit_bytes=VMEM_LIMIT),
        name="hyena",
    )(u, gate, hyena_d, kre, kim, cs)


def _pool_kernel(u_ref, z_ref, w_ref, b_ref, s_ref, y_ref):
    seq = u_ref.shape[0]
    gw = w_ref.shape[1]
    sl = SUBLANES
    pad = 2 * sl
    ext = seq + pad
    edge = lax.broadcasted_iota(jnp.int32, (sl, gw), 0)
    assert max(POOL_WINDOWS) // 2 <= sl
    for g, win in enumerate(POOL_WINDOWS):
        cols = slice(g * gw, (g + 1) * gw)
        half = win // 2
        u = u_ref[:, cols].astype(F32)
        s = jnp.concatenate([u, jnp.zeros((pad, gw), F32)], axis=0)
        w = 1
        while w < win:
            s = s + pltpu.roll(s, w, 0)
            w *= 2
        if half > 1:
            s = pltpu.roll(s, ext - (half - 1), 0)
        total = s[0:seq]
        cnt_first = jnp.minimum(edge + half, win).astype(F32)
        cnt_last = jnp.minimum(sl - edge + half, win).astype(F32)
        mean = jnp.concatenate([total[0:sl] / cnt_first,
                                total[sl:seq - sl] * (1.0 / win),
                                total[seq - sl:] / cnt_last], axis=0)
        pooled = (mean - u).astype(BF16)
        y = _dot(pooled, w_ref[g]) + b_ref[:, cols]
        y = y * s_ref[:, cols] * jax.nn.silu(z_ref[:, cols].astype(F32))
        y_ref[:, cols] = y.astype(BF16)


def _pool(plain, pool_w_bf, pool_b, pool_scale, batch, seq):
    dp = pool_b.shape[1]
    full = lambda a: pl.BlockSpec(a.shape, lambda b: (0,) * a.ndim)
    return pl.pallas_call(
        _pool_kernel,
        grid=(batch,),
        in_specs=[
            pl.BlockSpec((seq, dp), lambda b: (b, 0)),
            pl.BlockSpec((seq, dp), lambda b: (b, 1)),
            full(pool_w_bf), full(pool_b), full(pool_scale),
        ],
        out_specs=pl.BlockSpec((seq, dp), lambda b: (b, 0)),
        out_shape=jax.ShapeDtypeStruct((batch * seq, dp), BF16),
        compiler_params=pltpu.CompilerParams(
            dimension_semantics=("parallel",), vmem_limit_bytes=VMEM_LIMIT),
        name="pool",
    )(plain, plain, pool_w_bf, pool_b, pool_scale)


def _merge_kernel(yh_ref, yp_ref, g0_ref, g1_ref, x_ref, wh_ref, wp_ref, wo_ref, gf_ref, o_ref):
    out_h = _dot(yh_ref[...], wh_ref[...])
    out_p = _dot(yp_ref[...], wp_ref[...])
    merged = (jax.nn.sigmoid(g0_ref[...].astype(F32)) * out_h
              + jax.nn.sigmoid(g1_ref[...].astype(F32)) * out_p)
    res = x_ref[...] + _dot(merged.astype(BF16), wo_ref[...])
    o_ref[...] = _rms_scale(res) * gf_ref[...]


def _merge(yh, yp, plain, x2, wh_bf, wp_bf, wo_bf, g_final, tm=1024):
    n_tok, d = x2.shape
    tile = lambda col: pl.BlockSpec((tm, d), lambda i: (i, col))
    resident = lambda a: pl.BlockSpec(a.shape, lambda i: (0, 0), pipeline_mode=pl.Buffered(1))
    return pl.pallas_call(
        _merge_kernel,
        grid=(n_tok // tm,),
        in_specs=[
            tile(0), tile(0), tile(2), tile(3), tile(0),
            resident(wh_bf), resident(wp_bf), resident(wo_bf),
            pl.BlockSpec((1, d), lambda i: (0, 0)),
        ],
        out_specs=tile(0),
        out_shape=jax.ShapeDtypeStruct((n_tok, d), F32),
        compiler_params=pltpu.CompilerParams(
            dimension_semantics=("parallel",), vmem_limit_bytes=VMEM_LIMIT),
        name="merge",
    )(yh, yp, plain, plain, x2, wh_bf, wp_bf, wo_bf, g_final)


def _filter_features(seq):
    t = jnp.linspace(0.0, 1.0, seq, dtype=F32)[:, None]
    bands = (FILTER_EMB - 1) // 2
    w = 2.0 * math.pi * jnp.arange(seq, dtype=F32) / seq
    f = jnp.linspace(1e-4, bands - 1, bands, dtype=F32)
    ang = w[:, None] * f[None, :]
    return jnp.concatenate([t, jnp.cos(ang), -jnp.sin(ang)], axis=-1)


def kernel(x, g_norm, w_in, b_in, conv_w, conv_b, filt_w1, filt_b1, filt_w2, filt_b2, filt_w3, filt_b3, filt_w4, filt_freq, hyena_d, w_hyena_out, pool_w, pool_b, pool_scale, w_pool_out, w_out, g_final):
    batch, seq, d = x.shape
    dh = hyena_d.shape[1]
    dp = pool_b.shape[1]
    assert g_norm.shape[0] == 1, "single-layer block"
    assert d == dh == dp and conv_b.shape[1] == 3 * dh and w_in.shape[2] == 4 * dh + 2 * dp + 2 * d
    assert dh % HYENA_COLS == 0 and seq % ROW_CHUNK == 0
    assert seq % CONV_BLOCKS == 0 and (seq // CONV_BLOCKS) % SPEC_ROWS == 0

    cs = _transform_matrices(seq // CONV_BLOCKS)
    z_pad = jnp.pad(_filter_features(seq), ((0, 0), (0, LANES - FILTER_EMB)))
    w1_pad = jnp.pad(filt_w1[0], ((0, LANES - FILTER_EMB), (0, 0)))
    deltas = jnp.abs(jnp.linspace(MIN_DECAY, MAX_DECAY, dh, dtype=F32))[None, :]
    kre, kim = _filter_spectra(z_pad, w1_pad, filt_b1, filt_w2[0], filt_b2, filt_w3[0], filt_b3,
                               filt_freq, filt_w4[0], deltas, cs)

    x2 = x.reshape(batch * seq, d)
    u, gate, plain = _inproj(x2, g_norm, w_in[0], b_in, conv_w[0], conv_b, batch, seq)
    yh = _hyena(u, gate, hyena_d, kre, kim, cs, batch, seq)
    yp = _pool(plain, pool_w[0].astype(BF16), pool_b, pool_scale, batch, seq)
    out = _merge(yh, yp, plain, x2, w_hyena_out[0].astype(BF16), w_pool_out[0].astype(BF16),
                 w_out[0].astype(BF16), g_final[None, :])
    return out.reshape(batch, seq, d)
```

```python
import functools
import math

import jax
import jax.numpy as jnp
from jax import lax
from jax.experimental import pallas as pl
from jax.experimental.pallas import tpu as pltpu

FILTER_EMB = 33
FAST_DECAY_PCT = 0.3
SLOW_DECAY_PCT = 1.5
DECAY_TARGET = 1e-2
MAX_DECAY = math.log(DECAY_TARGET) / FAST_DECAY_PCT
MIN_DECAY = math.log(DECAY_TARGET) / SLOW_DECAY_PCT
POOL_WINDOWS = (2, 4, 8, 16)
NORM_EPS = 1e-6

LANES = 128
SUBLANES = 8
MXU_DIM = 256
HYENA_COLS = 2 * MXU_DIM
ROW_CHUNK = 512
CONV_BLOCKS = 4
SPEC_ROWS = 64
VMEM_LIMIT = 56 * 1024 * 1024

F32 = jnp.float32
BF16 = jnp.bfloat16


def _dot(a, b):
    return jnp.dot(a, b, preferred_element_type=F32)


def _dot_hi(a, b):
    return jnp.dot(a, b, preferred_element_type=F32, precision=lax.Precision.HIGHEST)


def _rms_scale(x):
    return x * lax.rsqrt(jnp.mean(x * x, axis=-1, keepdims=True) + NORM_EPS)


def _chunk(i, size):
    return pl.ds(pl.multiple_of(i * size, size), size)


def _transform_matrices(lb):
    period = 8 * lb
    odd = 2 * jnp.arange(lb, dtype=jnp.int32) + 1
    ang = ((odd[:, None] * odd[None, :]) % period).astype(F32) * (2.0 * math.pi / period)
    return jnp.concatenate([jnp.cos(ang), jnp.sin(ang)], axis=1).astype(BF16)


def _filter_kernel(z_ref, w1_ref, b1_ref, w2_ref, b2_ref, w3_ref, b3_ref, fr_ref,
                   w4f_ref, w4b_ref, dl_ref, cs_ref, kre_ref, kim_ref,
                   h_ref, cphi_ref, sphi_ref, sgn_ref, kk_ref, xr_ref, xi_ref):
    seq = z_ref.shape[0]
    lb = cs_ref.shape[0]
    nb = seq // lb
    cb = dl_ref.shape[1]
    scale = 1.0 / lb

    @pl.when(pl.program_id(0) == 0)
    def _():
        fr = fr_ref[...]
        h = jnp.sin(fr * (_dot_hi(z_ref[...], w1_ref[...]) + b1_ref[...]))
        h = jnp.sin(fr * (_dot_hi(h, w2_ref[...]) + b2_ref[...]))
        h_ref[...] = jnp.sin(fr * (_dot_hi(h, w3_ref[...]) + b3_ref[...]))
        f_idx = lax.broadcasted_iota(jnp.int32, cphi_ref.shape, 0)
        phi = (f_idx.astype(F32) + 0.5) * (math.pi / (2 * lb))
        cphi_ref[...] = jnp.cos(phi) * scale
        sphi_ref[...] = jnp.sin(phi) * scale
        sgn_ref[...] = (1 - 2 * (f_idx & 1)).astype(F32)

    h = h_ref[...]
    decay = jnp.exp(-z_ref[:, 0:1] * dl_ref[...])
    taps = (_dot_hi(h, w4f_ref[...]) * decay, _dot_hi(h, w4b_ref[...]) * decay)
    taps = tuple(k.astype(BF16) for k in taps)
    for side, k in enumerate(taps):
        for s_ in range(nb):
            c0 = (side * nb + s_) * cb
            kk_ref[:, c0:c0 + cb] = k[s_ * lb:(s_ + 1) * lb]

    group = 2 * cb
    tile = lambda ref, width: jnp.concatenate([ref[...]] * (width // LANES), axis=1)
    c_phi, s_phi = tile(cphi_ref, group), tile(sphi_ref, group)
    for g in range(2 * nb * cb // group):
        cols = slice(g * group, (g + 1) * group)
        kk = kk_ref[:, cols]
        a = _dot(cs_ref[:, 0:lb], kk)
        b = _dot(cs_ref[:, lb:2 * lb], kk)
        xr_ref[:, cols] = c_phi * a + s_phi * b
        xi_ref[:, cols] = s_phi * a - c_phi * b

    sgn = tile(sgn_ref, cb)

    def seg(ref, side, s_):
        c0 = (side * nb + s_) * cb
        return ref[:, c0:c0 + cb]

    def causal_block(side, delta):
        r0 = (delta - 1) * lb
        first = taps[side][r0:r0 + 1].astype(F32) * scale
        re = seg(xr_ref, side, delta) - sgn * seg(xi_ref, side, delta - 1)
        im = seg(xi_ref, side, delta) + sgn * (seg(xr_ref, side, delta - 1) - first)
        return re, im

    mid = nb - 1
    kre_ref[mid] = seg(xr_ref, 0, 0) + seg(xr_ref, 1, 0)
    kim_ref[mid] = seg(xi_ref, 0, 0) - seg(xi_ref, 1, 0)
    for delta in range(1, nb):
        re, im = causal_block(0, delta)
        kre_ref[mid + delta] = re
        kim_ref[mid + delta] = im
        re, im = causal_block(1, delta)
        kre_ref[mid - delta] = re
        kim_ref[mid - delta] = -im


def _filter_spectra(z_pad, w1_pad, b1, w2, b2, w3, b3, freq, w4, deltas, cs):
    seq = z_pad.shape[0]
    lb = cs.shape[0]
    nb = seq // lb
    hf = w2.shape[0]
    dh = deltas.shape[1]
    cb = MXU_DIM
    nblk = dh // cb
    full = lambda a: pl.BlockSpec(a.shape, lambda c: (0,) * a.ndim)
    return pl.pallas_call(
        _filter_kernel,
        grid=(nblk,),
        in_specs=[
            full(z_pad), full(w1_pad), full(b1), full(w2), full(b2), full(w3), full(b3), full(freq),
            pl.BlockSpec((hf, cb), lambda c: (0, c)),
            pl.BlockSpec((hf, cb), lambda c: (0, nblk + c)),
            pl.BlockSpec((1, cb), lambda c: (0, c)),
            full(cs),
        ],
        out_specs=[pl.BlockSpec((2 * nb - 1, lb, cb), lambda c: (0, 0, c))] * 2,
        out_shape=[jax.ShapeDtypeStruct((2 * nb - 1, lb, dh), F32)] * 2,
        scratch_shapes=[
            pltpu.VMEM((seq, hf), F32),
            pltpu.VMEM((lb, LANES), F32),
            pltpu.VMEM((lb, LANES), F32),
            pltpu.VMEM((lb, LANES), F32),
            pltpu.VMEM((lb, 2 * nb * cb), BF16),
            pltpu.VMEM((lb, 2 * nb * cb), F32),
            pltpu.VMEM((lb, 2 * nb * cb), F32),
        ],
        compiler_params=pltpu.CompilerParams(
            dimension_semantics=("arbitrary",), vmem_limit_bytes=VMEM_LIMIT),
        name="filter_spectra",
    )(z_pad, w1_pad, b1, w2, b2, w3, b3, freq, w4, w4, deltas, cs)


def _conv3(p, w, bias, lo, n):
    n_rows = p.shape[0]
    assert 1 <= lo and lo + n < n_rows
    dn = pltpu.roll(p, 1, 0)
    up = pltpu.roll(p, n_rows - 1, 0)
    out = dn * w[0:1] + p * w[1:2] + up * w[2:3] + bias
    return out[lo:lo + n]


def _inproj_kernel(xp_ref, x_ref, xn_ref, g_ref, w_ref, b_ref, cw_ref, cb_ref,
                   u_ref, gate_ref, p_ref, *, chunks_per_seq):
    i = pl.program_id(0)
    rows = x_ref.shape[0]
    halo = xp_ref.shape[0]
    dh = u_ref.shape[1]
    cb = MXU_DIM
    group = 2 * cb
    keep_prev = (i % chunks_per_seq != 0).astype(F32)
    keep_next = (i % chunks_per_seq != chunks_per_seq - 1).astype(F32)

    xw = jnp.concatenate([xp_ref[...], x_ref[...], xn_ref[...]], axis=0)
    h = (_rms_scale(xw) * g_ref[...]).astype(BF16)

    def project(col_a, col_b):
        w = jnp.concatenate([w_ref[:, col_a:col_a + cb], w_ref[:, col_b:col_b + cb]], axis=1)
        res = _dot(h, w)
        out = []
        for p, c0 in ((res[:, 0:cb], col_a), (res[:, cb:group], col_b)):
            p = p + b_ref[:, c0:c0 + cb]
            out.append(jnp.concatenate(
                [p[0:halo] * keep_prev, p[halo:halo + rows], p[halo + rows:] * keep_next], axis=0))
        return out

    def conv(p, c0):
        return _conv3(p, cw_ref[:, c0:c0 + cb], cb_ref[:, c0:c0 + cb], halo, rows)

    for c in range(dh // cb):
        lanes = slice(c * cb, (c + 1) * cb)
        x1, v = project(dh + c * cb, 2 * dh + c * cb)
        u_ref[:, lanes] = conv(x1, dh + c * cb) * conv(v, 2 * dh + c * cb)
        x0, z = project(c * cb, 3 * dh + c * cb)
        gate_ref[:, lanes] = (conv(x0, c * cb) * jax.nn.silu(z[halo:halo + rows])).astype(BF16)

    h_main = h[halo:halo + rows]
    for c0 in range(0, p_ref.shape[1], group):
        src = 4 * dh + c0
        res = _dot(h_main, w_ref[:, src:src + group]) + b_ref[:, src:src + group]
        p_ref[:, c0:c0 + group] = res.astype(BF16)


def _inproj(x2, g_norm, w_in_bf, b_in, conv_w, conv_b, seq):
    n_tok, d = x2.shape
    cols = w_in_bf.shape[1]
    dh = conv_b.shape[1] // 3
    n_plain = cols - 4 * dh
    rows = ROW_CHUNK
    halo = 2 * SUBLANES
    per_halo = rows // halo
    last_halo = n_tok // halo - 1
    full = lambda a: pl.BlockSpec(a.shape, lambda i: (0,) * a.ndim)
    tile = lambda width: pl.BlockSpec((rows, width), lambda i: (i, 0))
    return pl.pallas_call(
        functools.partial(_inproj_kernel, chunks_per_seq=seq // rows),
        grid=(n_tok // rows,),
        in_specs=[
            pl.BlockSpec((halo, d), lambda i: (jnp.maximum(i * per_halo - 1, 0), 0)),
            tile(d),
            pl.BlockSpec((halo, d), lambda i: (jnp.minimum((i + 1) * per_halo, last_halo), 0)),
            full(g_norm),
            pl.BlockSpec(w_in_bf.shape, lambda i: (0, 0), pipeline_mode=pl.Buffered(1)),
            full(b_in), full(conv_w), full(conv_b),
        ],
        out_specs=[tile(dh), tile(dh), tile(n_plain)],
        out_shape=[
            jax.ShapeDtypeStruct((n_tok, dh), F32),
            jax.ShapeDtypeStruct((n_tok, dh), BF16),
            jax.ShapeDtypeStruct((n_tok, n_plain), BF16),
        ],
        compiler_params=pltpu.CompilerParams(
            dimension_semantics=("parallel",), vmem_limit_bytes=VMEM_LIMIT),
        name="inproj",
    )(x2, x2, x2, g_norm, w_in_bf, b_in, conv_w, conv_b)


def _hyena_kernel(u_ref, gate_ref, d_ref, kre_ref, kim_ref, cs_ref, y_ref, ub_ref, av_ref, pq_ref):
    seq = u_ref.shape[0]
    lb = cs_ref.shape[0]
    nb = seq // lb
    mid = nb - 1
    blocks = [slice(b * lb, (b + 1) * lb) for b in range(nb)]

    for rows in blocks:
        ub_ref[rows, :] = u_ref[rows, :].astype(BF16)

    for b, rows in enumerate(blocks):
        ub = ub_ref[rows, :]
        av_ref[2 * b] = _dot(cs_ref[:, 0:lb], ub)
        av_ref[2 * b + 1] = _dot(cs_ref[:, lb:2 * lb], ub)

    def spectral(i, carry):
        rows = _chunk(i, SPEC_ROWS)
        a_blk = [av_ref[2 * b, rows, :] for b in range(nb)]
        v_blk = [av_ref[2 * b + 1, rows, :] for b in range(nb)]
        for a in range(nb):
            p = q = None
            for b in range(nb):
                kr = kre_ref[mid + a - b, rows, :]
                ki = kim_ref[mid + a - b, rows, :]
                tp = kr * a_blk[b] + ki * v_blk[b]
                tq = kr * v_blk[b] - ki * a_blk[b]
                p = tp if p is None else p + tp
                q = tq if q is None else q + tq
            pq_ref[a, rows, :] = p.astype(BF16)
            pq_ref[a, pl.ds(pl.multiple_of(lb + i * SPEC_ROWS, SPEC_ROWS), SPEC_ROWS), :] = (
                q.astype(BF16))
        return carry

    lax.fori_loop(0, lb // SPEC_ROWS, spectral, 0)

    for a, rows in enumerate(blocks):
        y = _dot(cs_ref[...], pq_ref[a])
        y = y + u_ref[rows, :] * d_ref[...]
        y_ref[rows, :] = (y * gate_ref[rows, :].astype(F32)).astype(BF16)


def _hyena(u, gate, hyena_d, kre, kim, cs, batch, seq):
    dh = hyena_d.shape[1]
    cw = HYENA_COLS
    lb = cs.shape[0]
    nb = seq // lb
    act = pl.BlockSpec((seq, cw), lambda c, b: (b, c))
    spectra = pl.BlockSpec((2 * nb - 1, lb, cw), lambda c, b: (0, 0, c),
                           pipeline_mode=pl.Buffered(1))
    return pl.pallas_call(
        _hyena_kernel,
        grid=(dh // cw, batch),
        in_specs=[
            act, act,
            pl.BlockSpec((1, cw), lambda c, b: (0, c)),
            spectra, spectra,
            pl.BlockSpec(cs.shape, lambda c, b: (0, 0)),
        ],
        out_specs=act,
        out_shape=jax.ShapeDtypeStruct((batch * seq, dh), BF16),
        scratch_shapes=[
            pltpu.VMEM((seq, cw), BF16),
            pltpu.VMEM((2 * nb, lb, cw), F32),
            pltpu.VMEM((nb, 2 * lb, cw), BF16),
        ],
        compiler_params=pltpu.CompilerParams(
            dimension_semantics=("arbitrary", "arbitrary"), vmem_limit_bytes=VMEM_LIMIT),
        name="hyena",
    )(u, gate, hyena_d, kre, kim, cs)


def _pool_kernel(u_ref, z_ref, w_ref, b_ref, s_ref, y_ref):
    seq = u_ref.shape[0]
    gw = w_ref.shape[1]
    sl = SUBLANES
    pad = 2 * sl
    ext = seq + pad
    edge = lax.broadcasted_iota(jnp.int32, (sl, gw), 0)
    assert max(POOL_WINDOWS) // 2 <= sl
    for g, win in enumerate(POOL_WINDOWS):
        cols = slice(g * gw, (g + 1) * gw)
        half = win // 2
        u = u_ref[:, cols].astype(F32)
        s = jnp.concatenate([u, jnp.zeros((pad, gw), F32)], axis=0)
        w = 1
        while w < win:
            s = s + pltpu.roll(s, w, 0)
            w *= 2
        if half > 1:
            s = pltpu.roll(s, ext - (half - 1), 0)
        total = s[0:seq]
        cnt_first = jnp.minimum(edge + half, win).astype(F32)
        cnt_last = jnp.minimum(sl - edge + half, win).astype(F32)
        mean = jnp.concatenate([total[0:sl] / cnt_first,
                                total[sl:seq - sl] * (1.0 / win),
                                total[seq - sl:] / cnt_last], axis=0)
        pooled = (mean - u).astype(BF16)
        y = _dot(pooled, w_ref[g]) + b_ref[:, cols]
        y = y * s_ref[:, cols] * jax.nn.silu(z_ref[:, cols].astype(F32))
        y_ref[:, cols] = y.astype(BF16)


def _pool(plain, pool_w_bf, pool_b, pool_scale, batch, seq):
    dp = pool_b.shape[1]
    full = lambda a: pl.BlockSpec(a.shape, lambda b: (0,) * a.ndim)
    return pl.pallas_call(
        _pool_kernel,
        grid=(batch,),
        in_specs=[
            pl.BlockSpec((seq, dp), lambda b: (b, 0)),
            pl.BlockSpec((seq, dp), lambda b: (b, 1)),
            full(pool_w_bf), full(pool_b), full(pool_scale),
        ],
        out_specs=pl.BlockSpec((seq, dp), lambda b: (b, 0)),
        out_shape=jax.ShapeDtypeStruct((batch * seq, dp), BF16),
        compiler_params=pltpu.CompilerParams(
            dimension_semantics=("parallel",), vmem_limit_bytes=VMEM_LIMIT),
        name="pool",
    )(plain, plain, pool_w_bf, pool_b, pool_scale)


def _merge_kernel(yh_ref, yp_ref, g0_ref, g1_ref, x_ref, wh_ref, wp_ref, wo_ref, gf_ref, o_ref):
    out_h = _dot(yh_ref[...], wh_ref[...])
    out_p = _dot(yp_ref[...], wp_ref[...])
    merged = (jax.nn.sigmoid(g0_ref[...].astype(F32)) * out_h
              + jax.nn.sigmoid(g1_ref[...].astype(F32)) * out_p)
    res = x_ref[...] + _dot(merged.astype(BF16), wo_ref[...])
    o_ref[...] = _rms_scale(res) * gf_ref[...]


def _merge(yh, yp, plain, x2, wh_bf, wp_bf, wo_bf, g_final, tm=1024):
    n_tok, d = x2.shape
    tile = lambda col: pl.BlockSpec((tm, d), lambda i: (i, col))
    resident = lambda a: pl.BlockSpec(a.shape, lambda i: (0, 0), pipeline_mode=pl.Buffered(1))
    return pl.pallas_call(
        _merge_kernel,
        grid=(n_tok // tm,),
        in_specs=[
            tile(0), tile(0), tile(2), tile(3), tile(0),
            resident(wh_bf), resident(wp_bf), resident(wo_bf),
            pl.BlockSpec((1, d), lambda i: (0, 0)),
        ],
        out_specs=tile(0),
        out_shape=jax.ShapeDtypeStruct((n_tok, d), F32),
        compiler_params=pltpu.CompilerParams(
            dimension_semantics=("parallel",), vmem_limit_bytes=VMEM_LIMIT),
        name="merge",
    )(yh, yp, plain, plain, x2, wh_bf, wp_bf, wo_bf, g_final)


def _filter_features(seq):
    t = jnp.linspace(0.0, 1.0, seq, dtype=F32)[:, None]
    bands = (FILTER_EMB - 1) // 2
    w = 2.0 * math.pi * jnp.arange(seq, dtype=F32) / seq
    f = jnp.linspace(1e-4, bands - 1, bands, dtype=F32)
    ang = w[:, None] * f[None, :]
    return jnp.concatenate([t, jnp.cos(ang), -jnp.sin(ang)], axis=-1)


def kernel(x, g_norm, w_in, b_in, conv_w, conv_b, filt_w1, filt_b1, filt_w2, filt_b2, filt_w3, filt_b3, filt_w4, filt_freq, hyena_d, w_hyena_out, pool_w, pool_b, pool_scale, w_pool_out, w_out, g_final):
    batch, seq, d = x.shape
    dh = hyena_d.shape[1]
    dp = pool_b.shape[1]
    assert g_norm.shape[0] == 1, "single-layer block"
    assert d == dh == dp and conv_b.shape[1] == 3 * dh and w_in.shape[2] == 4 * dh + 2 * dp + 2 * d
    assert dh % HYENA_COLS == 0 and seq % ROW_CHUNK == 0
    assert seq % CONV_BLOCKS == 0 and (seq // CONV_BLOCKS) % SPEC_ROWS == 0

    cs = _transform_matrices(seq // CONV_BLOCKS)
    z_pad = jnp.pad(_filter_features(seq), ((0, 0), (0, LANES - FILTER_EMB)))
    w1_pad = jnp.pad(filt_w1[0], ((0, LANES - FILTER_EMB), (0, 0)))
    deltas = jnp.abs(jnp.linspace(MIN_DECAY, MAX_DECAY, dh, dtype=F32))[None, :]
    kre, kim = _filter_spectra(z_pad, w1_pad, filt_b1, filt_w2[0], filt_b2, filt_w3[0], filt_b3,
                               filt_freq, filt_w4[0], deltas, cs)

    x2 = x.reshape(batch * seq, d)
    u, gate, plain = _inproj(x2, g_norm, w_in[0].astype(BF16), b_in, conv_w[0], conv_b, seq)
    yh = _hyena(u, gate, hyena_d, kre, kim, cs, batch, seq)
    yp = _pool(plain, pool_w[0].astype(BF16), pool_b, pool_scale, batch, seq)
    out = _merge(yh, yp, plain, x2, w_hyena_out[0].astype(BF16), w_pool_out[0].astype(BF16),
                 w_out[0].astype(BF16), g_final[None, :])
    return out.reshape(batch, seq, d)
```

```python
import functools
import math

import jax
import jax.numpy as jnp
from jax import lax
from jax.experimental import pallas as pl
from jax.experimental.pallas import tpu as pltpu

FILTER_EMB = 33
FAST_DECAY_PCT = 0.3
SLOW_DECAY_PCT = 1.5
DECAY_TARGET = 1e-2
MAX_DECAY = math.log(DECAY_TARGET) / FAST_DECAY_PCT
MIN_DECAY = math.log(DECAY_TARGET) / SLOW_DECAY_PCT
POOL_WINDOWS = (2, 4, 8, 16)
NORM_EPS = 1e-6

LANES = 128
SUBLANES = 8
MXU_DIM = 256
HYENA_COLS = 2 * MXU_DIM
ROW_CHUNK = 512
CONV_BLOCKS = 4
SPEC_ROWS = 2 * SUBLANES
VMEM_LIMIT = 56 * 1024 * 1024

F32 = jnp.float32
BF16 = jnp.bfloat16


def _dot(a, b):
    return jnp.dot(a, b, preferred_element_type=F32)


def _dot_hi(a, b):
    return jnp.dot(a, b, preferred_element_type=F32, precision=lax.Precision.HIGHEST)


def _rms_scale(x):
    return x * lax.rsqrt(jnp.mean(x * x, axis=-1, keepdims=True) + NORM_EPS)


def _transform_matrices(lb):
    period = 8 * lb
    odd = 2 * jnp.arange(lb, dtype=jnp.int32) + 1
    ang = ((odd[:, None] * odd[None, :]) % period).astype(F32) * (2.0 * math.pi / period)
    return jnp.stack([jnp.cos(ang), jnp.sin(ang)]).astype(BF16)


def _filter_kernel(z_ref, w1_ref, b1_ref, w2_ref, b2_ref, w3_ref, b3_ref, fr_ref,
                   w4f_ref, w4b_ref, dl_ref, cs_ref, kre_ref, kim_ref,
                   h_ref, cphi_ref, sphi_ref, sgn_ref, kk_ref, xr_ref, xi_ref):
    seq = z_ref.shape[0]
    lb = cs_ref.shape[1]
    nb = seq // lb
    cb = dl_ref.shape[1]
    scale = 1.0 / lb

    @pl.when(pl.program_id(0) == 0)
    def _():
        fr = fr_ref[...]
        h = jnp.sin(fr * (_dot_hi(z_ref[...], w1_ref[...]) + b1_ref[...]))
        h = jnp.sin(fr * (_dot_hi(h, w2_ref[...]) + b2_ref[...]))
        h_ref[...] = jnp.sin(fr * (_dot_hi(h, w3_ref[...]) + b3_ref[...]))
        f_idx = lax.broadcasted_iota(jnp.int32, cphi_ref.shape, 0)
        phi = (f_idx.astype(F32) + 0.5) * (math.pi / (2 * lb))
        cphi_ref[...] = jnp.cos(phi) * scale
        sphi_ref[...] = jnp.sin(phi) * scale
        sgn_ref[...] = (1 - 2 * (f_idx & 1)).astype(F32)

    h = h_ref[...]
    decay = jnp.exp(-z_ref[:, 0:1] * dl_ref[...])
    taps = (_dot_hi(h, w4f_ref[...]) * decay, _dot_hi(h, w4b_ref[...]) * decay)
    taps = tuple(k.astype(BF16) for k in taps)
    for side, k in enumerate(taps):
        for s_ in range(nb):
            c0 = (side * nb + s_) * cb
            kk_ref[:, c0:c0 + cb] = k[s_ * lb:(s_ + 1) * lb]

    group = 2 * cb
    tile = lambda ref, width: jnp.concatenate([ref[...]] * (width // LANES), axis=1)
    c_phi, s_phi = tile(cphi_ref, group), tile(sphi_ref, group)
    for g in range(2 * nb * cb // group):
        cols = slice(g * group, (g + 1) * group)
        kk = kk_ref[:, cols]
        a = _dot(cs_ref[0], kk)
        b = _dot(cs_ref[1], kk)
        xr_ref[:, cols] = c_phi * a + s_phi * b
        xi_ref[:, cols] = s_phi * a - c_phi * b

    sgn = tile(sgn_ref, cb)

    def seg(ref, side, s_):
        c0 = (side * nb + s_) * cb
        return ref[:, c0:c0 + cb]

    def causal_block(side, delta):
        r0 = (delta - 1) * lb
        first = taps[side][r0:r0 + 1].astype(F32) * scale
        re = seg(xr_ref, side, delta) - sgn * seg(xi_ref, side, delta - 1)
        im = seg(xi_ref, side, delta) + sgn * (seg(xr_ref, side, delta - 1) - first)
        return re, im

    mid = nb - 1
    kre_ref[mid] = seg(xr_ref, 0, 0) + seg(xr_ref, 1, 0)
    kim_ref[mid] = seg(xi_ref, 0, 0) - seg(xi_ref, 1, 0)
    for delta in range(1, nb):
        re, im = causal_block(0, delta)
        kre_ref[mid + delta] = re
        kim_ref[mid + delta] = im
        re, im = causal_block(1, delta)
        kre_ref[mid - delta] = re
        kim_ref[mid - delta] = -im


def _filter_spectra(z_pad, w1_pad, b1, w2, b2, w3, b3, freq, w4, deltas, cs):
    seq = z_pad.shape[0]
    lb = cs.shape[1]
    nb = seq // lb
    hf = w2.shape[0]
    dh = deltas.shape[1]
    cb = MXU_DIM
    nblk = dh // cb
    full = lambda a: pl.BlockSpec(a.shape, lambda c: (0,) * a.ndim)
    return pl.pallas_call(
        _filter_kernel,
        grid=(nblk,),
        in_specs=[
            full(z_pad), full(w1_pad), full(b1), full(w2), full(b2), full(w3), full(b3), full(freq),
            pl.BlockSpec((hf, cb), lambda c: (0, c)),
            pl.BlockSpec((hf, cb), lambda c: (0, nblk + c)),
            pl.BlockSpec((1, cb), lambda c: (0, c)),
            full(cs),
        ],
        out_specs=[pl.BlockSpec((2 * nb - 1, lb, cb), lambda c: (0, 0, c))] * 2,
        out_shape=[jax.ShapeDtypeStruct((2 * nb - 1, lb, dh), F32)] * 2,
        scratch_shapes=[
            pltpu.VMEM((seq, hf), F32),
            pltpu.VMEM((lb, LANES), F32),
            pltpu.VMEM((lb, LANES), F32),
            pltpu.VMEM((lb, LANES), F32),
            pltpu.VMEM((lb, 2 * nb * cb), BF16),
            pltpu.VMEM((lb, 2 * nb * cb), F32),
            pltpu.VMEM((lb, 2 * nb * cb), F32),
        ],
        compiler_params=pltpu.CompilerParams(
            dimension_semantics=("arbitrary",), vmem_limit_bytes=VMEM_LIMIT),
        name="filter_spectra",
    )(z_pad, w1_pad, b1, w2, b2, w3, b3, freq, w4, w4, deltas, cs)


def _conv3(p, w, bias, lo, n):
    n_rows = p.shape[0]
    assert 1 <= lo and lo + n < n_rows
    dn = pltpu.roll(p, 1, 0)
    up = pltpu.roll(p, n_rows - 1, 0)
    out = dn * w[0:1] + p * w[1:2] + up * w[2:3] + bias
    return out[lo:lo + n]


def _inproj_kernel(xp_ref, x_ref, xn_ref, g_ref, w_ref, b_ref, cw_ref, cb_ref,
                   u_ref, gate_ref, p_ref, *, chunks_per_seq):
    i = pl.program_id(0)
    rows = x_ref.shape[0]
    halo = xp_ref.shape[0]
    dh = u_ref.shape[1]
    cb = MXU_DIM
    group = 2 * cb
    keep_prev = (i % chunks_per_seq != 0).astype(F32)
    keep_next = (i % chunks_per_seq != chunks_per_seq - 1).astype(F32)

    xw = jnp.concatenate([xp_ref[...], x_ref[...], xn_ref[...]], axis=0)
    h = (_rms_scale(xw) * g_ref[...]).astype(BF16)

    def project(col_a, col_b):
        w = jnp.concatenate([w_ref[:, col_a:col_a + cb], w_ref[:, col_b:col_b + cb]], axis=1)
        res = _dot(h, w)
        out = []
        for p, c0 in ((res[:, 0:cb], col_a), (res[:, cb:group], col_b)):
            p = p + b_ref[:, c0:c0 + cb]
            out.append(jnp.concatenate(
                [p[0:halo] * keep_prev, p[halo:halo + rows], p[halo + rows:] * keep_next], axis=0))
        return out

    def conv(p, c0):
        return _conv3(p, cw_ref[:, c0:c0 + cb], cb_ref[:, c0:c0 + cb], halo, rows)

    for c in range(dh // cb):
        lanes = slice(c * cb, (c + 1) * cb)
        x1, v = project(dh + c * cb, 2 * dh + c * cb)
        u_ref[:, lanes] = conv(x1, dh + c * cb) * conv(v, 2 * dh + c * cb)
        x0, z = project(c * cb, 3 * dh + c * cb)
        gate_ref[:, lanes] = (conv(x0, c * cb) * jax.nn.silu(z[halo:halo + rows])).astype(BF16)

    h_main = h[halo:halo + rows]
    for c0 in range(0, p_ref.shape[1], group):
        src = 4 * dh + c0
        res = _dot(h_main, w_ref[:, src:src + group]) + b_ref[:, src:src + group]
        p_ref[:, c0:c0 + group] = res.astype(BF16)


def _inproj(x2, g_norm, w_in_bf, b_in, conv_w, conv_b, seq):
    n_tok, d = x2.shape
    cols = w_in_bf.shape[1]
    dh = conv_b.shape[1] // 3
    n_plain = cols - 4 * dh
    rows = ROW_CHUNK
    halo = 2 * SUBLANES
    per_halo = rows // halo
    last_halo = n_tok // halo - 1
    full = lambda a: pl.BlockSpec(a.shape, lambda i: (0,) * a.ndim)
    tile = lambda width: pl.BlockSpec((rows, width), lambda i: (i, 0))
    return pl.pallas_call(
        functools.partial(_inproj_kernel, chunks_per_seq=seq // rows),
        grid=(n_tok // rows,),
        in_specs=[
            pl.BlockSpec((halo, d), lambda i: (jnp.maximum(i * per_halo - 1, 0), 0)),
            tile(d),
            pl.BlockSpec((halo, d), lambda i: (jnp.minimum((i + 1) * per_halo, last_halo), 0)),
            full(g_norm),
            pl.BlockSpec(w_in_bf.shape, lambda i: (0, 0), pipeline_mode=pl.Buffered(1)),
            full(b_in), full(conv_w), full(conv_b),
        ],
        out_specs=[tile(dh), tile(dh), tile(n_plain)],
        out_shape=[
            jax.ShapeDtypeStruct((n_tok, dh), F32),
            jax.ShapeDtypeStruct((n_tok, dh), BF16),
            jax.ShapeDtypeStruct((n_tok, n_plain), BF16),
        ],
        compiler_params=pltpu.CompilerParams(
            dimension_semantics=("parallel",), vmem_limit_bytes=VMEM_LIMIT),
        name="inproj",
    )(x2, x2, x2, g_norm, w_in_bf, b_in, conv_w, conv_b)


def _hyena_kernel(u_ref, gate_ref, d_ref, kre_ref, kim_ref, cs_ref, y_ref,
                  ub_ref, av0_ref, av1_ref, pq0_ref, pq1_ref):
    seq, cw = u_ref.shape
    lb = cs_ref.shape[1]
    nb = seq // lb
    mid = nb - 1
    spec_rows = lb // nb
    halves = [(slice(0, MXU_DIM), av0_ref, pq0_ref), (slice(MXU_DIM, cw), av1_ref, pq1_ref)]
    assert cw == 2 * MXU_DIM

    def block_rows(b):
        return pl.ds(pl.multiple_of(b * lb, lb), lb)

    def forward(b, half):
        cols, av_ref, _ = half
        ub = ub_ref[block_rows(b), cols]
        av_ref[2 * b] = _dot(cs_ref[0], ub)
        av_ref[2 * b + 1] = _dot(cs_ref[1], ub)

    def spectral(base, half):
        cols, av_ref, pq_ref = half
        tiles = [(r0, c0) for c0 in range(0, cols.stop - cols.start, LANES)
                 for r0 in range(0, spec_rows, SPEC_ROWS)]
        for r0, c0 in tiles:
            rows = pl.ds(pl.multiple_of(base + r0, SPEC_ROWS), SPEC_ROWS)
            lanes = slice(c0, c0 + LANES)
            k_lanes = slice(cols.start + c0, cols.start + c0 + LANES)
            a_blk = [av_ref[2 * b, rows, lanes] for b in range(nb)]
            v_blk = [av_ref[2 * b + 1, rows, lanes] for b in range(nb)]
            p = [None] * nb
            q = [None] * nb
            for delta in range(-mid, nb):
                kr = kre_ref[mid + delta, rows, k_lanes]
                ki = kim_ref[mid + delta, rows, k_lanes]
                for a in range(max(delta, 0), min(nb + delta, nb)):
                    b = a - delta
                    tp = kr * a_blk[b] + ki * v_blk[b]
                    tq = kr * v_blk[b] - ki * a_blk[b]
                    p[a] = tp if p[a] is None else p[a] + tp
                    q[a] = tq if q[a] is None else q[a] + tq
            for a in range(nb):
                pq_ref[a, 0, rows, lanes] = p[a].astype(BF16)
                pq_ref[a, 1, rows, lanes] = q[a].astype(BF16)

    def inverse(a, half):
        cols, _, pq_ref = half
        rows = block_rows(a)
        y = _dot(cs_ref[0], pq_ref[a, 0]) + _dot(cs_ref[1], pq_ref[a, 1])
        y = y + u_ref[rows, cols] * d_ref[:, cols]
        y_ref[rows, cols] = (y * gate_ref[rows, cols].astype(F32)).astype(BF16)

    ub_ref[...] = u_ref[...].astype(BF16)

    first, second = halves
    for b in range(nb):
        forward(b, first)

    def spectral_first(i, carry):
        spectral(i * spec_rows, first)
        forward(i, second)
        return carry

    lax.fori_loop(0, nb, spectral_first, 0)

    def spectral_second(i, carry):
        inverse(i, first)
        spectral(i * spec_rows, second)
        return carry

    lax.fori_loop(0, nb, spectral_second, 0)
    for a in range(nb):
        inverse(a, second)


def _hyena(u, gate, hyena_d, kre, kim, cs, batch, seq):
    dh = hyena_d.shape[1]
    cw = HYENA_COLS
    lb = cs.shape[1]
    nb = seq // lb
    act = pl.BlockSpec((seq, cw), lambda c, b: (b, c))
    spectra = pl.BlockSpec((2 * nb - 1, lb, cw), lambda c, b: (0, 0, c),
                           pipeline_mode=pl.Buffered(1))
    return pl.pallas_call(
        _hyena_kernel,
        grid=(dh // cw, batch),
        in_specs=[
            act, act,
            pl.BlockSpec((1, cw), lambda c, b: (0, c)),
            spectra, spectra,
            pl.BlockSpec(cs.shape, lambda c, b: (0, 0, 0)),
        ],
        out_specs=act,
        out_shape=jax.ShapeDtypeStruct((batch * seq, dh), BF16),
        scratch_shapes=[
            pltpu.VMEM((seq, cw), BF16),
            pltpu.VMEM((2 * nb, lb, cw // 2), F32),
            pltpu.VMEM((2 * nb, lb, cw // 2), F32),
            pltpu.VMEM((nb, 2, lb, cw // 2), BF16),
            pltpu.VMEM((nb, 2, lb, cw // 2), BF16),
        ],
        compiler_params=pltpu.CompilerParams(
            dimension_semantics=("arbitrary", "arbitrary"), vmem_limit_bytes=VMEM_LIMIT),
        name="hyena",
    )(u, gate, hyena_d, kre, kim, cs)


def _pool_kernel(u_ref, z_ref, w_ref, b_ref, s_ref, y_ref):
    seq = u_ref.shape[0]
    gw = w_ref.shape[1]
    sl = SUBLANES
    pad = 2 * sl
    ext = seq + pad
    edge = lax.broadcasted_iota(jnp.int32, (sl, gw), 0)
    assert max(POOL_WINDOWS) // 2 <= sl
    for g, win in enumerate(POOL_WINDOWS):
        cols = slice(g * gw, (g + 1) * gw)
        half = win // 2
        u = u_ref[:, cols].astype(F32)
        s = jnp.concatenate([u, jnp.zeros((pad, gw), F32)], axis=0)
        w = 1
        while w < win:
            s = s + pltpu.roll(s, w, 0)
            w *= 2
        if half > 1:
            s = pltpu.roll(s, ext - (half - 1), 0)
        total = s[0:seq]
        cnt_first = jnp.minimum(edge + half, win).astype(F32)
        cnt_last = jnp.minimum(sl - edge + half, win).astype(F32)
        mean = jnp.concatenate([total[0:sl] / cnt_first,
                                total[sl:seq - sl] * (1.0 / win),
                                total[seq - sl:] / cnt_last], axis=0)
        pooled = (mean - u).astype(BF16)
        y = _dot(pooled, w_ref[g]) + b_ref[:, cols]
        y = y * s_ref[:, cols] * jax.nn.silu(z_ref[:, cols].astype(F32))
        y_ref[:, cols] = y.astype(BF16)


def _pool(plain, pool_w_bf, pool_b, pool_scale, batch, seq):
    dp = pool_b.shape[1]
    full = lambda a: pl.BlockSpec(a.shape, lambda b: (0,) * a.ndim)
    return pl.pallas_call(
        _pool_kernel,
        grid=(batch,),
        in_specs=[
            pl.BlockSpec((seq, dp), lambda b: (b, 0)),
            pl.BlockSpec((seq, dp), lambda b: (b, 1)),
            full(pool_w_bf), full(pool_b), full(pool_scale),
        ],
        out_specs=pl.BlockSpec((seq, dp), lambda b: (b, 0)),
        out_shape=jax.ShapeDtypeStruct((batch * seq, dp), BF16),
        compiler_params=pltpu.CompilerParams(
            dimension_semantics=("parallel",), vmem_limit_bytes=VMEM_LIMIT),
        name="pool",
    )(plain, plain, pool_w_bf, pool_b, pool_scale)


def _merge_kernel(yh_ref, yp_ref, g0_ref, g1_ref, x_ref, wh_ref, wp_ref, wo_ref, gf_ref, o_ref):
    out_h = _dot(yh_ref[...], wh_ref[...])
    out_p = _dot(yp_ref[...], wp_ref[...])
    merged = (jax.nn.sigmoid(g0_ref[...].astype(F32)) * out_h
              + jax.nn.sigmoid(g1_ref[...].astype(F32)) * out_p)
    res = x_ref[...] + _dot(merged.astype(BF16), wo_ref[...])
    o_ref[...] = _rms_scale(res) * gf_ref[...]


def _merge(yh, yp, plain, x2, wh_bf, wp_bf, wo_bf, g_final, tm=1024):
    n_tok, d = x2.shape
    tile = lambda col: pl.BlockSpec((tm, d), lambda i: (i, col))
    resident = lambda a: pl.BlockSpec(a.shape, lambda i: (0, 0), pipeline_mode=pl.Buffered(1))
    return pl.pallas_call(
        _merge_kernel,
        grid=(n_tok // tm,),
        in_specs=[
            tile(0), tile(0), tile(2), tile(3), tile(0),
            resident(wh_bf), resident(wp_bf), resident(wo_bf),
            pl.BlockSpec((1, d), lambda i: (0, 0)),
        ],
        out_specs=tile(0),
        out_shape=jax.ShapeDtypeStruct((n_tok, d), F32),
        compiler_params=pltpu.CompilerParams(
            dimension_semantics=("parallel",), vmem_limit_bytes=VMEM_LIMIT),
        name="merge",
    )(yh, yp, plain, plain, x2, wh_bf, wp_bf, wo_bf, g_final)


def _filter_features(seq):
    t = jnp.linspace(0.0, 1.0, seq, dtype=F32)[:, None]
    bands = (FILTER_EMB - 1) // 2
    w = 2.0 * math.pi * jnp.arange(seq, dtype=F32) / seq
    f = jnp.linspace(1e-4, bands - 1, bands, dtype=F32)
    ang = w[:, None] * f[None, :]
    return jnp.concatenate([t, jnp.cos(ang), -jnp.sin(ang)], axis=-1)


def kernel(x, g_norm, w_in, b_in, conv_w, conv_b, filt_w1, filt_b1, filt_w2, filt_b2, filt_w3, filt_b3, filt_w4, filt_freq, hyena_d, w_hyena_out, pool_w, pool_b, pool_scale, w_pool_out, w_out, g_final):
    batch, seq, d = x.shape
    dh = hyena_d.shape[1]
    dp = pool_b.shape[1]
    assert g_norm.shape[0] == 1, "single-layer block"
    assert d == dh == dp and conv_b.shape[1] == 3 * dh and w_in.shape[2] == 4 * dh + 2 * dp + 2 * d
    assert dh % HYENA_COLS == 0 and seq % ROW_CHUNK == 0
    assert seq % CONV_BLOCKS == 0 and (seq // CONV_BLOCKS) % SPEC_ROWS == 0

    cs = _transform_matrices(seq // CONV_BLOCKS)
    z_pad = jnp.pad(_filter_features(seq), ((0, 0), (0, LANES - FILTER_EMB)))
    w1_pad = jnp.pad(filt_w1[0], ((0, LANES - FILTER_EMB), (0, 0)))
    deltas = jnp.abs(jnp.linspace(MIN_DECAY, MAX_DECAY, dh, dtype=F32))[None, :]
    kre, kim = _filter_spectra(z_pad, w1_pad, filt_b1, filt_w2[0], filt_b2, filt_w3[0], filt_b3,
                               filt_freq, filt_w4[0], deltas, cs)

    x2 = x.reshape(batch * seq, d)
    u, gate, plain = _inproj(x2, g_norm, w_in[0].astype(BF16), b_in, conv_w[0], conv_b, seq)
    yh = _hyena(u, gate, hyena_d, kre, kim, cs, batch, seq)
    yp = _pool(plain, pool_w[0].astype(BF16), pool_b, pool_scale, batch, seq)
    out = _merge(yh, yp, plain, x2, w_hyena_out[0].astype(BF16), w_pool_out[0].astype(BF16),
                 w_out[0].astype(BF16), g_final[None, :])
    return out.reshape(batch, seq, d)
```

```python
import functools
import math

import jax
import jax.numpy as jnp
from jax import lax
from jax.experimental import pallas as pl
from jax.experimental.pallas import tpu as pltpu

FILTER_EMB = 33
FAST_DECAY_PCT = 0.3
SLOW_DECAY_PCT = 1.5
DECAY_TARGET = 1e-2
MAX_DECAY = math.log(DECAY_TARGET) / FAST_DECAY_PCT
MIN_DECAY = math.log(DECAY_TARGET) / SLOW_DECAY_PCT
POOL_WINDOWS = (2, 4, 8, 16)
NORM_EPS = 1e-6

LANES = 128
SUBLANES = 8
MXU_DIM = 256
HYENA_COLS = 2 * MXU_DIM
ROW_CHUNK = 512
CONV_BLOCKS = 4
SPEC_ROWS = 64
VMEM_LIMIT = 56 * 1024 * 1024

F32 = jnp.float32
BF16 = jnp.bfloat16


def _dot(a, b):
    return jnp.dot(a, b, preferred_element_type=F32)


def _dot_hi(a, b):
    return jnp.dot(a, b, preferred_element_type=F32, precision=lax.Precision.HIGHEST)


def _rms_scale(x):
    return x * lax.rsqrt(jnp.mean(x * x, axis=-1, keepdims=True) + NORM_EPS)


def _chunk(i, size):
    return pl.ds(pl.multiple_of(i * size, size), size)


def _transform_matrices(lb):
    period = 8 * lb
    odd = 2 * jnp.arange(lb, dtype=jnp.int32) + 1
    ang = ((odd[:, None] * odd[None, :]) % period).astype(F32) * (2.0 * math.pi / period)
    return jnp.concatenate([jnp.cos(ang), jnp.sin(ang)], axis=1).astype(BF16)


def _filter_kernel(z_ref, w1_ref, b1_ref, w2_ref, b2_ref, w3_ref, b3_ref, fr_ref,
                   w4f_ref, w4b_ref, dl_ref, cs_ref, kre_ref, kim_ref,
                   h_ref, cphi_ref, sphi_ref, sgn_ref, kk_ref, xr_ref, xi_ref):
    seq = z_ref.shape[0]
    lb = cs_ref.shape[0]
    nb = seq // lb
    cb = dl_ref.shape[1]
    scale = 1.0 / lb

    @pl.when(pl.program_id(0) == 0)
    def _():
        fr = fr_ref[...]
        h = jnp.sin(fr * (_dot_hi(z_ref[...], w1_ref[...]) + b1_ref[...]))
        h = jnp.sin(fr * (_dot_hi(h, w2_ref[...]) + b2_ref[...]))
        h_ref[...] = jnp.sin(fr * (_dot_hi(h, w3_ref[...]) + b3_ref[...]))
        f_idx = lax.broadcasted_iota(jnp.int32, cphi_ref.shape, 0)
        phi = (f_idx.astype(F32) + 0.5) * (math.pi / (2 * lb))
        cphi_ref[...] = jnp.cos(phi) * scale
        sphi_ref[...] = jnp.sin(phi) * scale
        sgn_ref[...] = (1 - 2 * (f_idx & 1)).astype(F32)

    h = h_ref[...]
    decay = jnp.exp(-z_ref[:, 0:1] * dl_ref[...])
    taps = (_dot_hi(h, w4f_ref[...]) * decay, _dot_hi(h, w4b_ref[...]) * decay)
    taps = tuple(k.astype(BF16) for k in taps)
    for side, k in enumerate(taps):
        for s_ in range(nb):
            c0 = (side * nb + s_) * cb
            kk_ref[:, c0:c0 + cb] = k[s_ * lb:(s_ + 1) * lb]

    group = 2 * cb
    tile = lambda ref, width: jnp.concatenate([ref[...]] * (width // LANES), axis=1)
    c_phi, s_phi = tile(cphi_ref, group), tile(sphi_ref, group)
    for g in range(2 * nb * cb // group):
        cols = slice(g * group, (g + 1) * group)
        kk = kk_ref[:, cols]
        a = _dot(cs_ref[:, 0:lb], kk)
        b = _dot(cs_ref[:, lb:2 * lb], kk)
        xr_ref[:, cols] = c_phi * a + s_phi * b
        xi_ref[:, cols] = s_phi * a - c_phi * b

    sgn = tile(sgn_ref, cb)

    def seg(ref, side, s_):
        c0 = (side * nb + s_) * cb
        return ref[:, c0:c0 + cb]

    def causal_block(side, delta):
        r0 = (delta - 1) * lb
        first = taps[side][r0:r0 + 1].astype(F32) * scale
        re = seg(xr_ref, side, delta) - sgn * seg(xi_ref, side, delta - 1)
        im = seg(xi_ref, side, delta) + sgn * (seg(xr_ref, side, delta - 1) - first)
        return re, im

    mid = nb - 1
    kre_ref[mid] = seg(xr_ref, 0, 0) + seg(xr_ref, 1, 0)
    kim_ref[mid] = seg(xi_ref, 0, 0) - seg(xi_ref, 1, 0)
    for delta in range(1, nb):
        re, im = causal_block(0, delta)
        kre_ref[mid + delta] = re
        kim_ref[mid + delta] = im
        re, im = causal_block(1, delta)
        kre_ref[mid - delta] = re
        kim_ref[mid - delta] = -im


def _filter_spectra(z_pad, w1_pad, b1, w2, b2, w3, b3, freq, w4, deltas, cs):
    seq = z_pad.shape[0]
    lb = cs.shape[0]
    nb = seq // lb
    hf = w2.shape[0]
    dh = deltas.shape[1]
    cb = MXU_DIM
    nblk = dh // cb
    full = lambda a: pl.BlockSpec(a.shape, lambda c: (0,) * a.ndim)
    return pl.pallas_call(
        _filter_kernel,
        grid=(nblk,),
        in_specs=[
            full(z_pad), full(w1_pad), full(b1), full(w2), full(b2), full(w3), full(b3), full(freq),
            pl.BlockSpec((hf, cb), lambda c: (0, c)),
            pl.BlockSpec((hf, cb), lambda c: (0, nblk + c)),
            pl.BlockSpec((1, cb), lambda c: (0, c)),
            full(cs),
        ],
        out_specs=[pl.BlockSpec((2 * nb - 1, lb, cb), lambda c: (0, 0, c))] * 2,
        out_shape=[jax.ShapeDtypeStruct((2 * nb - 1, lb, dh), F32)] * 2,
        scratch_shapes=[
            pltpu.VMEM((seq, hf), F32),
            pltpu.VMEM((lb, LANES), F32),
            pltpu.VMEM((lb, LANES), F32),
            pltpu.VMEM((lb, LANES), F32),
            pltpu.VMEM((lb, 2 * nb * cb), BF16),
            pltpu.VMEM((lb, 2 * nb * cb), F32),
            pltpu.VMEM((lb, 2 * nb * cb), F32),
        ],
        compiler_params=pltpu.CompilerParams(
            dimension_semantics=("arbitrary",), vmem_limit_bytes=VMEM_LIMIT),
        name="filter_spectra",
    )(z_pad, w1_pad, b1, w2, b2, w3, b3, freq, w4, w4, deltas, cs)


def _conv3(p, w, bias, lo, n):
    n_rows = p.shape[0]
    assert 1 <= lo and lo + n < n_rows
    dn = pltpu.roll(p, 1, 0)
    up = pltpu.roll(p, n_rows - 1, 0)
    out = dn * w[0:1] + p * w[1:2] + up * w[2:3] + bias
    return out[lo:lo + n]


def _inproj_kernel(xp_ref, x_ref, xn_ref, g_ref, w_ref, b_ref, cw_ref, cb_ref,
                   u_ref, gate_ref, p_ref, *, chunks_per_seq):
    i = pl.program_id(0)
    rows = x_ref.shape[0]
    halo = xp_ref.shape[0]
    dh = u_ref.shape[1]
    cb = MXU_DIM
    keep_prev = (i % chunks_per_seq != 0).astype(F32)
    keep_next = (i % chunks_per_seq != chunks_per_seq - 1).astype(F32)

    xw = jnp.concatenate([xp_ref[...], x_ref[...], xn_ref[...]], axis=0)
    h = (_rms_scale(xw) * g_ref[...]).astype(BF16)

    h_main = h[halo:halo + rows]

    def project(rows_h, c0):
        return _dot(rows_h, w_ref[:, c0:c0 + cb]) + b_ref[:, c0:c0 + cb]

    def conv_input(c0):
        p = project(h, c0)
        return jnp.concatenate(
            [p[0:halo] * keep_prev, p[halo:halo + rows], p[halo + rows:] * keep_next], axis=0)

    def conv(c0):
        return _conv3(conv_input(c0), cw_ref[:, c0:c0 + cb], cb_ref[:, c0:c0 + cb], halo, rows)

    for c in range(dh // cb):
        lanes = slice(c * cb, (c + 1) * cb)
        u_ref[:, lanes] = conv(dh + c * cb) * conv(2 * dh + c * cb)
        z = project(h_main, 3 * dh + c * cb)
        gate_ref[:, lanes] = (conv(c * cb) * jax.nn.silu(z)).astype(BF16)

    for c0 in range(0, p_ref.shape[1], cb):
        p_ref[:, c0:c0 + cb] = project(h_main, 4 * dh + c0).astype(BF16)


def _inproj(x2, g_norm, w_in_bf, b_in, conv_w, conv_b, seq):
    n_tok, d = x2.shape
    cols = w_in_bf.shape[1]
    dh = conv_b.shape[1] // 3
    n_plain = cols - 4 * dh
    rows = ROW_CHUNK
    halo = 2 * SUBLANES
    per_halo = rows // halo
    last_halo = n_tok // halo - 1
    full = lambda a: pl.BlockSpec(a.shape, lambda i: (0,) * a.ndim)
    tile = lambda width: pl.BlockSpec((rows, width), lambda i: (i, 0))
    return pl.pallas_call(
        functools.partial(_inproj_kernel, chunks_per_seq=seq // rows),
        grid=(n_tok // rows,),
        in_specs=[
            pl.BlockSpec((halo, d), lambda i: (jnp.maximum(i * per_halo - 1, 0), 0)),
            tile(d),
            pl.BlockSpec((halo, d), lambda i: (jnp.minimum((i + 1) * per_halo, last_halo), 0)),
            full(g_norm),
            pl.BlockSpec(w_in_bf.shape, lambda i: (0, 0), pipeline_mode=pl.Buffered(1)),
            full(b_in), full(conv_w), full(conv_b),
        ],
        out_specs=[tile(dh), tile(dh), tile(n_plain)],
        out_shape=[
            jax.ShapeDtypeStruct((n_tok, dh), F32),
            jax.ShapeDtypeStruct((n_tok, dh), BF16),
            jax.ShapeDtypeStruct((n_tok, n_plain), BF16),
        ],
        compiler_params=pltpu.CompilerParams(
            dimension_semantics=("parallel",), vmem_limit_bytes=VMEM_LIMIT),
        name="inproj",
    )(x2, x2, x2, g_norm, w_in_bf, b_in, conv_w, conv_b)


def _hyena_kernel(u_ref, gate_ref, d_ref, kre_ref, kim_ref, cs_ref, y_ref, ub_ref, av_ref, pq_ref):
    seq = u_ref.shape[0]
    lb = cs_ref.shape[0]
    nb = seq // lb
    mid = nb - 1
    blocks = [slice(b * lb, (b + 1) * lb) for b in range(nb)]

    for rows in blocks:
        ub_ref[rows, :] = u_ref[rows, :].astype(BF16)

    for b, rows in enumerate(blocks):
        ub = ub_ref[rows, :]
        av_ref[2 * b] = _dot(cs_ref[:, 0:lb], ub)
        av_ref[2 * b + 1] = _dot(cs_ref[:, lb:2 * lb], ub)

    def spectral(i, carry):
        rows = _chunk(i, SPEC_ROWS)
        a_blk = [av_ref[2 * b, rows, :] for b in range(nb)]
        v_blk = [av_ref[2 * b + 1, rows, :] for b in range(nb)]
        for a in range(nb):
            p = q = None
            for b in range(nb):
                kr = kre_ref[mid + a - b, rows, :]
                ki = kim_ref[mid + a - b, rows, :]
                tp = kr * a_blk[b] + ki * v_blk[b]
                tq = kr * v_blk[b] - ki * a_blk[b]
                p = tp if p is None else p + tp
                q = tq if q is None else q + tq
            pq_ref[a, rows, :] = p.astype(BF16)
            pq_ref[a, pl.ds(pl.multiple_of(lb + i * SPEC_ROWS, SPEC_ROWS), SPEC_ROWS), :] = (
                q.astype(BF16))
        return carry

    lax.fori_loop(0, lb // SPEC_ROWS, spectral, 0)

    for a, rows in enumerate(blocks):
        y = _dot(cs_ref[...], pq_ref[a])
        y = y + u_ref[rows, :] * d_ref[...]
        y_ref[rows, :] = (y * gate_ref[rows, :].astype(F32)).astype(BF16)


def _hyena(u, gate, hyena_d, kre, kim, cs, batch, seq):
    dh = hyena_d.shape[1]
    cw = HYENA_COLS
    lb = cs.shape[0]
    nb = seq // lb
    act = pl.BlockSpec((seq, cw), lambda c, b: (b, c))
    spectra = pl.BlockSpec((2 * nb - 1, lb, cw), lambda c, b: (0, 0, c),
                           pipeline_mode=pl.Buffered(1))
    return pl.pallas_call(
        _hyena_kernel,
        grid=(dh // cw, batch),
        in_specs=[
            act, act,
            pl.BlockSpec((1, cw), lambda c, b: (0, c)),
            spectra, spectra,
            pl.BlockSpec(cs.shape, lambda c, b: (0, 0)),
        ],
        out_specs=act,
        out_shape=jax.ShapeDtypeStruct((batch * seq, dh), BF16),
        scratch_shapes=[
            pltpu.VMEM((seq, cw), BF16),
            pltpu.VMEM((2 * nb, lb, cw), F32),
            pltpu.VMEM((nb, 2 * lb, cw), BF16),
        ],
        compiler_params=pltpu.CompilerParams(
            dimension_semantics=("arbitrary", "arbitrary"), vmem_limit_bytes=VMEM_LIMIT),
        name="hyena",
    )(u, gate, hyena_d, kre, kim, cs)


def _pool_kernel(u_ref, z_ref, w_ref, b_ref, s_ref, y_ref):
    seq = u_ref.shape[0]
    gw = w_ref.shape[1]
    sl = SUBLANES
    pad = 2 * sl
    ext = seq + pad
    edge = lax.broadcasted_iota(jnp.int32, (sl, gw), 0)
    assert max(POOL_WINDOWS) // 2 <= sl
    for g, win in enumerate(POOL_WINDOWS):
        cols = slice(g * gw, (g + 1) * gw)
        half = win // 2
        u = u_ref[:, cols].astype(F32)
        s = jnp.concatenate([u, jnp.zeros((pad, gw), F32)], axis=0)
        w = 1
        while w < win:
            s = s + pltpu.roll(s, w, 0)
            w *= 2
        if half > 1:
            s = pltpu.roll(s, ext - (half - 1), 0)
        total = s[0:seq]
        cnt_first = jnp.minimum(edge + half, win).astype(F32)
        cnt_last = jnp.minimum(sl - edge + half, win).astype(F32)
        mean = jnp.concatenate([total[0:sl] / cnt_first,
                                total[sl:seq - sl] * (1.0 / win),
                                total[seq - sl:] / cnt_last], axis=0)
        pooled = (mean - u).astype(BF16)
        y = _dot(pooled, w_ref[g]) + b_ref[:, cols]
        y = y * s_ref[:, cols] * jax.nn.silu(z_ref[:, cols].astype(F32))
        y_ref[:, cols] = y.astype(BF16)


def _pool(plain, pool_w_bf, pool_b, pool_scale, batch, seq):
    dp = pool_b.shape[1]
    full = lambda a: pl.BlockSpec(a.shape, lambda b: (0,) * a.ndim)
    return pl.pallas_call(
        _pool_kernel,
        grid=(batch,),
        in_specs=[
            pl.BlockSpec((seq, dp), lambda b: (b, 0)),
            pl.BlockSpec((seq, dp), lambda b: (b, 1)),
            full(pool_w_bf), full(pool_b), full(pool_scale),
        ],
        out_specs=pl.BlockSpec((seq, dp), lambda b: (b, 0)),
        out_shape=jax.ShapeDtypeStruct((batch * seq, dp), BF16),
        compiler_params=pltpu.CompilerParams(
            dimension_semantics=("parallel",), vmem_limit_bytes=VMEM_LIMIT),
        name="pool",
    )(plain, plain, pool_w_bf, pool_b, pool_scale)


def _merge_kernel(yh_ref, yp_ref, g0_ref, g1_ref, x_ref, wh_ref, wp_ref, wo_ref, gf_ref, o_ref):
    out_h = _dot(yh_ref[...], wh_ref[...])
    out_p = _dot(yp_ref[...], wp_ref[...])
    merged = (jax.nn.sigmoid(g0_ref[...].astype(F32)) * out_h
              + jax.nn.sigmoid(g1_ref[...].astype(F32)) * out_p)
    res = x_ref[...] + _dot(merged.astype(BF16), wo_ref[...])
    o_ref[...] = _rms_scale(res) * gf_ref[...]


def _merge(yh, yp, plain, x2, wh_bf, wp_bf, wo_bf, g_final, tm=1024):
    n_tok, d = x2.shape
    tile = lambda col: pl.BlockSpec((tm, d), lambda i: (i, col))
    resident = lambda a: pl.BlockSpec(a.shape, lambda i: (0, 0), pipeline_mode=pl.Buffered(1))
    return pl.pallas_call(
        _merge_kernel,
        grid=(n_tok // tm,),
        in_specs=[
            tile(0), tile(0), tile(2), tile(3), tile(0),
            resident(wh_bf), resident(wp_bf), resident(wo_bf),
            pl.BlockSpec((1, d), lambda i: (0, 0)),
        ],
        out_specs=tile(0),
        out_shape=jax.ShapeDtypeStruct((n_tok, d), F32),
        compiler_params=pltpu.CompilerParams(
            dimension_semantics=("parallel",), vmem_limit_bytes=VMEM_LIMIT),
        name="merge",
    )(yh, yp, plain, plain, x2, wh_bf, wp_bf, wo_bf, g_final)


def _filter_features(seq):
    t = jnp.linspace(0.0, 1.0, seq, dtype=F32)[:, None]
    bands = (FILTER_EMB - 1) // 2
    w = 2.0 * math.pi * jnp.arange(seq, dtype=F32) / seq
    f = jnp.linspace(1e-4, bands - 1, bands, dtype=F32)
    ang = w[:, None] * f[None, :]
    return jnp.concatenate([t, jnp.cos(ang), -jnp.sin(ang)], axis=-1)


def kernel(x, g_norm, w_in, b_in, conv_w, conv_b, filt_w1, filt_b1, filt_w2, filt_b2, filt_w3, filt_b3, filt_w4, filt_freq, hyena_d, w_hyena_out, pool_w, pool_b, pool_scale, w_pool_out, w_out, g_final):
    batch, seq, d = x.shape
    dh = hyena_d.shape[1]
    dp = pool_b.shape[1]
    assert g_norm.shape[0] == 1, "single-layer block"
    assert d == dh == dp and conv_b.shape[1] == 3 * dh and w_in.shape[2] == 4 * dh + 2 * dp + 2 * d
    assert dh % HYENA_COLS == 0 and seq % ROW_CHUNK == 0
    assert seq % CONV_BLOCKS == 0 and (seq // CONV_BLOCKS) % SPEC_ROWS == 0

    cs = _transform_matrices(seq // CONV_BLOCKS)
    z_pad = jnp.pad(_filter_features(seq), ((0, 0), (0, LANES - FILTER_EMB)))
    w1_pad = jnp.pad(filt_w1[0], ((0, LANES - FILTER_EMB), (0, 0)))
    deltas = jnp.abs(jnp.linspace(MIN_DECAY, MAX_DECAY, dh, dtype=F32))[None, :]
    kre, kim = _filter_spectra(z_pad, w1_pad, filt_b1, filt_w2[0], filt_b2, filt_w3[0], filt_b3,
                               filt_freq, filt_w4[0], deltas, cs)

    x2 = x.reshape(batch * seq, d)
    u, gate, plain = _inproj(x2, g_norm, w_in[0].astype(BF16), b_in, conv_w[0], conv_b, seq)
    yh = _hyena(u, gate, hyena_d, kre, kim, cs, batch, seq)
    yp = _pool(plain, pool_w[0].astype(BF16), pool_b, pool_scale, batch, seq)
    out = _merge(yh, yp, plain, x2, w_hyena_out[0].astype(BF16), w_pool_out[0].astype(BF16),
                 w_out[0].astype(BF16), g_final[None, :])
    return out.reshape(batch, seq, d)
```

```python
import functools
import math

import jax
import jax.numpy as jnp
from jax import lax
from jax.experimental import pallas as pl
from jax.experimental.pallas import tpu as pltpu

FILTER_EMB = 33
FAST_DECAY_PCT = 0.3
SLOW_DECAY_PCT = 1.5
DECAY_TARGET = 1e-2
MAX_DECAY = math.log(DECAY_TARGET) / FAST_DECAY_PCT
MIN_DECAY = math.log(DECAY_TARGET) / SLOW_DECAY_PCT
POOL_WINDOWS = (2, 4, 8, 16)
NORM_EPS = 1e-6

LANES = 128
SUBLANES = 8
MXU_DIM = 256
HYENA_COLS = 2 * MXU_DIM
ROW_CHUNK = 512
CONV_BLOCKS = 4
SPEC_ROWS = 64
VMEM_LIMIT = 56 * 1024 * 1024

F32 = jnp.float32
BF16 = jnp.bfloat16


def _dot(a, b):
    return jnp.dot(a, b, preferred_element_type=F32)


def _dot_hi(a, b):
    return jnp.dot(a, b, preferred_element_type=F32, precision=lax.Precision.HIGHEST)


def _rms_scale(x):
    return x * lax.rsqrt(jnp.mean(x * x, axis=-1, keepdims=True) + NORM_EPS)


def _chunk(i, size):
    return pl.ds(pl.multiple_of(i * size, size), size)


def _transform_matrices(lb):
    period = 8 * lb
    odd = 2 * jnp.arange(lb, dtype=jnp.int32) + 1
    ang = ((odd[:, None] * odd[None, :]) % period).astype(F32) * (2.0 * math.pi / period)
    return jnp.concatenate([jnp.cos(ang), jnp.sin(ang)], axis=1).astype(BF16)


def _filter_kernel(z_ref, w1_ref, b1_ref, w2_ref, b2_ref, w3_ref, b3_ref, fr_ref,
                   w4f_ref, w4b_ref, dl_ref, cs_ref, kre_ref, kim_ref,
                   h_ref, cphi_ref, sphi_ref, sgn_ref, kk_ref, xr_ref, xi_ref):
    seq = z_ref.shape[0]
    lb = cs_ref.shape[0]
    nb = seq // lb
    cb = dl_ref.shape[1]
    scale = 1.0 / lb

    @pl.when(pl.program_id(0) == 0)
    def _():
        fr = fr_ref[...]
        h = jnp.sin(fr * (_dot_hi(z_ref[...], w1_ref[...]) + b1_ref[...]))
        h = jnp.sin(fr * (_dot_hi(h, w2_ref[...]) + b2_ref[...]))
        h_ref[...] = jnp.sin(fr * (_dot_hi(h, w3_ref[...]) + b3_ref[...]))
        f_idx = lax.broadcasted_iota(jnp.int32, cphi_ref.shape, 0)
        phi = (f_idx.astype(F32) + 0.5) * (math.pi / (2 * lb))
        cphi_ref[...] = jnp.cos(phi) * scale
        sphi_ref[...] = jnp.sin(phi) * scale
        sgn_ref[...] = (1 - 2 * (f_idx & 1)).astype(F32)

    h = h_ref[...]
    decay = jnp.exp(-z_ref[:, 0:1] * dl_ref[...])
    taps = (_dot_hi(h, w4f_ref[...]) * decay, _dot_hi(h, w4b_ref[...]) * decay)
    taps = tuple(k.astype(BF16) for k in taps)
    for side, k in enumerate(taps):
        for s_ in range(nb):
            c0 = (side * nb + s_) * cb
            kk_ref[:, c0:c0 + cb] = k[s_ * lb:(s_ + 1) * lb]

    group = 2 * cb
    tile = lambda ref, width: jnp.concatenate([ref[...]] * (width // LANES), axis=1)
    c_phi, s_phi = tile(cphi_ref, group), tile(sphi_ref, group)
    for g in range(2 * nb * cb // group):
        cols = slice(g * group, (g + 1) * group)
        kk = kk_ref[:, cols]
        a = _dot(cs_ref[:, 0:lb], kk)
        b = _dot(cs_ref[:, lb:2 * lb], kk)
        xr_ref[:, cols] = c_phi * a + s_phi * b
        xi_ref[:, cols] = s_phi * a - c_phi * b

    sgn = tile(sgn_ref, cb)

    def seg(ref, side, s_):
        c0 = (side * nb + s_) * cb
        return ref[:, c0:c0 + cb]

    def causal_block(side, delta):
        r0 = (delta - 1) * lb
        first = taps[side][r0:r0 + 1].astype(F32) * scale
        re = seg(xr_ref, side, delta) - sgn * seg(xi_ref, side, delta - 1)
        im = seg(xi_ref, side, delta) + sgn * (seg(xr_ref, side, delta - 1) - first)
        return re, im

    mid = nb - 1
    kre_ref[mid] = seg(xr_ref, 0, 0) + seg(xr_ref, 1, 0)
    kim_ref[mid] = seg(xi_ref, 0, 0) - seg(xi_ref, 1, 0)
    for delta in range(1, nb):
        re, im = causal_block(0, delta)
        kre_ref[mid + delta] = re
        kim_ref[mid + delta] = im
        re, im = causal_block(1, delta)
        kre_ref[mid - delta] = re
        kim_ref[mid - delta] = -im


def _filter_spectra(z_pad, w1_pad, b1, w2, b2, w3, b3, freq, w4, deltas, cs):
    seq = z_pad.shape[0]
    lb = cs.shape[0]
    nb = seq // lb
    hf = w2.shape[0]
    dh = deltas.shape[1]
    cb = MXU_DIM
    nblk = dh // cb
    full = lambda a: pl.BlockSpec(a.shape, lambda c: (0,) * a.ndim)
    return pl.pallas_call(
        _filter_kernel,
        grid=(nblk,),
        in_specs=[
            full(z_pad), full(w1_pad), full(b1), full(w2), full(b2), full(w3), full(b3), full(freq),
            pl.BlockSpec((hf, cb), lambda c: (0, c)),
            pl.BlockSpec((hf, cb), lambda c: (0, nblk + c)),
            pl.BlockSpec((1, cb), lambda c: (0, c)),
            full(cs),
        ],
        out_specs=[pl.BlockSpec((2 * nb - 1, lb, cb), lambda c: (0, 0, c))] * 2,
        out_shape=[jax.ShapeDtypeStruct((2 * nb - 1, lb, dh), F32)] * 2,
        scratch_shapes=[
            pltpu.VMEM((seq, hf), F32),
            pltpu.VMEM((lb, LANES), F32),
            pltpu.VMEM((lb, LANES), F32),
            pltpu.VMEM((lb, LANES), F32),
            pltpu.VMEM((lb, 2 * nb * cb), BF16),
            pltpu.VMEM((lb, 2 * nb * cb), F32),
            pltpu.VMEM((lb, 2 * nb * cb), F32),
        ],
        compiler_params=pltpu.CompilerParams(
            dimension_semantics=("arbitrary",), vmem_limit_bytes=VMEM_LIMIT),
        name="filter_spectra",
    )(z_pad, w1_pad, b1, w2, b2, w3, b3, freq, w4, w4, deltas, cs)


def _conv3(p, w, bias, lo, n):
    n_rows = p.shape[0]
    assert 1 <= lo and lo + n < n_rows
    dn = pltpu.roll(p, 1, 0)
    up = pltpu.roll(p, n_rows - 1, 0)
    out = dn * w[0:1] + p * w[1:2] + up * w[2:3] + bias
    return out[lo:lo + n]


def _window_mean_minus_self(p, win, lo, n, at_start, at_end):
    n_rows = p.shape[0]
    half = win // 2
    sl = SUBLANES
    assert lo >= win and lo + n + win <= n_rows and half <= sl
    s = p
    w = 1
    while w < win:
        s = s + pltpu.roll(s, w, 0)
        w *= 2
    if half > 1:
        s = pltpu.roll(s, n_rows - (half - 1), 0)
    total = s[lo:lo + n]
    edge = lax.broadcasted_iota(jnp.int32, (sl, p.shape[1]), 0)
    cnt_first = jnp.where(at_start > 0, jnp.minimum(edge + half, win), win).astype(F32)
    cnt_last = jnp.where(at_end > 0, jnp.minimum(sl - edge + half, win), win).astype(F32)
    mean = jnp.concatenate([total[0:sl] / cnt_first,
                            total[sl:n - sl] * (1.0 / win),
                            total[n - sl:] / cnt_last], axis=0)
    return mean - p[lo:lo + n]


def _inproj_kernel(xp_ref, x_ref, xn_ref, g_ref, w_ref, b_ref, cw_ref, cb_ref,
                   pw_ref, pb_ref, ps_ref, u_ref, gate_ref, yp_ref, p_ref, *, chunks_per_seq):
    i = pl.program_id(0)
    rows = x_ref.shape[0]
    halo = xp_ref.shape[0]
    dh = u_ref.shape[1]
    cb = MXU_DIM
    keep_prev = (i % chunks_per_seq != 0).astype(F32)
    keep_next = (i % chunks_per_seq != chunks_per_seq - 1).astype(F32)

    xw = jnp.concatenate([xp_ref[...], x_ref[...], xn_ref[...]], axis=0)
    h = (_rms_scale(xw) * g_ref[...]).astype(BF16)

    h_main = h[halo:halo + rows]

    def project(rows_h, c0):
        return _dot(rows_h, w_ref[:, c0:c0 + cb]) + b_ref[:, c0:c0 + cb]

    def conv_input(c0):
        p = project(h, c0)
        return jnp.concatenate(
            [p[0:halo] * keep_prev, p[halo:halo + rows], p[halo + rows:] * keep_next], axis=0)

    def conv(c0):
        return _conv3(conv_input(c0), cw_ref[:, c0:c0 + cb], cb_ref[:, c0:c0 + cb], halo, rows)

    dp = yp_ref.shape[1]
    assert dp == len(POOL_WINDOWS) * cb
    for g, win in enumerate(POOL_WINDOWS):
        lanes = slice(g * cb, (g + 1) * cb)
        pooled = _window_mean_minus_self(conv_input(4 * dh + g * cb), win, halo, rows,
                                         1.0 - keep_prev, 1.0 - keep_next)
        y = _dot(pooled.astype(BF16), pw_ref[g]) + pb_ref[:, lanes]
        z = project(h_main, 4 * dh + dp + g * cb)
        yp_ref[:, lanes] = (y * ps_ref[:, lanes] * jax.nn.silu(z)).astype(BF16)

    for c in range(dh // cb):
        lanes = slice(c * cb, (c + 1) * cb)
        u_ref[:, lanes] = conv(dh + c * cb) * conv(2 * dh + c * cb)
        z = project(h_main, 3 * dh + c * cb)
        gate_ref[:, lanes] = (conv(c * cb) * jax.nn.silu(z)).astype(BF16)

    for c0 in range(0, p_ref.shape[1], cb):
        p_ref[:, c0:c0 + cb] = project(h_main, 4 * dh + 2 * dp + c0).astype(BF16)


def _inproj(x2, g_norm, w_in_bf, b_in, conv_w, conv_b, pool_w_bf, pool_b, pool_scale, seq):
    n_tok, d = x2.shape
    cols = w_in_bf.shape[1]
    dh = conv_b.shape[1] // 3
    dp = pool_b.shape[1]
    n_gate = cols - 4 * dh - 2 * dp
    rows = ROW_CHUNK
    halo = 2 * SUBLANES
    per_halo = rows // halo
    last_halo = n_tok // halo - 1
    full = lambda a: pl.BlockSpec(a.shape, lambda i: (0,) * a.ndim)
    tile = lambda width: pl.BlockSpec((rows, width), lambda i: (i, 0))
    return pl.pallas_call(
        functools.partial(_inproj_kernel, chunks_per_seq=seq // rows),
        grid=(n_tok // rows,),
        in_specs=[
            pl.BlockSpec((halo, d), lambda i: (jnp.maximum(i * per_halo - 1, 0), 0)),
            tile(d),
            pl.BlockSpec((halo, d), lambda i: (jnp.minimum((i + 1) * per_halo, last_halo), 0)),
            full(g_norm),
            pl.BlockSpec(w_in_bf.shape, lambda i: (0, 0), pipeline_mode=pl.Buffered(1)),
            full(b_in), full(conv_w), full(conv_b),
            full(pool_w_bf), full(pool_b), full(pool_scale),
        ],
        out_specs=[tile(dh), tile(dh), tile(dp), tile(n_gate)],
        out_shape=[
            jax.ShapeDtypeStruct((n_tok, dh), F32),
            jax.ShapeDtypeStruct((n_tok, dh), BF16),
            jax.ShapeDtypeStruct((n_tok, dp), BF16),
            jax.ShapeDtypeStruct((n_tok, n_gate), BF16),
        ],
        compiler_params=pltpu.CompilerParams(
            dimension_semantics=("parallel",), vmem_limit_bytes=VMEM_LIMIT),
        name="inproj",
    )(x2, x2, x2, g_norm, w_in_bf, b_in, conv_w, conv_b, pool_w_bf, pool_b, pool_scale)


def _hyena_kernel(u_ref, gate_ref, d_ref, kre_ref, kim_ref, cs_ref, y_ref, ub_ref, av_ref, pq_ref):
    seq = u_ref.shape[0]
    lb = cs_ref.shape[0]
    nb = seq // lb
    mid = nb - 1
    blocks = [slice(b * lb, (b + 1) * lb) for b in range(nb)]

    for rows in blocks:
        ub_ref[rows, :] = u_ref[rows, :].astype(BF16)

    for b, rows in enumerate(blocks):
        ub = ub_ref[rows, :]
        av_ref[2 * b] = _dot(cs_ref[:, 0:lb], ub)
        av_ref[2 * b + 1] = _dot(cs_ref[:, lb:2 * lb], ub)

    def spectral(i, carry):
        rows = _chunk(i, SPEC_ROWS)
        a_blk = [av_ref[2 * b, rows, :] for b in range(nb)]
        v_blk = [av_ref[2 * b + 1, rows, :] for b in range(nb)]
        for a in range(nb):
            p = q = None
            for b in range(nb):
                kr = kre_ref[mid + a - b, rows, :]
                ki = kim_ref[mid + a - b, rows, :]
                tp = kr * a_blk[b] + ki * v_blk[b]
                tq = kr * v_blk[b] - ki * a_blk[b]
                p = tp if p is None else p + tp
                q = tq if q is None else q + tq
            pq_ref[a, rows, :] = p.astype(BF16)
            pq_ref[a, pl.ds(pl.multiple_of(lb + i * SPEC_ROWS, SPEC_ROWS), SPEC_ROWS), :] = (
                q.astype(BF16))
        return carry

    lax.fori_loop(0, lb // SPEC_ROWS, spectral, 0)

    for a, rows in enumerate(blocks):
        y = _dot(cs_ref[...], pq_ref[a])
        y = y + u_ref[rows, :] * d_ref[...]
        y_ref[rows, :] = (y * gate_ref[rows, :].astype(F32)).astype(BF16)


def _hyena(u, gate, hyena_d, kre, kim, cs, batch, seq):
    dh = hyena_d.shape[1]
    cw = HYENA_COLS
    lb = cs.shape[0]
    nb = seq // lb
    act = pl.BlockSpec((seq, cw), lambda c, b: (b, c))
    spectra = pl.BlockSpec((2 * nb - 1, lb, cw), lambda c, b: (0, 0, c),
                           pipeline_mode=pl.Buffered(1))
    return pl.pallas_call(
        _hyena_kernel,
        grid=(dh // cw, batch),
        in_specs=[
            act, act,
            pl.BlockSpec((1, cw), lambda c, b: (0, c)),
            spectra, spectra,
            pl.BlockSpec(cs.shape, lambda c, b: (0, 0)),
        ],
        out_specs=act,
        out_shape=jax.ShapeDtypeStruct((batch * seq, dh), BF16),
        scratch_shapes=[
            pltpu.VMEM((seq, cw), BF16),
            pltpu.VMEM((2 * nb, lb, cw), F32),
            pltpu.VMEM((nb, 2 * lb, cw), BF16),
        ],
        compiler_params=pltpu.CompilerParams(
            dimension_semantics=("arbitrary", "arbitrary"), vmem_limit_bytes=VMEM_LIMIT),
        name="hyena",
    )(u, gate, hyena_d, kre, kim, cs)


def _merge_kernel(yh_ref, yp_ref, g0_ref, g1_ref, x_ref, wh_ref, wp_ref, wo_ref, gf_ref, o_ref):
    out_h = _dot(yh_ref[...], wh_ref[...])
    out_p = _dot(yp_ref[...], wp_ref[...])
    merged = (jax.nn.sigmoid(g0_ref[...].astype(F32)) * out_h
              + jax.nn.sigmoid(g1_ref[...].astype(F32)) * out_p)
    res = x_ref[...] + _dot(merged.astype(BF16), wo_ref[...])
    o_ref[...] = _rms_scale(res) * gf_ref[...]


def _merge(yh, yp, gates, x2, wh_bf, wp_bf, wo_bf, g_final, tm=1024):
    n_tok, d = x2.shape
    tile = lambda col: pl.BlockSpec((tm, d), lambda i: (i, col))
    resident = lambda a: pl.BlockSpec(a.shape, lambda i: (0, 0), pipeline_mode=pl.Buffered(1))
    return pl.pallas_call(
        _merge_kernel,
        grid=(n_tok // tm,),
        in_specs=[
            tile(0), tile(0), tile(0), tile(1), tile(0),
            resident(wh_bf), resident(wp_bf), resident(wo_bf),
            pl.BlockSpec((1, d), lambda i: (0, 0)),
        ],
        out_specs=tile(0),
        out_shape=jax.ShapeDtypeStruct((n_tok, d), F32),
        compiler_params=pltpu.CompilerParams(
            dimension_semantics=("parallel",), vmem_limit_bytes=VMEM_LIMIT),
        name="merge",
    )(yh, yp, gates, gates, x2, wh_bf, wp_bf, wo_bf, g_final)


def _filter_features(seq):
    t = jnp.linspace(0.0, 1.0, seq, dtype=F32)[:, None]
    bands = (FILTER_EMB - 1) // 2
    w = 2.0 * math.pi * jnp.arange(seq, dtype=F32) / seq
    f = jnp.linspace(1e-4, bands - 1, bands, dtype=F32)
    ang = w[:, None] * f[None, :]
    return jnp.concatenate([t, jnp.cos(ang), -jnp.sin(ang)], axis=-1)


def kernel(x, g_norm, w_in, b_in, conv_w, conv_b, filt_w1, filt_b1, filt_w2, filt_b2, filt_w3, filt_b3, filt_w4, filt_freq, hyena_d, w_hyena_out, pool_w, pool_b, pool_scale, w_pool_out, w_out, g_final):
    batch, seq, d = x.shape
    dh = hyena_d.shape[1]
    dp = pool_b.shape[1]
    assert g_norm.shape[0] == 1, "single-layer block"
    assert d == dh == dp and conv_b.shape[1] == 3 * dh and w_in.shape[2] == 4 * dh + 2 * dp + 2 * d
    assert dh % HYENA_COLS == 0 and seq % ROW_CHUNK == 0
    assert seq % CONV_BLOCKS == 0 and (seq // CONV_BLOCKS) % SPEC_ROWS == 0

    cs = _transform_matrices(seq // CONV_BLOCKS)
    z_pad = jnp.pad(_filter_features(seq), ((0, 0), (0, LANES - FILTER_EMB)))
    w1_pad = jnp.pad(filt_w1[0], ((0, LANES - FILTER_EMB), (0, 0)))
    deltas = jnp.abs(jnp.linspace(MIN_DECAY, MAX_DECAY, dh, dtype=F32))[None, :]
    kre, kim = _filter_spectra(z_pad, w1_pad, filt_b1, filt_w2[0], filt_b2, filt_w3[0], filt_b3,
                               filt_freq, filt_w4[0], deltas, cs)

    x2 = x.reshape(batch * seq, d)
    u, gate, yp, gates = _inproj(x2, g_norm, w_in[0].astype(BF16), b_in, conv_w[0], conv_b,
                                 pool_w[0].astype(BF16), pool_b, pool_scale, seq)
    yh = _hyena(u, gate, hyena_d, kre, kim, cs, batch, seq)
    out = _merge(yh, yp, gates, x2, w_hyena_out[0].astype(BF16), w_pool_out[0].astype(BF16),
                 w_out[0].astype(BF16), g_final[None, :])
    return out.reshape(batch, seq, d)
```

```python
import functools
import math

import jax
import jax.numpy as jnp
from jax import lax
from jax.experimental import pallas as pl
from jax.experimental.pallas import tpu as pltpu

FILTER_EMB = 33
FAST_DECAY_PCT = 0.3
SLOW_DECAY_PCT = 1.5
DECAY_TARGET = 1e-2
MAX_DECAY = math.log(DECAY_TARGET) / FAST_DECAY_PCT
MIN_DECAY = math.log(DECAY_TARGET) / SLOW_DECAY_PCT
POOL_WINDOWS = (2, 4, 8, 16)
NORM_EPS = 1e-6

LANES = 128
SUBLANES = 8
MXU_DIM = 256
HYENA_COLS = MXU_DIM
ROW_CHUNK = 512
CONV_BLOCKS = 4
SPEC_ROWS = 64
VMEM_LIMIT = 56 * 1024 * 1024

F32 = jnp.float32
BF16 = jnp.bfloat16


def _dot(a, b):
    return jnp.dot(a, b, preferred_element_type=F32)


def _dot_hi(a, b):
    return jnp.dot(a, b, preferred_element_type=F32, precision=lax.Precision.HIGHEST)


def _rms_scale(x):
    return x * lax.rsqrt(jnp.mean(x * x, axis=-1, keepdims=True) + NORM_EPS)


def _transform_matrices(lb):
    period = 8 * lb
    odd = 2 * jnp.arange(lb, dtype=jnp.int32) + 1
    ang = ((odd[:, None] * odd[None, :]) % period).astype(F32) * (2.0 * math.pi / period)
    return jnp.concatenate([jnp.cos(ang), jnp.sin(ang)], axis=1).astype(BF16)


def _filter_kernel(z_ref, w1_ref, b1_ref, w2_ref, b2_ref, w3_ref, b3_ref, fr_ref,
                   w4f_ref, w4b_ref, dl_ref, cs_ref, kre_ref, kim_ref,
                   h_ref, cphi_ref, sphi_ref, sgn_ref, kk_ref, xr_ref, xi_ref):
    seq = z_ref.shape[0]
    lb = cs_ref.shape[0]
    nb = seq // lb
    cb = dl_ref.shape[1]
    scale = 1.0 / lb

    @pl.when(pl.program_id(0) == 0)
    def _():
        fr = fr_ref[...]
        h = jnp.sin(fr * (_dot_hi(z_ref[...], w1_ref[...]) + b1_ref[...]))
        h = jnp.sin(fr * (_dot_hi(h, w2_ref[...]) + b2_ref[...]))
        h_ref[...] = jnp.sin(fr * (_dot_hi(h, w3_ref[...]) + b3_ref[...]))
        f_idx = lax.broadcasted_iota(jnp.int32, cphi_ref.shape, 0)
        phi = (f_idx.astype(F32) + 0.5) * (math.pi / (2 * lb))
        cphi_ref[...] = jnp.cos(phi) * scale
        sphi_ref[...] = jnp.sin(phi) * scale
        sgn_ref[...] = (1 - 2 * (f_idx & 1)).astype(F32)

    h = h_ref[...]
    decay = jnp.exp(-z_ref[:, 0:1] * dl_ref[...])
    taps = (_dot_hi(h, w4f_ref[...]) * decay, _dot_hi(h, w4b_ref[...]) * decay)
    taps = tuple(k.astype(BF16) for k in taps)
    for side, k in enumerate(taps):
        for s_ in range(nb):
            c0 = (side * nb + s_) * cb
            kk_ref[:, c0:c0 + cb] = k[s_ * lb:(s_ + 1) * lb]

    group = 2 * cb
    tile = lambda ref, width: jnp.concatenate([ref[...]] * (width // LANES), axis=1)
    c_phi, s_phi = tile(cphi_ref, group), tile(sphi_ref, group)
    for g in range(2 * nb * cb // group):
        cols = slice(g * group, (g + 1) * group)
        kk = kk_ref[:, cols]
        a = _dot(cs_ref[:, 0:lb], kk)
        b = _dot(cs_ref[:, lb:2 * lb], kk)
        xr_ref[:, cols] = c_phi * a + s_phi * b
        xi_ref[:, cols] = s_phi * a - c_phi * b

    sgn = tile(sgn_ref, cb)

    def seg(ref, side, s_):
        c0 = (side * nb + s_) * cb
        return ref[:, c0:c0 + cb]

    def causal_block(side, delta):
        r0 = (delta - 1) * lb
        first = taps[side][r0:r0 + 1].astype(F32) * scale
        re = seg(xr_ref, side, delta) - sgn * seg(xi_ref, side, delta - 1)
        im = seg(xi_ref, side, delta) + sgn * (seg(xr_ref, side, delta - 1) - first)
        return re, im

    mid = nb - 1
    kre_ref[mid] = seg(xr_ref, 0, 0) + seg(xr_ref, 1, 0)
    kim_ref[mid] = seg(xi_ref, 0, 0) - seg(xi_ref, 1, 0)
    for delta in range(1, nb):
        re, im = causal_block(0, delta)
        kre_ref[mid + delta] = re
        kim_ref[mid + delta] = im
        re, im = causal_block(1, delta)
        kre_ref[mid - delta] = re
        kim_ref[mid - delta] = -im


def _filter_spectra(z_pad, w1_pad, b1, w2, b2, w3, b3, freq, w4, deltas, cs):
    seq = z_pad.shape[0]
    lb = cs.shape[0]
    nb = seq // lb
    hf = w2.shape[0]
    dh = deltas.shape[1]
    cb = MXU_DIM
    nblk = dh // cb
    full = lambda a: pl.BlockSpec(a.shape, lambda c: (0,) * a.ndim)
    return pl.pallas_call(
        _filter_kernel,
        grid=(nblk,),
        in_specs=[
            full(z_pad), full(w1_pad), full(b1), full(w2), full(b2), full(w3), full(b3), full(freq),
            pl.BlockSpec((hf, cb), lambda c: (0, c)),
            pl.BlockSpec((hf, cb), lambda c: (0, nblk + c)),
            pl.BlockSpec((1, cb), lambda c: (0, c)),
            full(cs),
        ],
        out_specs=[pl.BlockSpec((2 * nb - 1, lb, cb), lambda c: (0, 0, c))] * 2,
        out_shape=[jax.ShapeDtypeStruct((2 * nb - 1, lb, dh), F32)] * 2,
        scratch_shapes=[
            pltpu.VMEM((seq, hf), F32),
            pltpu.VMEM((lb, LANES), F32),
            pltpu.VMEM((lb, LANES), F32),
            pltpu.VMEM((lb, LANES), F32),
            pltpu.VMEM((lb, 2 * nb * cb), BF16),
            pltpu.VMEM((lb, 2 * nb * cb), F32),
            pltpu.VMEM((lb, 2 * nb * cb), F32),
        ],
        compiler_params=pltpu.CompilerParams(
            dimension_semantics=("arbitrary",), vmem_limit_bytes=VMEM_LIMIT),
        name="filter_spectra",
    )(z_pad, w1_pad, b1, w2, b2, w3, b3, freq, w4, w4, deltas, cs)


def _conv3(p, w, bias, lo, n):
    n_rows = p.shape[0]
    assert 1 <= lo and lo + n < n_rows
    dn = pltpu.roll(p, 1, 0)
    up = pltpu.roll(p, n_rows - 1, 0)
    out = dn * w[0:1] + p * w[1:2] + up * w[2:3] + bias
    return out[lo:lo + n]


def _window_mean_minus_self(p, win, lo, n, at_start, at_end):
    n_rows = p.shape[0]
    half = win // 2
    sl = SUBLANES
    assert lo >= win and lo + n + win <= n_rows and half <= sl
    s = p
    w = 1
    while w < win:
        s = s + pltpu.roll(s, w, 0)
        w *= 2
    if half > 1:
        s = pltpu.roll(s, n_rows - (half - 1), 0)
    total = s[lo:lo + n]
    edge = lax.broadcasted_iota(jnp.int32, (sl, p.shape[1]), 0)
    cnt_first = jnp.where(at_start > 0, jnp.minimum(edge + half, win), win).astype(F32)
    cnt_last = jnp.where(at_end > 0, jnp.minimum(sl - edge + half, win), win).astype(F32)
    mean = jnp.concatenate([total[0:sl] / cnt_first,
                            total[sl:n - sl] * (1.0 / win),
                            total[n - sl:] / cnt_last], axis=0)
    return mean - p[lo:lo + n]


def _inproj_kernel(xp_ref, x_ref, xn_ref, g_ref, w_ref, b_ref, cw_ref, cb_ref,
                   pw_ref, pb_ref, ps_ref, u_ref, gate_ref, yp_ref, p_ref, *, chunks_per_seq):
    i = pl.program_id(0)
    rows = x_ref.shape[0]
    halo = xp_ref.shape[0]
    dh = u_ref.shape[1]
    cb = MXU_DIM
    keep_prev = (i % chunks_per_seq != 0).astype(F32)
    keep_next = (i % chunks_per_seq != chunks_per_seq - 1).astype(F32)

    xw = jnp.concatenate([xp_ref[...], x_ref[...], xn_ref[...]], axis=0)
    h = (_rms_scale(xw) * g_ref[...]).astype(BF16)

    h_main = h[halo:halo + rows]

    def project(rows_h, c0):
        return _dot(rows_h, w_ref[:, c0:c0 + cb]) + b_ref[:, c0:c0 + cb]

    def conv_input(c0):
        p = project(h, c0)
        return jnp.concatenate(
            [p[0:halo] * keep_prev, p[halo:halo + rows], p[halo + rows:] * keep_next], axis=0)

    def conv(c0):
        return _conv3(conv_input(c0), cw_ref[:, c0:c0 + cb], cb_ref[:, c0:c0 + cb], halo, rows)

    dp = yp_ref.shape[1]
    assert dp == len(POOL_WINDOWS) * cb
    for g, win in enumerate(POOL_WINDOWS):
        lanes = slice(g * cb, (g + 1) * cb)
        pooled = _window_mean_minus_self(conv_input(4 * dh + g * cb), win, halo, rows,
                                         1.0 - keep_prev, 1.0 - keep_next)
        y = _dot(pooled.astype(BF16), pw_ref[g]) + pb_ref[:, lanes]
        z = project(h_main, 4 * dh + dp + g * cb)
        yp_ref[:, lanes] = (y * ps_ref[:, lanes] * jax.nn.silu(z)).astype(BF16)

    for c in range(dh // cb):
        lanes = slice(c * cb, (c + 1) * cb)
        u_ref[:, lanes] = conv(dh + c * cb) * conv(2 * dh + c * cb)
        z = project(h_main, 3 * dh + c * cb)
        gate_ref[:, lanes] = (conv(c * cb) * jax.nn.silu(z)).astype(BF16)

    for c0 in range(0, p_ref.shape[1], cb):
        p_ref[:, c0:c0 + cb] = project(h_main, 4 * dh + 2 * dp + c0).astype(BF16)


def _inproj(x2, g_norm, w_in_bf, b_in, conv_w, conv_b, pool_w_bf, pool_b, pool_scale, seq):
    n_tok, d = x2.shape
    cols = w_in_bf.shape[1]
    dh = conv_b.shape[1] // 3
    dp = pool_b.shape[1]
    n_gate = cols - 4 * dh - 2 * dp
    rows = ROW_CHUNK
    halo = 2 * SUBLANES
    per_halo = rows // halo
    last_halo = n_tok // halo - 1
    full = lambda a: pl.BlockSpec(a.shape, lambda i: (0,) * a.ndim)
    tile = lambda width: pl.BlockSpec((rows, width), lambda i: (i, 0))
    return pl.pallas_call(
        functools.partial(_inproj_kernel, chunks_per_seq=seq // rows),
        grid=(n_tok // rows,),
        in_specs=[
            pl.BlockSpec((halo, d), lambda i: (jnp.maximum(i * per_halo - 1, 0), 0)),
            tile(d),
            pl.BlockSpec((halo, d), lambda i: (jnp.minimum((i + 1) * per_halo, last_halo), 0)),
            full(g_norm),
            pl.BlockSpec(w_in_bf.shape, lambda i: (0, 0), pipeline_mode=pl.Buffered(1)),
            full(b_in), full(conv_w), full(conv_b),
            full(pool_w_bf), full(pool_b), full(pool_scale),
        ],
        out_specs=[tile(dh), tile(dh), tile(dp), tile(n_gate)],
        out_shape=[
            jax.ShapeDtypeStruct((n_tok, dh), F32),
            jax.ShapeDtypeStruct((n_tok, dh), BF16),
            jax.ShapeDtypeStruct((n_tok, dp), BF16),
            jax.ShapeDtypeStruct((n_tok, n_gate), BF16),
        ],
        compiler_params=pltpu.CompilerParams(
            dimension_semantics=("parallel",), vmem_limit_bytes=VMEM_LIMIT),
        name="inproj",
    )(x2, x2, x2, g_norm, w_in_bf, b_in, conv_w, conv_b, pool_w_bf, pool_b, pool_scale)


def _hyena_kernel(u_ref, gate_ref, d_ref, kre_ref, kim_ref, cs_ref, y_ref, av_ref, acc_ref, pq_ref):
    seq = u_ref.shape[0]
    lb = cs_ref.shape[0]
    nb = seq // lb
    mid = nb - 1
    blocks = [slice(b * lb, (b + 1) * lb) for b in range(nb)]
    chunks = [slice(r0, r0 + SPEC_ROWS) for r0 in range(0, lb, SPEC_ROWS)]

    def forward(b):
        ub = u_ref[blocks[b], :].astype(BF16)
        av_ref[2 * b] = _dot(cs_ref[:, 0:lb], ub)
        av_ref[2 * b + 1] = _dot(cs_ref[:, lb:2 * lb], ub)

    def spectrum(b, rows):
        return av_ref[2 * b, rows, :], av_ref[2 * b + 1, rows, :]

    def products(a, b, rows, ur, vi):
        kr = kre_ref[mid + a - b, rows, :]
        ki = kim_ref[mid + a - b, rows, :]
        return kr * ur + ki * vi, kr * vi - ki * ur

    def accumulate(b):
        for rows in chunks:
            ur, vi = spectrum(b, rows)
            for a in range(nb):
                tp, tq = products(a, b, rows, ur, vi)
                if b == 0:
                    acc_ref[2 * a, rows, :] = tp
                    acc_ref[2 * a + 1, rows, :] = tq
                else:
                    acc_ref[2 * a, rows, :] += tp
                    acc_ref[2 * a + 1, rows, :] += tq

    def finish(a):
        for rows in chunks:
            tp, tq = products(a, nb - 1, rows, *spectrum(nb - 1, rows))
            pq_ref[a, rows, :] = (acc_ref[2 * a, rows, :] + tp).astype(BF16)
            pq_ref[a, lb + rows.start:lb + rows.stop, :] = (
                acc_ref[2 * a + 1, rows, :] + tq).astype(BF16)

    def inverse(a):
        rows = blocks[a]
        y = _dot(cs_ref[...], pq_ref[a])
        y = y + u_ref[rows, :] * d_ref[...]
        y_ref[rows, :] = (y * gate_ref[rows, :].astype(F32)).astype(BF16)

    forward(0)
    for b in range(nb - 1):
        forward(b + 1)
        accumulate(b)
    for a in range(nb):
        finish(a)
        if a > 0:
            inverse(a - 1)
    inverse(nb - 1)


def _hyena(u, gate, hyena_d, kre, kim, cs, batch, seq):
    dh = hyena_d.shape[1]
    cw = HYENA_COLS
    lb = cs.shape[0]
    nb = seq // lb
    act = pl.BlockSpec((seq, cw), lambda c, b: (b, c))
    spectra = pl.BlockSpec((2 * nb - 1, lb, cw), lambda c, b: (0, 0, c),
                           pipeline_mode=pl.Buffered(1))
    return pl.pallas_call(
        _hyena_kernel,
        grid=(dh // cw, batch),
        in_specs=[
            act, act,
            pl.BlockSpec((1, cw), lambda c, b: (0, c)),
            spectra, spectra,
            pl.BlockSpec(cs.shape, lambda c, b: (0, 0)),
        ],
        out_specs=act,
        out_shape=jax.ShapeDtypeStruct((batch * seq, dh), BF16),
        scratch_shapes=[
            pltpu.VMEM((2 * nb, lb, cw), F32),
            pltpu.VMEM((2 * nb, lb, cw), F32),
            pltpu.VMEM((nb, 2 * lb, cw), BF16),
        ],
        compiler_params=pltpu.CompilerParams(
            dimension_semantics=("arbitrary", "arbitrary"), vmem_limit_bytes=VMEM_LIMIT),
        name="hyena",
    )(u, gate, hyena_d, kre, kim, cs)


def _merge_kernel(yh_ref, yp_ref, g0_ref, g1_ref, x_ref, wh_ref, wp_ref, wo_ref, gf_ref, o_ref):
    out_h = _dot(yh_ref[...], wh_ref[...])
    out_p = _dot(yp_ref[...], wp_ref[...])
    merged = (jax.nn.sigmoid(g0_ref[...].astype(F32)) * out_h
              + jax.nn.sigmoid(g1_ref[...].astype(F32)) * out_p)
    res = x_ref[...] + _dot(merged.astype(BF16), wo_ref[...])
    o_ref[...] = _rms_scale(res) * gf_ref[...]


def _merge(yh, yp, gates, x2, wh_bf, wp_bf, wo_bf, g_final, tm=1024):
    n_tok, d = x2.shape
    tile = lambda col: pl.BlockSpec((tm, d), lambda i: (i, col))
    resident = lambda a: pl.BlockSpec(a.shape, lambda i: (0, 0), pipeline_mode=pl.Buffered(1))
    return pl.pallas_call(
        _merge_kernel,
        grid=(n_tok // tm,),
        in_specs=[
            tile(0), tile(0), tile(0), tile(1), tile(0),
            resident(wh_bf), resident(wp_bf), resident(wo_bf),
            pl.BlockSpec((1, d), lambda i: (0, 0)),
        ],
        out_specs=tile(0),
        out_shape=jax.ShapeDtypeStruct((n_tok, d), F32),
        compiler_params=pltpu.CompilerParams(
            dimension_semantics=("parallel",), vmem_limit_bytes=VMEM_LIMIT),
        name="merge",
    )(yh, yp, gates, gates, x2, wh_bf, wp_bf, wo_bf, g_final)


def _filter_features(seq):
    t = jnp.linspace(0.0, 1.0, seq, dtype=F32)[:, None]
    bands = (FILTER_EMB - 1) // 2
    w = 2.0 * math.pi * jnp.arange(seq, dtype=F32) / seq
    f = jnp.linspace(1e-4, bands - 1, bands, dtype=F32)
    ang = w[:, None] * f[None, :]
    return jnp.concatenate([t, jnp.cos(ang), -jnp.sin(ang)], axis=-1)


def kernel(x, g_norm, w_in, b_in, conv_w, conv_b, filt_w1, filt_b1, filt_w2, filt_b2, filt_w3, filt_b3, filt_w4, filt_freq, hyena_d, w_hyena_out, pool_w, pool_b, pool_scale, w_pool_out, w_out, g_final):
    batch, seq, d = x.shape
    dh = hyena_d.shape[1]
    dp = pool_b.shape[1]
    assert g_norm.shape[0] == 1, "single-layer block"
    assert d == dh == dp and conv_b.shape[1] == 3 * dh and w_in.shape[2] == 4 * dh + 2 * dp + 2 * d
    assert dh % HYENA_COLS == 0 and seq % ROW_CHUNK == 0
    assert seq % CONV_BLOCKS == 0 and (seq // CONV_BLOCKS) % SPEC_ROWS == 0

    cs = _transform_matrices(seq // CONV_BLOCKS)
    z_pad = jnp.pad(_filter_features(seq), ((0, 0), (0, LANES - FILTER_EMB)))
    w1_pad = jnp.pad(filt_w1[0], ((0, LANES - FILTER_EMB), (0, 0)))
    deltas = jnp.abs(jnp.linspace(MIN_DECAY, MAX_DECAY, dh, dtype=F32))[None, :]
    kre, kim = _filter_spectra(z_pad, w1_pad, filt_b1, filt_w2[0], filt_b2, filt_w3[0], filt_b3,
                               filt_freq, filt_w4[0], deltas, cs)

    x2 = x.reshape(batch * seq, d)
    u, gate, yp, gates = _inproj(x2, g_norm, w_in[0].astype(BF16), b_in, conv_w[0], conv_b,
                                 pool_w[0].astype(BF16), pool_b, pool_scale, seq)
    yh = _hyena(u, gate, hyena_d, kre, kim, cs, batch, seq)
    out = _merge(yh, yp, gates, x2, w_hyena_out[0].astype(BF16), w_pool_out[0].astype(BF16),
                 w_out[0].astype(BF16), g_final[None, :])
    return out.reshape(batch, seq, d)
```

```python
import functools
import math

import jax
import jax.numpy as jnp
from jax import lax
from jax.experimental import pallas as pl
from jax.experimental.pallas import tpu as pltpu

FILTER_EMB = 33
FAST_DECAY_PCT = 0.3
SLOW_DECAY_PCT = 1.5
DECAY_TARGET = 1e-2
MAX_DECAY = math.log(DECAY_TARGET) / FAST_DECAY_PCT
MIN_DECAY = math.log(DECAY_TARGET) / SLOW_DECAY_PCT
POOL_WINDOWS = (2, 4, 8, 16)
NORM_EPS = 1e-6

LANES = 128
SUBLANES = 8
MXU_DIM = 256
HYENA_COLS = MXU_DIM
ROW_CHUNK = 512
CONV_BLOCKS = 4
SPEC_ROWS = 64
VMEM_LIMIT = 56 * 1024 * 1024

F32 = jnp.float32
BF16 = jnp.bfloat16


def _dot(a, b):
    return jnp.dot(a, b, preferred_element_type=F32)


def _dot_hi(a, b):
    return jnp.dot(a, b, preferred_element_type=F32, precision=lax.Precision.HIGHEST)


def _rms_scale(x):
    return x * lax.rsqrt(jnp.mean(x * x, axis=-1, keepdims=True) + NORM_EPS)


def _transform_matrices(lb):
    period = 8 * lb
    odd = 2 * jnp.arange(lb, dtype=jnp.int32) + 1
    ang = ((odd[:, None] * odd[None, :]) % period).astype(F32) * (2.0 * math.pi / period)
    return jnp.concatenate([jnp.cos(ang), jnp.sin(ang)], axis=1).astype(BF16)


def _filter_kernel(z_ref, w1_ref, b1_ref, w2_ref, b2_ref, w3_ref, b3_ref, fr_ref,
                   w4f_ref, w4b_ref, dl_ref, cs_ref, kre_ref, kim_ref,
                   h_ref, cphi_ref, sphi_ref, sgn_ref, kk_ref, xr_ref, xi_ref):
    seq = z_ref.shape[0]
    lb = cs_ref.shape[0]
    nb = seq // lb
    cb = dl_ref.shape[1]
    scale = 1.0 / lb

    @pl.when(pl.program_id(0) == 0)
    def _():
        fr = fr_ref[...]
        h = jnp.sin(fr * (_dot_hi(z_ref[...], w1_ref[...]) + b1_ref[...]))
        h = jnp.sin(fr * (_dot_hi(h, w2_ref[...]) + b2_ref[...]))
        h_ref[...] = jnp.sin(fr * (_dot_hi(h, w3_ref[...]) + b3_ref[...]))
        f_idx = lax.broadcasted_iota(jnp.int32, cphi_ref.shape, 0)
        phi = (f_idx.astype(F32) + 0.5) * (math.pi / (2 * lb))
        cphi_ref[...] = jnp.cos(phi) * scale
        sphi_ref[...] = jnp.sin(phi) * scale
        sgn_ref[...] = (1 - 2 * (f_idx & 1)).astype(F32)

    h = h_ref[...]
    decay = jnp.exp(-z_ref[:, 0:1] * dl_ref[...])
    taps = (_dot_hi(h, w4f_ref[...]) * decay, _dot_hi(h, w4b_ref[...]) * decay)
    taps = tuple(k.astype(BF16) for k in taps)
    for side, k in enumerate(taps):
        for s_ in range(nb):
            c0 = (side * nb + s_) * cb
            kk_ref[:, c0:c0 + cb] = k[s_ * lb:(s_ + 1) * lb]

    group = 2 * cb
    tile = lambda ref, width: jnp.concatenate([ref[...]] * (width // LANES), axis=1)
    c_phi, s_phi = tile(cphi_ref, group), tile(sphi_ref, group)
    for g in range(2 * nb * cb // group):
        cols = slice(g * group, (g + 1) * group)
        kk = kk_ref[:, cols]
        a = _dot(cs_ref[:, 0:lb], kk)
        b = _dot(cs_ref[:, lb:2 * lb], kk)
        xr_ref[:, cols] = c_phi * a + s_phi * b
        xi_ref[:, cols] = s_phi * a - c_phi * b

    sgn = tile(sgn_ref, cb)

    def seg(ref, side, s_):
        c0 = (side * nb + s_) * cb
        return ref[:, c0:c0 + cb]

    def causal_block(side, delta):
        r0 = (delta - 1) * lb
        first = taps[side][r0:r0 + 1].astype(F32) * scale
        re = seg(xr_ref, side, delta) - sgn * seg(xi_ref, side, delta - 1)
        im = seg(xi_ref, side, delta) + sgn * (seg(xr_ref, side, delta - 1) - first)
        return re, im

    mid = nb - 1
    kre_ref[mid] = seg(xr_ref, 0, 0) + seg(xr_ref, 1, 0)
    kim_ref[mid] = seg(xi_ref, 0, 0) - seg(xi_ref, 1, 0)
    for delta in range(1, nb):
        re, im = causal_block(0, delta)
        kre_ref[mid + delta] = re
        kim_ref[mid + delta] = im
        re, im = causal_block(1, delta)
        kre_ref[mid - delta] = re
        kim_ref[mid - delta] = -im


def _filter_spectra(z_pad, w1_pad, b1, w2, b2, w3, b3, freq, w4, deltas, cs):
    seq = z_pad.shape[0]
    lb = cs.shape[0]
    nb = seq // lb
    hf = w2.shape[0]
    dh = deltas.shape[1]
    cb = MXU_DIM
    nblk = dh // cb
    full = lambda a: pl.BlockSpec(a.shape, lambda c: (0,) * a.ndim)
    return pl.pallas_call(
        _filter_kernel,
        grid=(nblk,),
        in_specs=[
            full(z_pad), full(w1_pad), full(b1), full(w2), full(b2), full(w3), full(b3), full(freq),
            pl.BlockSpec((hf, cb), lambda c: (0, c)),
            pl.BlockSpec((hf, cb), lambda c: (0, nblk + c)),
            pl.BlockSpec((1, cb), lambda c: (0, c)),
            full(cs),
        ],
        out_specs=[pl.BlockSpec((2 * nb - 1, lb, cb), lambda c: (0, 0, c))] * 2,
        out_shape=[jax.ShapeDtypeStruct((2 * nb - 1, lb, dh), F32)] * 2,
        scratch_shapes=[
            pltpu.VMEM((seq, hf), F32),
            pltpu.VMEM((lb, LANES), F32),
            pltpu.VMEM((lb, LANES), F32),
            pltpu.VMEM((lb, LANES), F32),
            pltpu.VMEM((lb, 2 * nb * cb), BF16),
            pltpu.VMEM((lb, 2 * nb * cb), F32),
            pltpu.VMEM((lb, 2 * nb * cb), F32),
        ],
        compiler_params=pltpu.CompilerParams(
            dimension_semantics=("arbitrary",), vmem_limit_bytes=VMEM_LIMIT),
        name="filter_spectra",
    )(z_pad, w1_pad, b1, w2, b2, w3, b3, freq, w4, w4, deltas, cs)


def _conv3(p, w, bias, lo, n):
    n_rows = p.shape[0]
    assert 1 <= lo and lo + n < n_rows
    dn = pltpu.roll(p, 1, 0)
    up = pltpu.roll(p, n_rows - 1, 0)
    out = dn * w[0:1] + p * w[1:2] + up * w[2:3] + bias
    return out[lo:lo + n]


def _window_mean_minus_self(p, win, lo, n, at_start, at_end):
    n_rows = p.shape[0]
    half = win // 2
    sl = SUBLANES
    assert lo >= win and lo + n + win <= n_rows and half <= sl
    s = p
    w = 1
    while w < win:
        s = s + pltpu.roll(s, w, 0)
        w *= 2
    if half > 1:
        s = pltpu.roll(s, n_rows - (half - 1), 0)
    total = s[lo:lo + n]
    edge = lax.broadcasted_iota(jnp.int32, (sl, p.shape[1]), 0)
    cnt_first = jnp.where(at_start > 0, jnp.minimum(edge + half, win), win).astype(F32)
    cnt_last = jnp.where(at_end > 0, jnp.minimum(sl - edge + half, win), win).astype(F32)
    mean = jnp.concatenate([total[0:sl] / cnt_first,
                            total[sl:n - sl] * (1.0 / win),
                            total[n - sl:] / cnt_last], axis=0)
    return mean - p[lo:lo + n]


def _inproj_kernel(xp_ref, x_ref, xn_ref, g_ref, w_ref, b_ref, cw_ref, cb_ref,
                   pw_ref, pb_ref, ps_ref, u_ref, gate_ref, yp_ref, p_ref, *, chunks_per_seq):
    i = pl.program_id(0)
    rows = x_ref.shape[0]
    halo = xp_ref.shape[0]
    dh = u_ref.shape[0] * u_ref.shape[2]
    cb = MXU_DIM
    keep_prev = (i % chunks_per_seq != 0).astype(F32)
    keep_next = (i % chunks_per_seq != chunks_per_seq - 1).astype(F32)

    xw = jnp.concatenate([xp_ref[...], x_ref[...], xn_ref[...]], axis=0)
    h = (_rms_scale(xw) * g_ref[...]).astype(BF16)

    h_main = h[halo:halo + rows]

    def project(rows_h, c0):
        return _dot(rows_h, w_ref[:, c0:c0 + cb]) + b_ref[:, c0:c0 + cb]

    def conv_input(c0):
        p = project(h, c0)
        return jnp.concatenate(
            [p[0:halo] * keep_prev, p[halo:halo + rows], p[halo + rows:] * keep_next], axis=0)

    def conv(c0):
        return _conv3(conv_input(c0), cw_ref[:, c0:c0 + cb], cb_ref[:, c0:c0 + cb], halo, rows)

    dp = yp_ref.shape[1]
    assert dp == len(POOL_WINDOWS) * cb
    for g, win in enumerate(POOL_WINDOWS):
        lanes = slice(g * cb, (g + 1) * cb)
        pooled = _window_mean_minus_self(conv_input(4 * dh + g * cb), win, halo, rows,
                                         1.0 - keep_prev, 1.0 - keep_next)
        y = _dot(pooled.astype(BF16), pw_ref[g]) + pb_ref[:, lanes]
        z = project(h_main, 4 * dh + dp + g * cb)
        yp_ref[:, lanes] = (y * ps_ref[:, lanes] * jax.nn.silu(z)).astype(BF16)

    for c in range(dh // cb):
        u_ref[c] = conv(dh + c * cb) * conv(2 * dh + c * cb)
        z = project(h_main, 3 * dh + c * cb)
        gate_ref[c] = (conv(c * cb) * jax.nn.silu(z)).astype(BF16)

    for c0 in range(0, p_ref.shape[1], cb):
        p_ref[:, c0:c0 + cb] = project(h_main, 4 * dh + 2 * dp + c0).astype(BF16)


def _inproj(x2, g_norm, w_in_bf, b_in, conv_w, conv_b, pool_w_bf, pool_b, pool_scale, seq):
    n_tok, d = x2.shape
    cols = w_in_bf.shape[1]
    dh = conv_b.shape[1] // 3
    dp = pool_b.shape[1]
    n_gate = cols - 4 * dh - 2 * dp
    rows = ROW_CHUNK
    halo = 2 * SUBLANES
    per_halo = rows // halo
    last_halo = n_tok // halo - 1
    full = lambda a: pl.BlockSpec(a.shape, lambda i: (0,) * a.ndim)
    tile = lambda width: pl.BlockSpec((rows, width), lambda i: (i, 0))
    blocked = pl.BlockSpec((dh // MXU_DIM, rows, MXU_DIM), lambda i: (0, i, 0))
    return pl.pallas_call(
        functools.partial(_inproj_kernel, chunks_per_seq=seq // rows),
        grid=(n_tok // rows,),
        in_specs=[
            pl.BlockSpec((halo, d), lambda i: (jnp.maximum(i * per_halo - 1, 0), 0)),
            tile(d),
            pl.BlockSpec((halo, d), lambda i: (jnp.minimum((i + 1) * per_halo, last_halo), 0)),
            full(g_norm),
            pl.BlockSpec(w_in_bf.shape, lambda i: (0, 0), pipeline_mode=pl.Buffered(1)),
            full(b_in), full(conv_w), full(conv_b),
            full(pool_w_bf), full(pool_b), full(pool_scale),
        ],
        out_specs=[blocked, blocked, tile(dp), tile(n_gate)],
        out_shape=[
            jax.ShapeDtypeStruct((dh // MXU_DIM, n_tok, MXU_DIM), F32),
            jax.ShapeDtypeStruct((dh // MXU_DIM, n_tok, MXU_DIM), BF16),
            jax.ShapeDtypeStruct((n_tok, dp), BF16),
            jax.ShapeDtypeStruct((n_tok, n_gate), BF16),
        ],
        compiler_params=pltpu.CompilerParams(
            dimension_semantics=("parallel",), vmem_limit_bytes=VMEM_LIMIT),
        name="inproj",
    )(x2, x2, x2, g_norm, w_in_bf, b_in, conv_w, conv_b, pool_w_bf, pool_b, pool_scale)


def _hyena_kernel(u_ref, gate_ref, d_ref, kre_ref, kim_ref, cs_ref, y_ref, av_ref, acc_ref, pq_ref):
    seq = u_ref.shape[0]
    lb = cs_ref.shape[0]
    nb = seq // lb
    mid = nb - 1
    blocks = [slice(b * lb, (b + 1) * lb) for b in range(nb)]
    chunks = [slice(r0, r0 + SPEC_ROWS) for r0 in range(0, lb, SPEC_ROWS)]

    def forward(b):
        ub = u_ref[blocks[b], :].astype(BF16)
        av_ref[2 * b] = _dot(cs_ref[:, 0:lb], ub)
        av_ref[2 * b + 1] = _dot(cs_ref[:, lb:2 * lb], ub)

    def spectrum(b, rows):
        return av_ref[2 * b, rows, :], av_ref[2 * b + 1, rows, :]

    def products(a, b, rows, ur, vi):
        kr = kre_ref[mid + a - b, rows, :]
        ki = kim_ref[mid + a - b, rows, :]
        return kr * ur + ki * vi, kr * vi - ki * ur

    def accumulate(b):
        for rows in chunks:
            ur, vi = spectrum(b, rows)
            for a in range(nb):
                tp, tq = products(a, b, rows, ur, vi)
                if b == 0:
                    acc_ref[2 * a, rows, :] = tp
                    acc_ref[2 * a + 1, rows, :] = tq
                else:
                    acc_ref[2 * a, rows, :] += tp
                    acc_ref[2 * a + 1, rows, :] += tq

    def finish(a):
        for rows in chunks:
            tp, tq = products(a, nb - 1, rows, *spectrum(nb - 1, rows))
            pq_ref[a, rows, :] = (acc_ref[2 * a, rows, :] + tp).astype(BF16)
            pq_ref[a, lb + rows.start:lb + rows.stop, :] = (
                acc_ref[2 * a + 1, rows, :] + tq).astype(BF16)

    def inverse(a):
        rows = blocks[a]
        y = _dot(cs_ref[...], pq_ref[a])
        y = y + u_ref[rows, :] * d_ref[...]
        y_ref[rows, :] = (y * gate_ref[rows, :].astype(F32)).astype(BF16)

    forward(0)
    for b in range(nb - 1):
        forward(b + 1)
        accumulate(b)
    for a in range(nb):
        finish(a)
        if a > 0:
            inverse(a - 1)
    inverse(nb - 1)


def _hyena(u, gate, hyena_d, kre, kim, cs, batch, seq):
    dh = hyena_d.shape[1]
    cw = HYENA_COLS
    lb = cs.shape[0]
    nb = seq // lb
    act = pl.BlockSpec((None, seq, cw), lambda c, b: (c, b, 0))
    spectra = pl.BlockSpec((2 * nb - 1, lb, cw), lambda c, b: (0, 0, c),
                           pipeline_mode=pl.Buffered(1))
    return pl.pallas_call(
        _hyena_kernel,
        grid=(dh // cw, batch),
        in_specs=[
            act, act,
            pl.BlockSpec((1, cw), lambda c, b: (0, c)),
            spectra, spectra,
            pl.BlockSpec(cs.shape, lambda c, b: (0, 0)),
        ],
        out_specs=act,
        out_shape=jax.ShapeDtypeStruct((dh // cw, batch * seq, cw), BF16),
        scratch_shapes=[
            pltpu.VMEM((2 * nb, lb, cw), F32),
            pltpu.VMEM((2 * nb, lb, cw), F32),
            pltpu.VMEM((nb, 2 * lb, cw), BF16),
        ],
        compiler_params=pltpu.CompilerParams(
            dimension_semantics=("arbitrary", "arbitrary"), vmem_limit_bytes=VMEM_LIMIT),
        name="hyena",
    )(u, gate, hyena_d, kre, kim, cs)


def _merge_kernel(yh_ref, yp_ref, g0_ref, g1_ref, x_ref, wh_ref, wp_ref, wo_ref, gf_ref, o_ref):
    yh = jnp.concatenate([yh_ref[c] for c in range(yh_ref.shape[0])], axis=1)
    out_h = _dot(yh, wh_ref[...])
    out_p = _dot(yp_ref[...], wp_ref[...])
    merged = (jax.nn.sigmoid(g0_ref[...].astype(F32)) * out_h
              + jax.nn.sigmoid(g1_ref[...].astype(F32)) * out_p)
    res = x_ref[...] + _dot(merged.astype(BF16), wo_ref[...])
    o_ref[...] = _rms_scale(res) * gf_ref[...]


def _merge(yh, yp, gates, x2, wh_bf, wp_bf, wo_bf, g_final, tm=1024):
    n_tok, d = x2.shape
    tile = lambda col: pl.BlockSpec((tm, d), lambda i: (i, col))
    resident = lambda a: pl.BlockSpec(a.shape, lambda i: (0, 0), pipeline_mode=pl.Buffered(1))
    return pl.pallas_call(
        _merge_kernel,
        grid=(n_tok // tm,),
        in_specs=[
            pl.BlockSpec((yh.shape[0], tm, yh.shape[2]), lambda i: (0, i, 0)),
            tile(0), tile(0), tile(1), tile(0),
            resident(wh_bf), resident(wp_bf), resident(wo_bf),
            pl.BlockSpec((1, d), lambda i: (0, 0)),
        ],
        out_specs=tile(0),
        out_shape=jax.ShapeDtypeStruct((n_tok, d), F32),
        compiler_params=pltpu.CompilerParams(
            dimension_semantics=("parallel",), vmem_limit_bytes=VMEM_LIMIT),
        name="merge",
    )(yh, yp, gates, gates, x2, wh_bf, wp_bf, wo_bf, g_final)


def _filter_features(seq):
    t = jnp.linspace(0.0, 1.0, seq, dtype=F32)[:, None]
    bands = (FILTER_EMB - 1) // 2
    w = 2.0 * math.pi * jnp.arange(seq, dtype=F32) / seq
    f = jnp.linspace(1e-4, bands - 1, bands, dtype=F32)
    ang = w[:, None] * f[None, :]
    return jnp.concatenate([t, jnp.cos(ang), -jnp.sin(ang)], axis=-1)


def kernel(x, g_norm, w_in, b_in, conv_w, conv_b, filt_w1, filt_b1, filt_w2, filt_b2, filt_w3, filt_b3, filt_w4, filt_freq, hyena_d, w_hyena_out, pool_w, pool_b, pool_scale, w_pool_out, w_out, g_final):
    batch, seq, d = x.shape
    dh = hyena_d.shape[1]
    dp = pool_b.shape[1]
    assert g_norm.shape[0] == 1, "single-layer block"
    assert d == dh == dp and conv_b.shape[1] == 3 * dh and w_in.shape[2] == 4 * dh + 2 * dp + 2 * d
    assert dh % HYENA_COLS == 0 and HYENA_COLS == MXU_DIM and seq % ROW_CHUNK == 0
    assert seq % CONV_BLOCKS == 0 and (seq // CONV_BLOCKS) % SPEC_ROWS == 0

    cs = _transform_matrices(seq // CONV_BLOCKS)
    z_pad = jnp.pad(_filter_features(seq), ((0, 0), (0, LANES - FILTER_EMB)))
    w1_pad = jnp.pad(filt_w1[0], ((0, LANES - FILTER_EMB), (0, 0)))
    deltas = jnp.abs(jnp.linspace(MIN_DECAY, MAX_DECAY, dh, dtype=F32))[None, :]
    kre, kim = _filter_spectra(z_pad, w1_pad, filt_b1, filt_w2[0], filt_b2, filt_w3[0], filt_b3,
                               filt_freq, filt_w4[0], deltas, cs)

    x2 = x.reshape(batch * seq, d)
    u, gate, yp, gates = _inproj(x2, g_norm, w_in[0].astype(BF16), b_in, conv_w[0], conv_b,
                                 pool_w[0].astype(BF16), pool_b, pool_scale, seq)
    yh = _hyena(u, gate, hyena_d, kre, kim, cs, batch, seq)
    out = _merge(yh, yp, gates, x2, w_hyena_out[0].astype(BF16), w_pool_out[0].astype(BF16),
                 w_out[0].astype(BF16), g_final[None, :])
    return out.reshape(batch, seq, d)
```

```python
import functools
import math

import jax
import jax.numpy as jnp
from jax import lax
from jax.experimental import pallas as pl
from jax.experimental.pallas import tpu as pltpu

FILTER_EMB = 33
FAST_DECAY_PCT = 0.3
SLOW_DECAY_PCT = 1.5
DECAY_TARGET = 1e-2
MAX_DECAY = math.log(DECAY_TARGET) / FAST_DECAY_PCT
MIN_DECAY = math.log(DECAY_TARGET) / SLOW_DECAY_PCT
POOL_WINDOWS = (2, 4, 8, 16)
NORM_EPS = 1e-6

LANES = 128
SUBLANES = 8
MXU_DIM = 256
HYENA_COLS = MXU_DIM
ROW_CHUNK = 1024
CONV_BLOCKS = 4
SPEC_ROWS = 64
VMEM_LIMIT = 56 * 1024 * 1024

F32 = jnp.float32
BF16 = jnp.bfloat16


def _dot(a, b):
    return jnp.dot(a, b, preferred_element_type=F32)


def _dot_hi(a, b):
    return jnp.dot(a, b, preferred_element_type=F32, precision=lax.Precision.HIGHEST)


def _rms_scale(x):
    return x * lax.rsqrt(jnp.mean(x * x, axis=-1, keepdims=True) + NORM_EPS)


def _transform_matrices(lb):
    period = 8 * lb
    odd = 2 * jnp.arange(lb, dtype=jnp.int32) + 1
    ang = ((odd[:, None] * odd[None, :]) % period).astype(F32) * (2.0 * math.pi / period)
    return jnp.concatenate([jnp.cos(ang), jnp.sin(ang)], axis=1).astype(BF16)


def _filter_kernel(z_ref, w1_ref, b1_ref, w2_ref, b2_ref, w3_ref, b3_ref, fr_ref,
                   w4f_ref, w4b_ref, dl_ref, cs_ref, kre_ref, kim_ref,
                   h_ref, cphi_ref, sphi_ref, sgn_ref, kk_ref, xr_ref, xi_ref):
    seq = z_ref.shape[0]
    lb = cs_ref.shape[0]
    nb = seq // lb
    cb = dl_ref.shape[1]
    scale = 1.0 / lb

    @pl.when(pl.program_id(0) == 0)
    def _():
        fr = fr_ref[...]
        h = jnp.sin(fr * (_dot_hi(z_ref[...], w1_ref[...]) + b1_ref[...]))
        h = jnp.sin(fr * (_dot_hi(h, w2_ref[...]) + b2_ref[...]))
        h_ref[...] = jnp.sin(fr * (_dot_hi(h, w3_ref[...]) + b3_ref[...]))
        f_idx = lax.broadcasted_iota(jnp.int32, cphi_ref.shape, 0)
        phi = (f_idx.astype(F32) + 0.5) * (math.pi / (2 * lb))
        cphi_ref[...] = jnp.cos(phi) * scale
        sphi_ref[...] = jnp.sin(phi) * scale
        sgn_ref[...] = (1 - 2 * (f_idx & 1)).astype(F32)

    h = h_ref[...]
    decay = jnp.exp(-z_ref[:, 0:1] * dl_ref[...])
    taps = (_dot_hi(h, w4f_ref[...]) * decay, _dot_hi(h, w4b_ref[...]) * decay)
    taps = tuple(k.astype(BF16) for k in taps)
    for side, k in enumerate(taps):
        for s_ in range(nb):
            c0 = (side * nb + s_) * cb
            kk_ref[:, c0:c0 + cb] = k[s_ * lb:(s_ + 1) * lb]

    group = 2 * cb
    tile = lambda ref, width: jnp.concatenate([ref[...]] * (width // LANES), axis=1)
    c_phi, s_phi = tile(cphi_ref, group), tile(sphi_ref, group)
    for g in range(2 * nb * cb // group):
        cols = slice(g * group, (g + 1) * group)
        kk = kk_ref[:, cols]
        a = _dot(cs_ref[:, 0:lb], kk)
        b = _dot(cs_ref[:, lb:2 * lb], kk)
        xr_ref[:, cols] = c_phi * a + s_phi * b
        xi_ref[:, cols] = s_phi * a - c_phi * b

    sgn = tile(sgn_ref, cb)

    def seg(ref, side, s_):
        c0 = (side * nb + s_) * cb
        return ref[:, c0:c0 + cb]

    def causal_block(side, delta):
        r0 = (delta - 1) * lb
        first = taps[side][r0:r0 + 1].astype(F32) * scale
        re = seg(xr_ref, side, delta) - sgn * seg(xi_ref, side, delta - 1)
        im = seg(xi_ref, side, delta) + sgn * (seg(xr_ref, side, delta - 1) - first)
        return re, im

    mid = nb - 1
    kre_ref[mid] = seg(xr_ref, 0, 0) + seg(xr_ref, 1, 0)
    kim_ref[mid] = seg(xi_ref, 0, 0) - seg(xi_ref, 1, 0)
    for delta in range(1, nb):
        re, im = causal_block(0, delta)
        kre_ref[mid + delta] = re
        kim_ref[mid + delta] = im
        re, im = causal_block(1, delta)
        kre_ref[mid - delta] = re
        kim_ref[mid - delta] = -im


def _filter_spectra(z_pad, w1_pad, b1, w2, b2, w3, b3, freq, w4, deltas, cs):
    seq = z_pad.shape[0]
    lb = cs.shape[0]
    nb = seq // lb
    hf = w2.shape[0]
    dh = deltas.shape[1]
    cb = MXU_DIM
    nblk = dh // cb
    full = lambda a: pl.BlockSpec(a.shape, lambda c: (0,) * a.ndim)
    return pl.pallas_call(
        _filter_kernel,
        grid=(nblk,),
        in_specs=[
            full(z_pad), full(w1_pad), full(b1), full(w2), full(b2), full(w3), full(b3), full(freq),
            pl.BlockSpec((hf, cb), lambda c: (0, c)),
            pl.BlockSpec((hf, cb), lambda c: (0, nblk + c)),
            pl.BlockSpec((1, cb), lambda c: (0, c)),
            full(cs),
        ],
        out_specs=[pl.BlockSpec((2 * nb - 1, lb, cb), lambda c: (0, 0, c))] * 2,
        out_shape=[jax.ShapeDtypeStruct((2 * nb - 1, lb, dh), F32)] * 2,
        scratch_shapes=[
            pltpu.VMEM((seq, hf), F32),
            pltpu.VMEM((lb, LANES), F32),
            pltpu.VMEM((lb, LANES), F32),
            pltpu.VMEM((lb, LANES), F32),
            pltpu.VMEM((lb, 2 * nb * cb), BF16),
            pltpu.VMEM((lb, 2 * nb * cb), F32),
            pltpu.VMEM((lb, 2 * nb * cb), F32),
        ],
        compiler_params=pltpu.CompilerParams(
            dimension_semantics=("arbitrary",), vmem_limit_bytes=VMEM_LIMIT),
        name="filter_spectra",
    )(z_pad, w1_pad, b1, w2, b2, w3, b3, freq, w4, w4, deltas, cs)


def _conv3(p, w, bias, lo, n):
    n_rows = p.shape[0]
    assert 1 <= lo and lo + n < n_rows
    dn = pltpu.roll(p, 1, 0)
    up = pltpu.roll(p, n_rows - 1, 0)
    out = dn * w[0:1] + p * w[1:2] + up * w[2:3] + bias
    return out[lo:lo + n]


def _window_mean_minus_self(p, win, lo, n, at_start, at_end):
    n_rows = p.shape[0]
    half = win // 2
    sl = SUBLANES
    assert lo >= win and lo + n + win <= n_rows and half <= sl
    s = p
    w = 1
    while w < win:
        s = s + pltpu.roll(s, w, 0)
        w *= 2
    if half > 1:
        s = pltpu.roll(s, n_rows - (half - 1), 0)
    total = s[lo:lo + n]
    edge = lax.broadcasted_iota(jnp.int32, (sl, p.shape[1]), 0)
    cnt_first = jnp.where(at_start > 0, jnp.minimum(edge + half, win), win).astype(F32)
    cnt_last = jnp.where(at_end > 0, jnp.minimum(sl - edge + half, win), win).astype(F32)
    mean = jnp.concatenate([total[0:sl] / cnt_first,
                            total[sl:n - sl] * (1.0 / win),
                            total[n - sl:] / cnt_last], axis=0)
    return mean - p[lo:lo + n]


def _inproj_kernel(xp_ref, x_ref, xn_ref, g_ref, w_ref, b_ref, cw_ref, cb_ref,
                   pw_ref, pb_ref, ps_ref, u_ref, gate_ref, yp_ref, p_ref, *, chunks_per_seq):
    i = pl.program_id(0)
    rows = x_ref.shape[0]
    halo = xp_ref.shape[0]
    dh = u_ref.shape[1]
    cb = MXU_DIM
    keep_prev = (i % chunks_per_seq != 0).astype(F32)
    keep_next = (i % chunks_per_seq != chunks_per_seq - 1).astype(F32)

    xw = jnp.concatenate([xp_ref[...], x_ref[...], xn_ref[...]], axis=0)
    h = (_rms_scale(xw) * g_ref[...]).astype(BF16)

    h_main = h[halo:halo + rows]

    def project(rows_h, c0):
        return _dot(rows_h, w_ref[:, c0:c0 + cb]) + b_ref[:, c0:c0 + cb]

    def conv_input(c0):
        p = project(h, c0)
        return jnp.concatenate(
            [p[0:halo] * keep_prev, p[halo:halo + rows], p[halo + rows:] * keep_next], axis=0)

    def conv(c0):
        return _conv3(conv_input(c0), cw_ref[:, c0:c0 + cb], cb_ref[:, c0:c0 + cb], halo, rows)

    dp = yp_ref.shape[1]
    assert dp == len(POOL_WINDOWS) * cb
    for g, win in enumerate(POOL_WINDOWS):
        lanes = slice(g * cb, (g + 1) * cb)
        pooled = _window_mean_minus_self(conv_input(4 * dh + g * cb), win, halo, rows,
                                         1.0 - keep_prev, 1.0 - keep_next)
        y = _dot(pooled.astype(BF16), pw_ref[g]) + pb_ref[:, lanes]
        z = project(h_main, 4 * dh + dp + g * cb)
        yp_ref[:, lanes] = (y * ps_ref[:, lanes] * jax.nn.silu(z)).astype(BF16)

    for c in range(dh // cb):
        lanes = slice(c * cb, (c + 1) * cb)
        u_ref[:, lanes] = conv(dh + c * cb) * conv(2 * dh + c * cb)
        z = project(h_main, 3 * dh + c * cb)
        gate_ref[:, lanes] = (conv(c * cb) * jax.nn.silu(z)).astype(BF16)

    for c0 in range(0, p_ref.shape[1], cb):
        p_ref[:, c0:c0 + cb] = project(h_main, 4 * dh + 2 * dp + c0).astype(BF16)


def _inproj(x2, g_norm, w_in_bf, b_in, conv_w, conv_b, pool_w_bf, pool_b, pool_scale, seq):
    n_tok, d = x2.shape
    cols = w_in_bf.shape[1]
    dh = conv_b.shape[1] // 3
    dp = pool_b.shape[1]
    n_gate = cols - 4 * dh - 2 * dp
    rows = ROW_CHUNK
    halo = 2 * SUBLANES
    per_halo = rows // halo
    last_halo = n_tok // halo - 1
    full = lambda a: pl.BlockSpec(a.shape, lambda i: (0,) * a.ndim)
    tile = lambda width: pl.BlockSpec((rows, width), lambda i: (i, 0))
    return pl.pallas_call(
        functools.partial(_inproj_kernel, chunks_per_seq=seq // rows),
        grid=(n_tok // rows,),
        in_specs=[
            pl.BlockSpec((halo, d), lambda i: (jnp.maximum(i * per_halo - 1, 0), 0)),
            tile(d),
            pl.BlockSpec((halo, d), lambda i: (jnp.minimum((i + 1) * per_halo, last_halo), 0)),
            full(g_norm),
            pl.BlockSpec(w_in_bf.shape, lambda i: (0, 0), pipeline_mode=pl.Buffered(1)),
            full(b_in), full(conv_w), full(conv_b),
            full(pool_w_bf), full(pool_b), full(pool_scale),
        ],
        out_specs=[tile(dh), tile(dh), tile(dp), tile(n_gate)],
        out_shape=[
            jax.ShapeDtypeStruct((n_tok, dh), F32),
            jax.ShapeDtypeStruct((n_tok, dh), BF16),
            jax.ShapeDtypeStruct((n_tok, dp), BF16),
            jax.ShapeDtypeStruct((n_tok, n_gate), BF16),
        ],
        compiler_params=pltpu.CompilerParams(
            dimension_semantics=("parallel",), vmem_limit_bytes=VMEM_LIMIT),
        name="inproj",
    )(x2, x2, x2, g_norm, w_in_bf, b_in, conv_w, conv_b, pool_w_bf, pool_b, pool_scale)


def _hyena_kernel(u_ref, gate_ref, d_ref, kre_ref, kim_ref, cs_ref, y_ref, av_ref, acc_ref, pq_ref):
    seq = u_ref.shape[0]
    lb = cs_ref.shape[0]
    nb = seq // lb
    mid = nb - 1
    blocks = [slice(b * lb, (b + 1) * lb) for b in range(nb)]
    chunks = [slice(r0, r0 + SPEC_ROWS) for r0 in range(0, lb, SPEC_ROWS)]

    def forward(b):
        ub = u_ref[blocks[b], :].astype(BF16)
        av_ref[2 * b] = _dot(cs_ref[:, 0:lb], ub)
        av_ref[2 * b + 1] = _dot(cs_ref[:, lb:2 * lb], ub)

    def spectrum(b, rows):
        return av_ref[2 * b, rows, :], av_ref[2 * b + 1, rows, :]

    def products(a, b, rows, ur, vi):
        kr = kre_ref[mid + a - b, rows, :]
        ki = kim_ref[mid + a - b, rows, :]
        return kr * ur + ki * vi, kr * vi - ki * ur

    def accumulate(b):
        for rows in chunks:
            ur, vi = spectrum(b, rows)
            for a in range(nb):
                tp, tq = products(a, b, rows, ur, vi)
                if b == 0:
                    acc_ref[2 * a, rows, :] = tp
                    acc_ref[2 * a + 1, rows, :] = tq
                else:
                    acc_ref[2 * a, rows, :] += tp
                    acc_ref[2 * a + 1, rows, :] += tq

    def finish(a):
        for rows in chunks:
            tp, tq = products(a, nb - 1, rows, *spectrum(nb - 1, rows))
            pq_ref[a, rows, :] = (acc_ref[2 * a, rows, :] + tp).astype(BF16)
            pq_ref[a, lb + rows.start:lb + rows.stop, :] = (
                acc_ref[2 * a + 1, rows, :] + tq).astype(BF16)

    def inverse(a):
        rows = blocks[a]
        y = _dot(cs_ref[...], pq_ref[a])
        y = y + u_ref[rows, :] * d_ref[...]
        y_ref[rows, :] = (y * gate_ref[rows, :].astype(F32)).astype(BF16)

    forward(0)
    for b in range(nb - 1):
        forward(b + 1)
        accumulate(b)
    for a in range(nb):
        finish(a)
        if a > 0:
            inverse(a - 1)
    inverse(nb - 1)


def _hyena(u, gate, hyena_d, kre, kim, cs, batch, seq):
    dh = hyena_d.shape[1]
    cw = HYENA_COLS
    lb = cs.shape[0]
    nb = seq // lb
    act = pl.BlockSpec((seq, cw), lambda c, b: (b, c))
    spectra = pl.BlockSpec((2 * nb - 1, lb, cw), lambda c, b: (0, 0, c),
                           pipeline_mode=pl.Buffered(1))
    return pl.pallas_call(
        _hyena_kernel,
        grid=(dh // cw, batch),
        in_specs=[
            act, act,
            pl.BlockSpec((1, cw), lambda c, b: (0, c)),
            spectra, spectra,
            pl.BlockSpec(cs.shape, lambda c, b: (0, 0)),
        ],
        out_specs=act,
        out_shape=jax.ShapeDtypeStruct((batch * seq, dh), BF16),
        scratch_shapes=[
            pltpu.VMEM((2 * nb, lb, cw), F32),
            pltpu.VMEM((2 * nb, lb, cw), F32),
            pltpu.VMEM((nb, 2 * lb, cw), BF16),
        ],
        compiler_params=pltpu.CompilerParams(
            dimension_semantics=("arbitrary", "arbitrary"), vmem_limit_bytes=VMEM_LIMIT),
        name="hyena",
    )(u, gate, hyena_d, kre, kim, cs)


def _merge_kernel(yh_ref, yp_ref, g0_ref, g1_ref, x_ref, wh_ref, wp_ref, wo_ref, gf_ref, o_ref):
    out_h = _dot(yh_ref[...], wh_ref[...])
    out_p = _dot(yp_ref[...], wp_ref[...])
    merged = (jax.nn.sigmoid(g0_ref[...].astype(F32)) * out_h
              + jax.nn.sigmoid(g1_ref[...].astype(F32)) * out_p)
    res = x_ref[...] + _dot(merged.astype(BF16), wo_ref[...])
    o_ref[...] = _rms_scale(res) * gf_ref[...]


def _merge(yh, yp, gates, x2, wh_bf, wp_bf, wo_bf, g_final, tm=1024):
    n_tok, d = x2.shape
    tile = lambda col: pl.BlockSpec((tm, d), lambda i: (i, col))
    resident = lambda a: pl.BlockSpec(a.shape, lambda i: (0, 0), pipeline_mode=pl.Buffered(1))
    return pl.pallas_call(
        _merge_kernel,
        grid=(n_tok // tm,),
        in_specs=[
            tile(0), tile(0), tile(0), tile(1), tile(0),
            resident(wh_bf), resident(wp_bf), resident(wo_bf),
            pl.BlockSpec((1, d), lambda i: (0, 0)),
        ],
        out_specs=tile(0),
        out_shape=jax.ShapeDtypeStruct((n_tok, d), F32),
        compiler_params=pltpu.CompilerParams(
            dimension_semantics=("parallel",), vmem_limit_bytes=VMEM_LIMIT),
        name="merge",
    )(yh, yp, gates, gates, x2, wh_bf, wp_bf, wo_bf, g_final)


def _filter_features(seq):
    t = jnp.linspace(0.0, 1.0, seq, dtype=F32)[:, None]
    bands = (FILTER_EMB - 1) // 2
    w = 2.0 * math.pi * jnp.arange(seq, dtype=F32) / seq
    f = jnp.linspace(1e-4, bands - 1, bands, dtype=F32)
    ang = w[:, None] * f[None, :]
    return jnp.concatenate([t, jnp.cos(ang), -jnp.sin(ang)], axis=-1)


def kernel(x, g_norm, w_in, b_in, conv_w, conv_b, filt_w1, filt_b1, filt_w2, filt_b2, filt_w3, filt_b3, filt_w4, filt_freq, hyena_d, w_hyena_out, pool_w, pool_b, pool_scale, w_pool_out, w_out, g_final):
    batch, seq, d = x.shape
    dh = hyena_d.shape[1]
    dp = pool_b.shape[1]
    assert g_norm.shape[0] == 1, "single-layer block"
    assert d == dh == dp and conv_b.shape[1] == 3 * dh and w_in.shape[2] == 4 * dh + 2 * dp + 2 * d
    assert dh % HYENA_COLS == 0 and seq % ROW_CHUNK == 0
    assert seq % CONV_BLOCKS == 0 and (seq // CONV_BLOCKS) % SPEC_ROWS == 0

    cs = _transform_matrices(seq // CONV_BLOCKS)
    z_pad = jnp.pad(_filter_features(seq), ((0, 0), (0, LANES - FILTER_EMB)))
    w1_pad = jnp.pad(filt_w1[0], ((0, LANES - FILTER_EMB), (0, 0)))
    deltas = jnp.abs(jnp.linspace(MIN_DECAY, MAX_DECAY, dh, dtype=F32))[None, :]
    kre, kim = _filter_spectra(z_pad, w1_pad, filt_b1, filt_w2[0], filt_b2, filt_w3[0], filt_b3,
                               filt_freq, filt_w4[0], deltas, cs)

    x2 = x.reshape(batch * seq, d)
    u, gate, yp, gates = _inproj(x2, g_norm, w_in[0].astype(BF16), b_in, conv_w[0], conv_b,
                                 pool_w[0].astype(BF16), pool_b, pool_scale, seq)
    yh = _hyena(u, gate, hyena_d, kre, kim, cs, batch, seq)
    out = _merge(yh, yp, gates, x2, w_hyena_out[0].astype(BF16), w_pool_out[0].astype(BF16),
                 w_out[0].astype(BF16), g_final[None, :])
    return out.reshape(batch, seq, d)
```

```python
import functools
import math

import jax
import jax.numpy as jnp
from jax import lax
from jax.experimental import pallas as pl
from jax.experimental.pallas import tpu as pltpu

FILTER_EMB = 33
FAST_DECAY_PCT = 0.3
SLOW_DECAY_PCT = 1.5
DECAY_TARGET = 1e-2
MAX_DECAY = math.log(DECAY_TARGET) / FAST_DECAY_PCT
MIN_DECAY = math.log(DECAY_TARGET) / SLOW_DECAY_PCT
POOL_WINDOWS = (2, 4, 8, 16)
NORM_EPS = 1e-6

LANES = 128
SUBLANES = 8
MXU_DIM = 256
HYENA_COLS = MXU_DIM
ROW_CHUNK = 1024
CONV_BLOCKS = 4
SPEC_ROWS = 64
VMEM_LIMIT = 56 * 1024 * 1024

F32 = jnp.float32
BF16 = jnp.bfloat16
SPEC_DTYPE = BF16


def _dot(a, b):
    return jnp.dot(a, b, preferred_element_type=F32)


def _dot_hi(a, b):
    return jnp.dot(a, b, preferred_element_type=F32, precision=lax.Precision.HIGHEST)


def _rms_scale(x):
    return x * lax.rsqrt(jnp.mean(x * x, axis=-1, keepdims=True) + NORM_EPS)


def _transform_matrices(lb):
    period = 8 * lb
    odd = 2 * jnp.arange(lb, dtype=jnp.int32) + 1
    ang = ((odd[:, None] * odd[None, :]) % period).astype(F32) * (2.0 * math.pi / period)
    return jnp.concatenate([jnp.cos(ang), jnp.sin(ang)], axis=1).astype(BF16)


def _filter_kernel(z_ref, w1_ref, b1_ref, w2_ref, b2_ref, w3_ref, b3_ref, fr_ref,
                   w4f_ref, w4b_ref, dl_ref, cs_ref, kre_ref, kim_ref,
                   h_ref, cphi_ref, sphi_ref, sgn_ref, kk_ref, xr_ref, xi_ref):
    seq = z_ref.shape[0]
    lb = cs_ref.shape[0]
    nb = seq // lb
    cb = dl_ref.shape[1]
    scale = 1.0 / lb

    @pl.when(pl.program_id(0) == 0)
    def _():
        fr = fr_ref[...]
        h = jnp.sin(fr * (_dot_hi(z_ref[...], w1_ref[...]) + b1_ref[...]))
        h = jnp.sin(fr * (_dot_hi(h, w2_ref[...]) + b2_ref[...]))
        h_ref[...] = jnp.sin(fr * (_dot_hi(h, w3_ref[...]) + b3_ref[...]))
        f_idx = lax.broadcasted_iota(jnp.int32, cphi_ref.shape, 0)
        phi = (f_idx.astype(F32) + 0.5) * (math.pi / (2 * lb))
        cphi_ref[...] = jnp.cos(phi) * scale
        sphi_ref[...] = jnp.sin(phi) * scale
        sgn_ref[...] = (1 - 2 * (f_idx & 1)).astype(F32)

    h = h_ref[...]
    decay = jnp.exp(-z_ref[:, 0:1] * dl_ref[...])
    taps = (_dot_hi(h, w4f_ref[...]) * decay, _dot_hi(h, w4b_ref[...]) * decay)
    taps = tuple(k.astype(BF16) for k in taps)
    for side, k in enumerate(taps):
        for s_ in range(nb):
            c0 = (side * nb + s_) * cb
            kk_ref[:, c0:c0 + cb] = k[s_ * lb:(s_ + 1) * lb]

    group = 2 * cb
    tile = lambda ref, width: jnp.concatenate([ref[...]] * (width // LANES), axis=1)
    c_phi, s_phi = tile(cphi_ref, group), tile(sphi_ref, group)
    for g in range(2 * nb * cb // group):
        cols = slice(g * group, (g + 1) * group)
        kk = kk_ref[:, cols]
        a = _dot(cs_ref[:, 0:lb], kk)
        b = _dot(cs_ref[:, lb:2 * lb], kk)
        xr_ref[:, cols] = c_phi * a + s_phi * b
        xi_ref[:, cols] = s_phi * a - c_phi * b

    sgn = tile(sgn_ref, cb)

    def seg(ref, side, s_):
        c0 = (side * nb + s_) * cb
        return ref[:, c0:c0 + cb]

    def causal_block(side, delta):
        r0 = (delta - 1) * lb
        first = taps[side][r0:r0 + 1].astype(F32) * scale
        re = seg(xr_ref, side, delta) - sgn * seg(xi_ref, side, delta - 1)
        im = seg(xi_ref, side, delta) + sgn * (seg(xr_ref, side, delta - 1) - first)
        return re, im

    mid = nb - 1
    out = kre_ref.dtype
    kre_ref[mid] = (seg(xr_ref, 0, 0) + seg(xr_ref, 1, 0)).astype(out)
    kim_ref[mid] = (seg(xi_ref, 0, 0) - seg(xi_ref, 1, 0)).astype(out)
    for delta in range(1, nb):
        re, im = causal_block(0, delta)
        kre_ref[mid + delta] = re.astype(out)
        kim_ref[mid + delta] = im.astype(out)
        re, im = causal_block(1, delta)
        kre_ref[mid - delta] = re.astype(out)
        kim_ref[mid - delta] = (-im).astype(out)


def _filter_spectra(z_pad, w1_pad, b1, w2, b2, w3, b3, freq, w4, deltas, cs):
    seq = z_pad.shape[0]
    lb = cs.shape[0]
    nb = seq // lb
    hf = w2.shape[0]
    dh = deltas.shape[1]
    cb = MXU_DIM
    nblk = dh // cb
    full = lambda a: pl.BlockSpec(a.shape, lambda c: (0,) * a.ndim)
    return pl.pallas_call(
        _filter_kernel,
        grid=(nblk,),
        in_specs=[
            full(z_pad), full(w1_pad), full(b1), full(w2), full(b2), full(w3), full(b3), full(freq),
            pl.BlockSpec((hf, cb), lambda c: (0, c)),
            pl.BlockSpec((hf, cb), lambda c: (0, nblk + c)),
            pl.BlockSpec((1, cb), lambda c: (0, c)),
            full(cs),
        ],
        out_specs=[pl.BlockSpec((2 * nb - 1, lb, cb), lambda c: (0, 0, c))] * 2,
        out_shape=[jax.ShapeDtypeStruct((2 * nb - 1, lb, dh), SPEC_DTYPE)] * 2,
        scratch_shapes=[
            pltpu.VMEM((seq, hf), F32),
            pltpu.VMEM((lb, LANES), F32),
            pltpu.VMEM((lb, LANES), F32),
            pltpu.VMEM((lb, LANES), F32),
            pltpu.VMEM((lb, 2 * nb * cb), BF16),
            pltpu.VMEM((lb, 2 * nb * cb), F32),
            pltpu.VMEM((lb, 2 * nb * cb), F32),
        ],
        compiler_params=pltpu.CompilerParams(
            dimension_semantics=("arbitrary",), vmem_limit_bytes=VMEM_LIMIT),
        name="filter_spectra",
    )(z_pad, w1_pad, b1, w2, b2, w3, b3, freq, w4, w4, deltas, cs)


def _conv3(p, w, bias, lo, n):
    n_rows = p.shape[0]
    assert 1 <= lo and lo + n < n_rows
    dn = pltpu.roll(p, 1, 0)
    up = pltpu.roll(p, n_rows - 1, 0)
    out = dn * w[0:1] + p * w[1:2] + up * w[2:3] + bias
    return out[lo:lo + n]


def _window_mean_minus_self(p, win, lo, n, at_start, at_end):
    n_rows = p.shape[0]
    half = win // 2
    sl = SUBLANES
    assert lo >= win and lo + n + win <= n_rows and half <= sl
    s = p
    w = 1
    while w < win:
        s = s + pltpu.roll(s, w, 0)
        w *= 2
    if half > 1:
        s = pltpu.roll(s, n_rows - (half - 1), 0)
    total = s[lo:lo + n]
    edge = lax.broadcasted_iota(jnp.int32, (sl, p.shape[1]), 0)
    cnt_first = jnp.where(at_start > 0, jnp.minimum(edge + half, win), win).astype(F32)
    cnt_last = jnp.where(at_end > 0, jnp.minimum(sl - edge + half, win), win).astype(F32)
    mean = jnp.concatenate([total[0:sl] / cnt_first,
                            total[sl:n - sl] * (1.0 / win),
                            total[n - sl:] / cnt_last], axis=0)
    return mean - p[lo:lo + n]


def _inproj_kernel(xp_ref, x_ref, xn_ref, g_ref, w_ref, b_ref, cw_ref, cb_ref,
                   pw_ref, pb_ref, ps_ref, u_ref, gate_ref, yp_ref, p_ref, *, chunks_per_seq):
    i = pl.program_id(0)
    rows = x_ref.shape[0]
    halo = xp_ref.shape[0]
    dh = u_ref.shape[1]
    cb = MXU_DIM
    keep_prev = (i % chunks_per_seq != 0).astype(F32)
    keep_next = (i % chunks_per_seq != chunks_per_seq - 1).astype(F32)

    xw = jnp.concatenate([xp_ref[...], x_ref[...], xn_ref[...]], axis=0)
    h = (_rms_scale(xw) * g_ref[...]).astype(BF16)

    h_main = h[halo:halo + rows]

    def project(rows_h, c0):
        return _dot(rows_h, w_ref[:, c0:c0 + cb]) + b_ref[:, c0:c0 + cb]

    def conv_input(c0):
        p = project(h, c0)
        return jnp.concatenate(
            [p[0:halo] * keep_prev, p[halo:halo + rows], p[halo + rows:] * keep_next], axis=0)

    def conv(c0):
        return _conv3(conv_input(c0), cw_ref[:, c0:c0 + cb], cb_ref[:, c0:c0 + cb], halo, rows)

    dp = yp_ref.shape[1]
    assert dp == len(POOL_WINDOWS) * cb
    for g, win in enumerate(POOL_WINDOWS):
        lanes = slice(g * cb, (g + 1) * cb)
        pooled = _window_mean_minus_self(conv_input(4 * dh + g * cb), win, halo, rows,
                                         1.0 - keep_prev, 1.0 - keep_next)
        y = _dot(pooled.astype(BF16), pw_ref[g]) + pb_ref[:, lanes]
        z = project(h_main, 4 * dh + dp + g * cb)
        yp_ref[:, lanes] = (y * ps_ref[:, lanes] * jax.nn.silu(z)).astype(BF16)

    for c in range(dh // cb):
        lanes = slice(c * cb, (c + 1) * cb)
        u_ref[:, lanes] = conv(dh + c * cb) * conv(2 * dh + c * cb)
        z = project(h_main, 3 * dh + c * cb)
        gate_ref[:, lanes] = (conv(c * cb) * jax.nn.silu(z)).astype(BF16)

    for c0 in range(0, p_ref.shape[1], cb):
        p_ref[:, c0:c0 + cb] = project(h_main, 4 * dh + 2 * dp + c0).astype(BF16)


def _inproj(x2, g_norm, w_in_bf, b_in, conv_w, conv_b, pool_w_bf, pool_b, pool_scale, seq):
    n_tok, d = x2.shape
    cols = w_in_bf.shape[1]
    dh = conv_b.shape[1] // 3
    dp = pool_b.shape[1]
    n_gate = cols - 4 * dh - 2 * dp
    rows = ROW_CHUNK
    halo = 2 * SUBLANES
    per_halo = rows // halo
    last_halo = n_tok // halo - 1
    full = lambda a: pl.BlockSpec(a.shape, lambda i: (0,) * a.ndim)
    tile = lambda width: pl.BlockSpec((rows, width), lambda i: (i, 0))
    return pl.pallas_call(
        functools.partial(_inproj_kernel, chunks_per_seq=seq // rows),
        grid=(n_tok // rows,),
        in_specs=[
            pl.BlockSpec((halo, d), lambda i: (jnp.maximum(i * per_halo - 1, 0), 0)),
            tile(d),
            pl.BlockSpec((halo, d), lambda i: (jnp.minimum((i + 1) * per_halo, last_halo), 0)),
            full(g_norm),
            pl.BlockSpec(w_in_bf.shape, lambda i: (0, 0), pipeline_mode=pl.Buffered(1)),
            full(b_in), full(conv_w), full(conv_b),
            full(pool_w_bf), full(pool_b), full(pool_scale),
        ],
        out_specs=[tile(dh), tile(dh), tile(dp), tile(n_gate)],
        out_shape=[
            jax.ShapeDtypeStruct((n_tok, dh), F32),
            jax.ShapeDtypeStruct((n_tok, dh), BF16),
            jax.ShapeDtypeStruct((n_tok, dp), BF16),
            jax.ShapeDtypeStruct((n_tok, n_gate), BF16),
        ],
        compiler_params=pltpu.CompilerParams(
            dimension_semantics=("parallel",), vmem_limit_bytes=VMEM_LIMIT),
        name="inproj",
    )(x2, x2, x2, g_norm, w_in_bf, b_in, conv_w, conv_b, pool_w_bf, pool_b, pool_scale)


def _hyena_kernel(u_ref, gate_ref, d_ref, kre_ref, kim_ref, cs_ref, y_ref, av_ref, acc_ref, pq_ref):
    seq = u_ref.shape[0]
    lb = cs_ref.shape[0]
    nb = seq // lb
    mid = nb - 1
    blocks = [slice(b * lb, (b + 1) * lb) for b in range(nb)]
    chunks = [slice(r0, r0 + SPEC_ROWS) for r0 in range(0, lb, SPEC_ROWS)]

    def forward(b):
        ub = u_ref[blocks[b], :].astype(BF16)
        av_ref[2 * b] = _dot(cs_ref[:, 0:lb], ub).astype(av_ref.dtype)
        av_ref[2 * b + 1] = _dot(cs_ref[:, lb:2 * lb], ub).astype(av_ref.dtype)

    def spectrum(b, rows):
        return av_ref[2 * b, rows, :], av_ref[2 * b + 1, rows, :]

    def products(a, b, rows, ur, vi):
        kr = kre_ref[mid + a - b, rows, :]
        ki = kim_ref[mid + a - b, rows, :]
        return kr * ur + ki * vi, kr * vi - ki * ur

    def accumulate(b):
        for rows in chunks:
            ur, vi = spectrum(b, rows)
            for a in range(nb):
                tp, tq = products(a, b, rows, ur, vi)
                if b == 0:
                    acc_ref[2 * a, rows, :] = tp
                    acc_ref[2 * a + 1, rows, :] = tq
                else:
                    acc_ref[2 * a, rows, :] += tp
                    acc_ref[2 * a + 1, rows, :] += tq

    def finish(a):
        for rows in chunks:
            tp, tq = products(a, nb - 1, rows, *spectrum(nb - 1, rows))
            pq_ref[a, rows, :] = (acc_ref[2 * a, rows, :] + tp).astype(BF16)
            pq_ref[a, lb + rows.start:lb + rows.stop, :] = (
                acc_ref[2 * a + 1, rows, :] + tq).astype(BF16)

    def inverse(a):
        rows = blocks[a]
        y = _dot(cs_ref[...], pq_ref[a])
        y = y + u_ref[rows, :] * d_ref[...]
        y_ref[rows, :] = (y * gate_ref[rows, :].astype(F32)).astype(BF16)

    forward(0)
    for b in range(nb - 1):
        forward(b + 1)
        accumulate(b)
    for a in range(nb):
        finish(a)
        if a > 0:
            inverse(a - 1)
    inverse(nb - 1)


def _hyena(u, gate, hyena_d, kre, kim, cs, batch, seq):
    dh = hyena_d.shape[1]
    cw = HYENA_COLS
    lb = cs.shape[0]
    nb = seq // lb
    act = pl.BlockSpec((seq, cw), lambda c, b: (b, c))
    spectra = pl.BlockSpec((2 * nb - 1, lb, cw), lambda c, b: (0, 0, c),
                           pipeline_mode=pl.Buffered(1))
    return pl.pallas_call(
        _hyena_kernel,
        grid=(dh // cw, batch),
        in_specs=[
            act, act,
            pl.BlockSpec((1, cw), lambda c, b: (0, c)),
            spectra, spectra,
            pl.BlockSpec(cs.shape, lambda c, b: (0, 0)),
        ],
        out_specs=act,
        out_shape=jax.ShapeDtypeStruct((batch * seq, dh), BF16),
        scratch_shapes=[
            pltpu.VMEM((2 * nb, lb, cw), SPEC_DTYPE),
            pltpu.VMEM((2 * nb, lb, cw), SPEC_DTYPE),
            pltpu.VMEM((nb, 2 * lb, cw), BF16),
        ],
        compiler_params=pltpu.CompilerParams(
            dimension_semantics=("arbitrary", "arbitrary"), vmem_limit_bytes=VMEM_LIMIT),
        name="hyena",
    )(u, gate, hyena_d, kre, kim, cs)


def _merge_kernel(yh_ref, yp_ref, g0_ref, g1_ref, x_ref, wh_ref, wp_ref, wo_ref, gf_ref, o_ref):
    out_h = _dot(yh_ref[...], wh_ref[...])
    out_p = _dot(yp_ref[...], wp_ref[...])
    merged = (jax.nn.sigmoid(g0_ref[...].astype(F32)) * out_h
              + jax.nn.sigmoid(g1_ref[...].astype(F32)) * out_p)
    res = x_ref[...] + _dot(merged.astype(BF16), wo_ref[...])
    o_ref[...] = _rms_scale(res) * gf_ref[...]


def _merge(yh, yp, gates, x2, wh_bf, wp_bf, wo_bf, g_final, tm=1024):
    n_tok, d = x2.shape
    tile = lambda col: pl.BlockSpec((tm, d), lambda i: (i, col))
    resident = lambda a: pl.BlockSpec(a.shape, lambda i: (0, 0), pipeline_mode=pl.Buffered(1))
    return pl.pallas_call(
        _merge_kernel,
        grid=(n_tok // tm,),
        in_specs=[
            tile(0), tile(0), tile(0), tile(1), tile(0),
            resident(wh_bf), resident(wp_bf), resident(wo_bf),
            pl.BlockSpec((1, d), lambda i: (0, 0)),
        ],
        out_specs=tile(0),
        out_shape=jax.ShapeDtypeStruct((n_tok, d), F32),
        compiler_params=pltpu.CompilerParams(
            dimension_semantics=("parallel",), vmem_limit_bytes=VMEM_LIMIT),
        name="merge",
    )(yh, yp, gates, gates, x2, wh_bf, wp_bf, wo_bf, g_final)


def _filter_features(seq):
    t = jnp.linspace(0.0, 1.0, seq, dtype=F32)[:, None]
    bands = (FILTER_EMB - 1) // 2
    w = 2.0 * math.pi * jnp.arange(seq, dtype=F32) / seq
    f = jnp.linspace(1e-4, bands - 1, bands, dtype=F32)
    ang = w[:, None] * f[None, :]
    return jnp.concatenate([t, jnp.cos(ang), -jnp.sin(ang)], axis=-1)


def kernel(x, g_norm, w_in, b_in, conv_w, conv_b, filt_w1, filt_b1, filt_w2, filt_b2, filt_w3, filt_b3, filt_w4, filt_freq, hyena_d, w_hyena_out, pool_w, pool_b, pool_scale, w_pool_out, w_out, g_final):
    batch, seq, d = x.shape
    dh = hyena_d.shape[1]
    dp = pool_b.shape[1]
    assert g_norm.shape[0] == 1, "single-layer block"
    assert d == dh == dp and conv_b.shape[1] == 3 * dh and w_in.shape[2] == 4 * dh + 2 * dp + 2 * d
    assert dh % HYENA_COLS == 0 and seq % ROW_CHUNK == 0
    assert seq % CONV_BLOCKS == 0 and (seq // CONV_BLOCKS) % SPEC_ROWS == 0

    cs = _transform_matrices(seq // CONV_BLOCKS)
    z_pad = jnp.pad(_filter_features(seq), ((0, 0), (0, LANES - FILTER_EMB)))
    w1_pad = jnp.pad(filt_w1[0], ((0, LANES - FILTER_EMB), (0, 0)))
    deltas = jnp.abs(jnp.linspace(MIN_DECAY, MAX_DECAY, dh, dtype=F32))[None, :]
    kre, kim = _filter_spectra(z_pad, w1_pad, filt_b1, filt_w2[0], filt_b2, filt_w3[0], filt_b3,
                               filt_freq, filt_w4[0], deltas, cs)

    x2 = x.reshape(batch * seq, d)
    u, gate, yp, gates = _inproj(x2, g_norm, w_in[0].astype(BF16), b_in, conv_w[0], conv_b,
                                 pool_w[0].astype(BF16), pool_b, pool_scale, seq)
    yh = _hyena(u, gate, hyena_d, kre, kim, cs, batch, seq)
    out = _merge(yh, yp, gates, x2, w_hyena_out[0].astype(BF16), w_pool_out[0].astype(BF16),
                 w_out[0].astype(BF16), g_final[None, :])
    return out.reshape(batch, seq, d)
```

```python
import functools
import math

import jax
import jax.numpy as jnp
from jax import lax
from jax.experimental import pallas as pl
from jax.experimental.pallas import tpu as pltpu

FILTER_EMB = 33
FAST_DECAY_PCT = 0.3
SLOW_DECAY_PCT = 1.5
DECAY_TARGET = 1e-2
MAX_DECAY = math.log(DECAY_TARGET) / FAST_DECAY_PCT
MIN_DECAY = math.log(DECAY_TARGET) / SLOW_DECAY_PCT
POOL_WINDOWS = (2, 4, 8, 16)
NORM_EPS = 1e-6

LANES = 128
SUBLANES = 8
MXU_DIM = 256
HYENA_COLS = MXU_DIM
ROW_CHUNK = 1024
MLP_ROWS = 256
CONV_BLOCKS = 4
SPEC_ROWS = 64
VMEM_LIMIT = 56 * 1024 * 1024

F32 = jnp.float32
BF16 = jnp.bfloat16
SPEC_DTYPE = BF16


def _dot(a, b):
    return jnp.dot(a, b, preferred_element_type=F32)


def _dot_hi(a, b):
    return jnp.dot(a, b, preferred_element_type=F32, precision=lax.Precision.HIGHEST)


def _rms_scale(x):
    return x * lax.rsqrt(jnp.mean(x * x, axis=-1, keepdims=True) + NORM_EPS)


def _transform_matrices(lb):
    period = 8 * lb
    odd = 2 * jnp.arange(lb, dtype=jnp.int32) + 1
    ang = ((odd[:, None] * odd[None, :]) % period).astype(F32) * (2.0 * math.pi / period)
    return jnp.concatenate([jnp.cos(ang), jnp.sin(ang)], axis=1).astype(BF16)


def _filter_kernel(z_ref, w1_ref, b1_ref, w2_ref, b2_ref, w3_ref, b3_ref, fr_ref,
                   w4f_ref, w4b_ref, dl_ref, cs_ref, kre_ref, kim_ref,
                   h_ref, cphi_ref, sphi_ref, sgn_ref, kk_ref, xr_ref, xi_ref):
    seq = z_ref.shape[0]
    lb = cs_ref.shape[0]
    nb = seq // lb
    cb = dl_ref.shape[1]
    scale = 1.0 / lb

    @pl.when(pl.program_id(0) == 0)
    def _():
        fr = fr_ref[...]

        def mlp(i, carry):
            rows = pl.ds(pl.multiple_of(i * MLP_ROWS, MLP_ROWS), MLP_ROWS)
            h = jnp.sin(fr * (_dot_hi(z_ref[rows, :], w1_ref[...]) + b1_ref[...]))
            h = jnp.sin(fr * (_dot_hi(h, w2_ref[...]) + b2_ref[...]))
            h_ref[rows, :] = jnp.sin(fr * (_dot_hi(h, w3_ref[...]) + b3_ref[...]))
            return carry

        lax.fori_loop(0, seq // MLP_ROWS, mlp, 0)
        f_idx = lax.broadcasted_iota(jnp.int32, cphi_ref.shape, 0)
        phi = (f_idx.astype(F32) + 0.5) * (math.pi / (2 * lb))
        cphi_ref[...] = jnp.cos(phi) * scale
        sphi_ref[...] = jnp.sin(phi) * scale
        sgn_ref[...] = (1 - 2 * (f_idx & 1)).astype(F32)

    h = h_ref[...]
    decay = jnp.exp(-z_ref[:, 0:1] * dl_ref[...])
    taps = (_dot_hi(h, w4f_ref[...]) * decay, _dot_hi(h, w4b_ref[...]) * decay)
    taps = tuple(k.astype(BF16) for k in taps)
    for side, k in enumerate(taps):
        for s_ in range(nb):
            c0 = (side * nb + s_) * cb
            kk_ref[:, c0:c0 + cb] = k[s_ * lb:(s_ + 1) * lb]

    group = 2 * cb
    tile = lambda ref, width: jnp.concatenate([ref[...]] * (width // LANES), axis=1)
    c_phi, s_phi = tile(cphi_ref, group), tile(sphi_ref, group)
    for g in range(2 * nb * cb // group):
        cols = slice(g * group, (g + 1) * group)
        kk = kk_ref[:, cols]
        a = _dot(cs_ref[:, 0:lb], kk)
        b = _dot(cs_ref[:, lb:2 * lb], kk)
        xr_ref[:, cols] = c_phi * a + s_phi * b
        xi_ref[:, cols] = s_phi * a - c_phi * b

    sgn = tile(sgn_ref, cb)

    def seg(ref, side, s_):
        c0 = (side * nb + s_) * cb
        return ref[:, c0:c0 + cb]

    def causal_block(side, delta):
        r0 = (delta - 1) * lb
        first = taps[side][r0:r0 + 1].astype(F32) * scale
        re = seg(xr_ref, side, delta) - sgn * seg(xi_ref, side, delta - 1)
        im = seg(xi_ref, side, delta) + sgn * (seg(xr_ref, side, delta - 1) - first)
        return re, im

    mid = nb - 1
    out = kre_ref.dtype
    kre_ref[mid] = (seg(xr_ref, 0, 0) + seg(xr_ref, 1, 0)).astype(out)
    kim_ref[mid] = (seg(xi_ref, 0, 0) - seg(xi_ref, 1, 0)).astype(out)
    for delta in range(1, nb):
        re, im = causal_block(0, delta)
        kre_ref[mid + delta] = re.astype(out)
        kim_ref[mid + delta] = im.astype(out)
        re, im = causal_block(1, delta)
        kre_ref[mid - delta] = re.astype(out)
        kim_ref[mid - delta] = (-im).astype(out)


def _filter_spectra(z_pad, w1_pad, b1, w2, b2, w3, b3, freq, w4, deltas, cs):
    seq = z_pad.shape[0]
    lb = cs.shape[0]
    nb = seq // lb
    hf = w2.shape[0]
    dh = deltas.shape[1]
    cb = MXU_DIM
    nblk = dh // cb
    full = lambda a: pl.BlockSpec(a.shape, lambda c: (0,) * a.ndim)
    return pl.pallas_call(
        _filter_kernel,
        grid=(nblk,),
        in_specs=[
            full(z_pad), full(w1_pad), full(b1), full(w2), full(b2), full(w3), full(b3), full(freq),
            pl.BlockSpec((hf, cb), lambda c: (0, c)),
            pl.BlockSpec((hf, cb), lambda c: (0, nblk + c)),
            pl.BlockSpec((1, cb), lambda c: (0, c)),
            full(cs),
        ],
        out_specs=[pl.BlockSpec((2 * nb - 1, lb, cb), lambda c: (0, 0, c))] * 2,
        out_shape=[jax.ShapeDtypeStruct((2 * nb - 1, lb, dh), SPEC_DTYPE)] * 2,
        scratch_shapes=[
            pltpu.VMEM((seq, hf), F32),
            pltpu.VMEM((lb, LANES), F32),
            pltpu.VMEM((lb, LANES), F32),
            pltpu.VMEM((lb, LANES), F32),
            pltpu.VMEM((lb, 2 * nb * cb), BF16),
            pltpu.VMEM((lb, 2 * nb * cb), F32),
            pltpu.VMEM((lb, 2 * nb * cb), F32),
        ],
        compiler_params=pltpu.CompilerParams(
            dimension_semantics=("arbitrary",), vmem_limit_bytes=VMEM_LIMIT),
        name="filter_spectra",
    )(z_pad, w1_pad, b1, w2, b2, w3, b3, freq, w4, w4, deltas, cs)


def _conv3(p, w, bias, lo, n):
    n_rows = p.shape[0]
    assert 1 <= lo and lo + n < n_rows
    dn = pltpu.roll(p, 1, 0)
    up = pltpu.roll(p, n_rows - 1, 0)
    out = dn * w[0:1] + p * w[1:2] + up * w[2:3] + bias
    return out[lo:lo + n]


def _window_mean_minus_self(p, win, lo, n, at_start, at_end):
    n_rows = p.shape[0]
    half = win // 2
    sl = SUBLANES
    assert lo >= win and lo + n + win <= n_rows and half <= sl
    s = p
    w = 1
    while w < win:
        s = s + pltpu.roll(s, w, 0)
        w *= 2
    if half > 1:
        s = pltpu.roll(s, n_rows - (half - 1), 0)
    total = s[lo:lo + n]
    edge = lax.broadcasted_iota(jnp.int32, (sl, p.shape[1]), 0)
    cnt_first = jnp.where(at_start > 0, jnp.minimum(edge + half, win), win).astype(F32)
    cnt_last = jnp.where(at_end > 0, jnp.minimum(sl - edge + half, win), win).astype(F32)
    mean = jnp.concatenate([total[0:sl] / cnt_first,
                            total[sl:n - sl] * (1.0 / win),
                            total[n - sl:] / cnt_last], axis=0)
    return mean - p[lo:lo + n]


def _inproj_kernel(xp_ref, x_ref, xn_ref, g_ref, w_ref, b_ref, cw_ref, cb_ref,
                   pw_ref, pb_ref, ps_ref, u_ref, gate_ref, yp_ref, p_ref, *, chunks_per_seq):
    i = pl.program_id(0)
    rows = x_ref.shape[0]
    halo = xp_ref.shape[0]
    dh = u_ref.shape[1]
    cb = MXU_DIM
    keep_prev = (i % chunks_per_seq != 0).astype(F32)
    keep_next = (i % chunks_per_seq != chunks_per_seq - 1).astype(F32)

    xw = jnp.concatenate([xp_ref[...], x_ref[...], xn_ref[...]], axis=0)
    h = (_rms_scale(xw) * g_ref[...]).astype(BF16)

    h_main = h[halo:halo + rows]

    def project(rows_h, c0):
        return _dot(rows_h, w_ref[:, c0:c0 + cb]) + b_ref[:, c0:c0 + cb]

    def conv_input(c0):
        p = project(h, c0)
        return jnp.concatenate(
            [p[0:halo] * keep_prev, p[halo:halo + rows], p[halo + rows:] * keep_next], axis=0)

    def conv(c0):
        return _conv3(conv_input(c0), cw_ref[:, c0:c0 + cb], cb_ref[:, c0:c0 + cb], halo, rows)

    dp = yp_ref.shape[1]
    assert dp == len(POOL_WINDOWS) * cb
    for g, win in enumerate(POOL_WINDOWS):
        lanes = slice(g * cb, (g + 1) * cb)
        pooled = _window_mean_minus_self(conv_input(4 * dh + g * cb), win, halo, rows,
                                         1.0 - keep_prev, 1.0 - keep_next)
        y = _dot(pooled.astype(BF16), pw_ref[g]) + pb_ref[:, lanes]
        z = project(h_main, 4 * dh + dp + g * cb)
        yp_ref[:, lanes] = (y * ps_ref[:, lanes] * jax.nn.silu(z)).astype(BF16)

    for c in range(dh // cb):
        lanes = slice(c * cb, (c + 1) * cb)
        u_ref[:, lanes] = conv(dh + c * cb) * conv(2 * dh + c * cb)
        z = project(h_main, 3 * dh + c * cb)
        gate_ref[:, lanes] = (conv(c * cb) * jax.nn.silu(z)).astype(BF16)

    for c0 in range(0, p_ref.shape[1], cb):
        p_ref[:, c0:c0 + cb] = project(h_main, 4 * dh + 2 * dp + c0).astype(BF16)


def _inproj(x2, g_norm, w_in_bf, b_in, conv_w, conv_b, pool_w_bf, pool_b, pool_scale, seq):
    n_tok, d = x2.shape
    cols = w_in_bf.shape[1]
    dh = conv_b.shape[1] // 3
    dp = pool_b.shape[1]
    n_gate = cols - 4 * dh - 2 * dp
    rows = ROW_CHUNK
    halo = 2 * SUBLANES
    per_halo = rows // halo
    last_halo = n_tok // halo - 1
    full = lambda a: pl.BlockSpec(a.shape, lambda i: (0,) * a.ndim)
    tile = lambda width: pl.BlockSpec((rows, width), lambda i: (i, 0))
    return pl.pallas_call(
        functools.partial(_inproj_kernel, chunks_per_seq=seq // rows),
        grid=(n_tok // rows,),
        in_specs=[
            pl.BlockSpec((halo, d), lambda i: (jnp.maximum(i * per_halo - 1, 0), 0)),
            tile(d),
            pl.BlockSpec((halo, d), lambda i: (jnp.minimum((i + 1) * per_halo, last_halo), 0)),
            full(g_norm),
            pl.BlockSpec(w_in_bf.shape, lambda i: (0, 0), pipeline_mode=pl.Buffered(1)),
            full(b_in), full(conv_w), full(conv_b),
            full(pool_w_bf), full(pool_b), full(pool_scale),
        ],
        out_specs=[tile(dh), tile(dh), tile(dp), tile(n_gate)],
        out_shape=[
            jax.ShapeDtypeStruct((n_tok, dh), F32),
            jax.ShapeDtypeStruct((n_tok, dh), BF16),
            jax.ShapeDtypeStruct((n_tok, dp), BF16),
            jax.ShapeDtypeStruct((n_tok, n_gate), BF16),
        ],
        compiler_params=pltpu.CompilerParams(
            dimension_semantics=("parallel",), vmem_limit_bytes=VMEM_LIMIT),
        name="inproj",
    )(x2, x2, x2, g_norm, w_in_bf, b_in, conv_w, conv_b, pool_w_bf, pool_b, pool_scale)


def _hyena_kernel(u_ref, gate_ref, d_ref, kre_ref, kim_ref, cs_ref, y_ref, av_ref, acc_ref, pq_ref):
    seq = u_ref.shape[0]
    lb = cs_ref.shape[0]
    nb = seq // lb
    mid = nb - 1
    blocks = [slice(b * lb, (b + 1) * lb) for b in range(nb)]
    chunks = [slice(r0, r0 + SPEC_ROWS) for r0 in range(0, lb, SPEC_ROWS)]

    def forward(b):
        ub = u_ref[blocks[b], :].astype(BF16)
        av_ref[2 * b] = _dot(cs_ref[:, 0:lb], ub).astype(av_ref.dtype)
        av_ref[2 * b + 1] = _dot(cs_ref[:, lb:2 * lb], ub).astype(av_ref.dtype)

    def spectrum(b, rows):
        return av_ref[2 * b, rows, :], av_ref[2 * b + 1, rows, :]

    def products(a, b, rows, ur, vi):
        kr = kre_ref[mid + a - b, rows, :]
        ki = kim_ref[mid + a - b, rows, :]
        return kr * ur + ki * vi, kr * vi - ki * ur

    def accumulate(b):
        for rows in chunks:
            ur, vi = spectrum(b, rows)
            for a in range(nb):
                tp, tq = products(a, b, rows, ur, vi)
                if b == 0:
                    acc_ref[2 * a, rows, :] = tp
                    acc_ref[2 * a + 1, rows, :] = tq
                else:
                    acc_ref[2 * a, rows, :] += tp
                    acc_ref[2 * a + 1, rows, :] += tq

    def finish(a):
        for rows in chunks:
            tp, tq = products(a, nb - 1, rows, *spectrum(nb - 1, rows))
            pq_ref[a, rows, :] = (acc_ref[2 * a, rows, :] + tp).astype(BF16)
            pq_ref[a, lb + rows.start:lb + rows.stop, :] = (
                acc_ref[2 * a + 1, rows, :] + tq).astype(BF16)

    def inverse(a):
        rows = blocks[a]
        y = _dot(cs_ref[...], pq_ref[a])
        y = y + u_ref[rows, :] * d_ref[...]
        y_ref[rows, :] = (y * gate_ref[rows, :].astype(F32)).astype(BF16)

    forward(0)
    for b in range(nb - 1):
        forward(b + 1)
        accumulate(b)
    for a in range(nb):
        finish(a)
        if a > 0:
            inverse(a - 1)
    inverse(nb - 1)


def _hyena(u, gate, hyena_d, kre, kim, cs, batch, seq):
    dh = hyena_d.shape[1]
    cw = HYENA_COLS
    lb = cs.shape[0]
    nb = seq // lb
    act = pl.BlockSpec((seq, cw), lambda c, b: (b, c))
    spectra = pl.BlockSpec((2 * nb - 1, lb, cw), lambda c, b: (0, 0, c),
                           pipeline_mode=pl.Buffered(1))
    return pl.pallas_call(
        _hyena_kernel,
        grid=(dh // cw, batch),
        in_specs=[
            act, act,
            pl.BlockSpec((1, cw), lambda c, b: (0, c)),
            spectra, spectra,
            pl.BlockSpec(cs.shape, lambda c, b: (0, 0)),
        ],
        out_specs=act,
        out_shape=jax.ShapeDtypeStruct((batch * seq, dh), BF16),
        scratch_shapes=[
            pltpu.VMEM((2 * nb, lb, cw), SPEC_DTYPE),
            pltpu.VMEM((2 * nb, lb, cw), SPEC_DTYPE),
            pltpu.VMEM((nb, 2 * lb, cw), BF16),
        ],
        compiler_params=pltpu.CompilerParams(
            dimension_semantics=("arbitrary", "arbitrary"), vmem_limit_bytes=VMEM_LIMIT),
        name="hyena",
    )(u, gate, hyena_d, kre, kim, cs)


def _merge_kernel(yh_ref, yp_ref, g0_ref, g1_ref, x_ref, wh_ref, wp_ref, wo_ref, gf_ref, o_ref):
    out_h = _dot(yh_ref[...], wh_ref[...])
    out_p = _dot(yp_ref[...], wp_ref[...])
    merged = (jax.nn.sigmoid(g0_ref[...].astype(F32)) * out_h
              + jax.nn.sigmoid(g1_ref[...].astype(F32)) * out_p)
    res = x_ref[...] + _dot(merged.astype(BF16), wo_ref[...])
    o_ref[...] = _rms_scale(res) * gf_ref[...]


def _merge(yh, yp, gates, x2, wh_bf, wp_bf, wo_bf, g_final, tm=1024):
    n_tok, d = x2.shape
    tile = lambda col: pl.BlockSpec((tm, d), lambda i: (i, col))
    resident = lambda a: pl.BlockSpec(a.shape, lambda i: (0, 0), pipeline_mode=pl.Buffered(1))
    return pl.pallas_call(
        _merge_kernel,
        grid=(n_tok // tm,),
        in_specs=[
            tile(0), tile(0), tile(0), tile(1), tile(0),
            resident(wh_bf), resident(wp_bf), resident(wo_bf),
            pl.BlockSpec((1, d), lambda i: (0, 0)),
        ],
        out_specs=tile(0),
        out_shape=jax.ShapeDtypeStruct((n_tok, d), F32),
        compiler_params=pltpu.CompilerParams(
            dimension_semantics=("parallel",), vmem_limit_bytes=VMEM_LIMIT),
        name="merge",
    )(yh, yp, gates, gates, x2, wh_bf, wp_bf, wo_bf, g_final)


def _filter_features(seq):
    t = jnp.linspace(0.0, 1.0, seq, dtype=F32)[:, None]
    bands = (FILTER_EMB - 1) // 2
    w = 2.0 * math.pi * jnp.arange(seq, dtype=F32) / seq
    f = jnp.linspace(1e-4, bands - 1, bands, dtype=F32)
    ang = w[:, None] * f[None, :]
    return jnp.concatenate([t, jnp.cos(ang), -jnp.sin(ang)], axis=-1)


def kernel(x, g_norm, w_in, b_in, conv_w, conv_b, filt_w1, filt_b1, filt_w2, filt_b2, filt_w3, filt_b3, filt_w4, filt_freq, hyena_d, w_hyena_out, pool_w, pool_b, pool_scale, w_pool_out, w_out, g_final):
    batch, seq, d = x.shape
    dh = hyena_d.shape[1]
    dp = pool_b.shape[1]
    assert g_norm.shape[0] == 1, "single-layer block"
    assert d == dh == dp and conv_b.shape[1] == 3 * dh and w_in.shape[2] == 4 * dh + 2 * dp + 2 * d
    assert dh % HYENA_COLS == 0 and seq % ROW_CHUNK == 0
    assert seq % CONV_BLOCKS == 0 and (seq // CONV_BLOCKS) % SPEC_ROWS == 0

    cs = _transform_matrices(seq // CONV_BLOCKS)
    z_pad = jnp.pad(_filter_features(seq), ((0, 0), (0, LANES - FILTER_EMB)))
    w1_pad = jnp.pad(filt_w1[0], ((0, LANES - FILTER_EMB), (0, 0)))
    deltas = jnp.abs(jnp.linspace(MIN_DECAY, MAX_DECAY, dh, dtype=F32))[None, :]
    kre, kim = _filter_spectra(z_pad, w1_pad, filt_b1, filt_w2[0], filt_b2, filt_w3[0], filt_b3,
                               filt_freq, filt_w4[0], deltas, cs)

    x2 = x.reshape(batch * seq, d)
    u, gate, yp, gates = _inproj(x2, g_norm, w_in[0].astype(BF16), b_in, conv_w[0], conv_b,
                                 pool_w[0].astype(BF16), pool_b, pool_scale, seq)
    yh = _hyena(u, gate, hyena_d, kre, kim, cs, batch, seq)
    out = _merge(yh, yp, gates, x2, w_hyena_out[0].astype(BF16), w_pool_out[0].astype(BF16),
                 w_out[0].astype(BF16), g_final[None, :])
    return out.reshape(batch, seq, d)
```

```python
import functools
import math

import jax
import jax.numpy as jnp
from jax import lax
from jax.experimental import pallas as pl
from jax.experimental.pallas import tpu as pltpu

FILTER_EMB = 33
FAST_DECAY_PCT = 0.3
SLOW_DECAY_PCT = 1.5
DECAY_TARGET = 1e-2
MAX_DECAY = math.log(DECAY_TARGET) / FAST_DECAY_PCT
MIN_DECAY = math.log(DECAY_TARGET) / SLOW_DECAY_PCT
POOL_WINDOWS = (2, 4, 8, 16)
NORM_EPS = 1e-6

LANES = 128
SUBLANES = 8
MXU_DIM = 256
HYENA_COLS = MXU_DIM
ROW_CHUNK = 1024
MLP_ROWS = 256
CONV_BLOCKS = 4
SPEC_ROWS = 64
VMEM_LIMIT = 56 * 1024 * 1024

F32 = jnp.float32
BF16 = jnp.bfloat16
SPEC_DTYPE = BF16


def _dot(a, b):
    return jnp.dot(a, b, preferred_element_type=F32)


def _dot_hi(a, b):
    return jnp.dot(a, b, preferred_element_type=F32, precision=lax.Precision.HIGHEST)


def _dot_split(a, b):
    a_hi, b_hi = a.astype(BF16), b.astype(BF16)
    a_lo = (a - a_hi.astype(F32)).astype(BF16)
    b_lo = (b - b_hi.astype(F32)).astype(BF16)
    return _dot(a_hi, b_hi) + _dot(a_lo, b_hi) + _dot(a_hi, b_lo)


def _rms_scale(x):
    return x * lax.rsqrt(jnp.mean(x * x, axis=-1, keepdims=True) + NORM_EPS)


def _transform_matrices(lb):
    period = 8 * lb
    odd = 2 * jnp.arange(lb, dtype=jnp.int32) + 1
    ang = ((odd[:, None] * odd[None, :]) % period).astype(F32) * (2.0 * math.pi / period)
    return jnp.concatenate([jnp.cos(ang), jnp.sin(ang)], axis=1).astype(BF16)


def _filter_kernel(z_ref, w1_ref, b1_ref, w2_ref, b2_ref, w3_ref, b3_ref, fr_ref,
                   w4f_ref, w4b_ref, dl_ref, cs_ref, kre_ref, kim_ref,
                   h_ref, cphi_ref, sphi_ref, sgn_ref, kk_ref, xr_ref, xi_ref):
    seq = z_ref.shape[0]
    lb = cs_ref.shape[0]
    nb = seq // lb
    cb = dl_ref.shape[1]
    scale = 1.0 / lb

    @pl.when(pl.program_id(0) == 0)
    def _():
        fr = fr_ref[...]

        def mlp(i, carry):
            rows = pl.ds(pl.multiple_of(i * MLP_ROWS, MLP_ROWS), MLP_ROWS)
            h = jnp.sin(fr * (_dot_hi(z_ref[rows, :], w1_ref[...]) + b1_ref[...]))
            h = jnp.sin(fr * (_dot_hi(h, w2_ref[...]) + b2_ref[...]))
            h_ref[rows, :] = jnp.sin(fr * (_dot_hi(h, w3_ref[...]) + b3_ref[...]))
            return carry

        lax.fori_loop(0, seq // MLP_ROWS, mlp, 0)
        f_idx = lax.broadcasted_iota(jnp.int32, cphi_ref.shape, 0)
        phi = (f_idx.astype(F32) + 0.5) * (math.pi / (2 * lb))
        cphi_ref[...] = jnp.cos(phi) * scale
        sphi_ref[...] = jnp.sin(phi) * scale
        sgn_ref[...] = (1 - 2 * (f_idx & 1)).astype(F32)

    h = h_ref[...]
    decay = jnp.exp(-z_ref[:, 0:1] * dl_ref[...])
    taps = (_dot_split(h, w4f_ref[...]) * decay, _dot_split(h, w4b_ref[...]) * decay)
    taps = tuple(k.astype(BF16) for k in taps)
    for side, k in enumerate(taps):
        for s_ in range(nb):
            c0 = (side * nb + s_) * cb
            kk_ref[:, c0:c0 + cb] = k[s_ * lb:(s_ + 1) * lb]

    group = 2 * cb
    tile = lambda ref, width: jnp.concatenate([ref[...]] * (width // LANES), axis=1)
    c_phi, s_phi = tile(cphi_ref, group), tile(sphi_ref, group)
    for g in range(2 * nb * cb // group):
        cols = slice(g * group, (g + 1) * group)
        kk = kk_ref[:, cols]
        a = _dot(cs_ref[:, 0:lb], kk)
        b = _dot(cs_ref[:, lb:2 * lb], kk)
        xr_ref[:, cols] = c_phi * a + s_phi * b
        xi_ref[:, cols] = s_phi * a - c_phi * b

    sgn = tile(sgn_ref, cb)

    def seg(ref, side, s_):
        c0 = (side * nb + s_) * cb
        return ref[:, c0:c0 + cb]

    def causal_block(side, delta):
        r0 = (delta - 1) * lb
        first = taps[side][r0:r0 + 1].astype(F32) * scale
        re = seg(xr_ref, side, delta) - sgn * seg(xi_ref, side, delta - 1)
        im = seg(xi_ref, side, delta) + sgn * (seg(xr_ref, side, delta - 1) - first)
        return re, im

    mid = nb - 1
    out = kre_ref.dtype
    kre_ref[mid] = (seg(xr_ref, 0, 0) + seg(xr_ref, 1, 0)).astype(out)
    kim_ref[mid] = (seg(xi_ref, 0, 0) - seg(xi_ref, 1, 0)).astype(out)
    for delta in range(1, nb):
        re, im = causal_block(0, delta)
        kre_ref[mid + delta] = re.astype(out)
        kim_ref[mid + delta] = im.astype(out)
        re, im = causal_block(1, delta)
        kre_ref[mid - delta] = re.astype(out)
        kim_ref[mid - delta] = (-im).astype(out)


def _filter_spectra(z_pad, w1_pad, b1, w2, b2, w3, b3, freq, w4, deltas, cs):
    seq = z_pad.shape[0]
    lb = cs.shape[0]
    nb = seq // lb
    hf = w2.shape[0]
    dh = deltas.shape[1]
    cb = MXU_DIM
    nblk = dh // cb
    full = lambda a: pl.BlockSpec(a.shape, lambda c: (0,) * a.ndim)
    return pl.pallas_call(
        _filter_kernel,
        grid=(nblk,),
        in_specs=[
            full(z_pad), full(w1_pad), full(b1), full(w2), full(b2), full(w3), full(b3), full(freq),
            pl.BlockSpec((hf, cb), lambda c: (0, c)),
            pl.BlockSpec((hf, cb), lambda c: (0, nblk + c)),
            pl.BlockSpec((1, cb), lambda c: (0, c)),
            full(cs),
        ],
        out_specs=[pl.BlockSpec((2 * nb - 1, lb, cb), lambda c: (0, 0, c))] * 2,
        out_shape=[jax.ShapeDtypeStruct((2 * nb - 1, lb, dh), SPEC_DTYPE)] * 2,
        scratch_shapes=[
            pltpu.VMEM((seq, hf), F32),
            pltpu.VMEM((lb, LANES), F32),
            pltpu.VMEM((lb, LANES), F32),
            pltpu.VMEM((lb, LANES), F32),
            pltpu.VMEM((lb, 2 * nb * cb), BF16),
            pltpu.VMEM((lb, 2 * nb * cb), F32),
            pltpu.VMEM((lb, 2 * nb * cb), F32),
        ],
        compiler_params=pltpu.CompilerParams(
            dimension_semantics=("arbitrary",), vmem_limit_bytes=VMEM_LIMIT),
        name="filter_spectra",
    )(z_pad, w1_pad, b1, w2, b2, w3, b3, freq, w4, w4, deltas, cs)


def _conv3(p, w, bias, lo, n):
    n_rows = p.shape[0]
    assert 1 <= lo and lo + n < n_rows
    dn = pltpu.roll(p, 1, 0)
    up = pltpu.roll(p, n_rows - 1, 0)
    out = dn * w[0:1] + p * w[1:2] + up * w[2:3] + bias
    return out[lo:lo + n]


def _window_mean_minus_self(p, win, lo, n, at_start, at_end):
    n_rows = p.shape[0]
    half = win // 2
    sl = SUBLANES
    assert lo >= win and lo + n + win <= n_rows and half <= sl
    s = p
    w = 1
    while w < win:
        s = s + pltpu.roll(s, w, 0)
        w *= 2
    if half > 1:
        s = pltpu.roll(s, n_rows - (half - 1), 0)
    total = s[lo:lo + n]
    edge = lax.broadcasted_iota(jnp.int32, (sl, p.shape[1]), 0)
    cnt_first = jnp.where(at_start > 0, jnp.minimum(edge + half, win), win).astype(F32)
    cnt_last = jnp.where(at_end > 0, jnp.minimum(sl - edge + half, win), win).astype(F32)
    mean = jnp.concatenate([total[0:sl] / cnt_first,
                            total[sl:n - sl] * (1.0 / win),
                            total[n - sl:] / cnt_last], axis=0)
    return mean - p[lo:lo + n]


def _inproj_kernel(xp_ref, x_ref, xn_ref, g_ref, w_ref, b_ref, cw_ref, cb_ref,
                   pw_ref, pb_ref, ps_ref, u_ref, gate_ref, yp_ref, p_ref, *, chunks_per_seq):
    i = pl.program_id(0)
    rows = x_ref.shape[0]
    halo = xp_ref.shape[0]
    dh = u_ref.shape[1]
    cb = MXU_DIM
    keep_prev = (i % chunks_per_seq != 0).astype(F32)
    keep_next = (i % chunks_per_seq != chunks_per_seq - 1).astype(F32)

    xw = jnp.concatenate([xp_ref[...], x_ref[...], xn_ref[...]], axis=0)
    h = (_rms_scale(xw) * g_ref[...]).astype(BF16)

    h_main = h[halo:halo + rows]

    def project(rows_h, c0):
        return _dot(rows_h, w_ref[:, c0:c0 + cb]) + b_ref[:, c0:c0 + cb]

    def conv_input(c0):
        p = project(h, c0)
        return jnp.concatenate(
            [p[0:halo] * keep_prev, p[halo:halo + rows], p[halo + rows:] * keep_next], axis=0)

    def conv(c0):
        return _conv3(conv_input(c0), cw_ref[:, c0:c0 + cb], cb_ref[:, c0:c0 + cb], halo, rows)

    dp = yp_ref.shape[1]
    assert dp == len(POOL_WINDOWS) * cb
    for g, win in enumerate(POOL_WINDOWS):
        lanes = slice(g * cb, (g + 1) * cb)
        pooled = _window_mean_minus_self(conv_input(4 * dh + g * cb), win, halo, rows,
                                         1.0 - keep_prev, 1.0 - keep_next)
        y = _dot(pooled.astype(BF16), pw_ref[g]) + pb_ref[:, lanes]
        z = project(h_main, 4 * dh + dp + g * cb)
        yp_ref[:, lanes] = (y * ps_ref[:, lanes] * jax.nn.silu(z)).astype(BF16)

    for c in range(dh // cb):
        lanes = slice(c * cb, (c + 1) * cb)
        u_ref[:, lanes] = conv(dh + c * cb) * conv(2 * dh + c * cb)
        z = project(h_main, 3 * dh + c * cb)
        gate_ref[:, lanes] = (conv(c * cb) * jax.nn.silu(z)).astype(BF16)

    for c0 in range(0, p_ref.shape[1], cb):
        p_ref[:, c0:c0 + cb] = project(h_main, 4 * dh + 2 * dp + c0).astype(BF16)


def _inproj(x2, g_norm, w_in_bf, b_in, conv_w, conv_b, pool_w_bf, pool_b, pool_scale, seq):
    n_tok, d = x2.shape
    cols = w_in_bf.shape[1]
    dh = conv_b.shape[1] // 3
    dp = pool_b.shape[1]
    n_gate = cols - 4 * dh - 2 * dp
    rows = ROW_CHUNK
    halo = 2 * SUBLANES
    per_halo = rows // halo
    last_halo = n_tok // halo - 1
    full = lambda a: pl.BlockSpec(a.shape, lambda i: (0,) * a.ndim)
    tile = lambda width: pl.BlockSpec((rows, width), lambda i: (i, 0))
    return pl.pallas_call(
        functools.partial(_inproj_kernel, chunks_per_seq=seq // rows),
        grid=(n_tok // rows,),
        in_specs=[
            pl.BlockSpec((halo, d), lambda i: (jnp.maximum(i * per_halo - 1, 0), 0)),
            tile(d),
            pl.BlockSpec((halo, d), lambda i: (jnp.minimum((i + 1) * per_halo, last_halo), 0)),
            full(g_norm),
            pl.BlockSpec(w_in_bf.shape, lambda i: (0, 0), pipeline_mode=pl.Buffered(1)),
            full(b_in), full(conv_w), full(conv_b),
            full(pool_w_bf), full(pool_b), full(pool_scale),
        ],
        out_specs=[tile(dh), tile(dh), tile(dp), tile(n_gate)],
        out_shape=[
            jax.ShapeDtypeStruct((n_tok, dh), F32),
            jax.ShapeDtypeStruct((n_tok, dh), BF16),
            jax.ShapeDtypeStruct((n_tok, dp), BF16),
            jax.ShapeDtypeStruct((n_tok, n_gate), BF16),
        ],
        compiler_params=pltpu.CompilerParams(
            dimension_semantics=("parallel",), vmem_limit_bytes=VMEM_LIMIT),
        name="inproj",
    )(x2, x2, x2, g_norm, w_in_bf, b_in, conv_w, conv_b, pool_w_bf, pool_b, pool_scale)


def _hyena_kernel(u_ref, gate_ref, d_ref, kre_ref, kim_ref, cs_ref, y_ref, av_ref, acc_ref, pq_ref):
    seq = u_ref.shape[0]
    lb = cs_ref.shape[0]
    nb = seq // lb
    mid = nb - 1
    blocks = [slice(b * lb, (b + 1) * lb) for b in range(nb)]
    chunks = [slice(r0, r0 + SPEC_ROWS) for r0 in range(0, lb, SPEC_ROWS)]

    def forward(b):
        ub = u_ref[blocks[b], :].astype(BF16)
        av_ref[2 * b] = _dot(cs_ref[:, 0:lb], ub).astype(av_ref.dtype)
        av_ref[2 * b + 1] = _dot(cs_ref[:, lb:2 * lb], ub).astype(av_ref.dtype)

    def spectrum(b, rows):
        return av_ref[2 * b, rows, :], av_ref[2 * b + 1, rows, :]

    def products(a, b, rows, ur, vi):
        kr = kre_ref[mid + a - b, rows, :]
        ki = kim_ref[mid + a - b, rows, :]
        return kr * ur + ki * vi, kr * vi - ki * ur

    def accumulate(b):
        for rows in chunks:
            ur, vi = spectrum(b, rows)
            for a in range(nb):
                tp, tq = products(a, b, rows, ur, vi)
                if b == 0:
                    acc_ref[2 * a, rows, :] = tp
                    acc_ref[2 * a + 1, rows, :] = tq
                else:
                    acc_ref[2 * a, rows, :] += tp
                    acc_ref[2 * a + 1, rows, :] += tq

    def finish(a):
        for rows in chunks:
            tp, tq = products(a, nb - 1, rows, *spectrum(nb - 1, rows))
            pq_ref[a, rows, :] = (acc_ref[2 * a, rows, :] + tp).astype(BF16)
            pq_ref[a, lb + rows.start:lb + rows.stop, :] = (
                acc_ref[2 * a + 1, rows, :] + tq).astype(BF16)

    def inverse(a):
        rows = blocks[a]
        y = _dot(cs_ref[...], pq_ref[a])
        y = y + u_ref[rows, :] * d_ref[...]
        y_ref[rows, :] = (y * gate_ref[rows, :].astype(F32)).astype(BF16)

    forward(0)
    for b in range(nb - 1):
        forward(b + 1)
        accumulate(b)
    for a in range(nb):
        finish(a)
        if a > 0:
            inverse(a - 1)
    inverse(nb - 1)


def _hyena(u, gate, hyena_d, kre, kim, cs, batch, seq):
    dh = hyena_d.shape[1]
    cw = HYENA_COLS
    lb = cs.shape[0]
    nb = seq // lb
    act = pl.BlockSpec((seq, cw), lambda c, b: (b, c))
    spectra = pl.BlockSpec((2 * nb - 1, lb, cw), lambda c, b: (0, 0, c),
                           pipeline_mode=pl.Buffered(1))
    return pl.pallas_call(
        _hyena_kernel,
        grid=(dh // cw, batch),
        in_specs=[
            act, act,
            pl.BlockSpec((1, cw), lambda c, b: (0, c)),
            spectra, spectra,
            pl.BlockSpec(cs.shape, lambda c, b: (0, 0)),
        ],
        out_specs=act,
        out_shape=jax.ShapeDtypeStruct((batch * seq, dh), BF16),
        scratch_shapes=[
            pltpu.VMEM((2 * nb, lb, cw), SPEC_DTYPE),
            pltpu.VMEM((2 * nb, lb, cw), SPEC_DTYPE),
            pltpu.VMEM((nb, 2 * lb, cw), BF16),
        ],
        compiler_params=pltpu.CompilerParams(
            dimension_semantics=("arbitrary", "arbitrary"), vmem_limit_bytes=VMEM_LIMIT),
        name="hyena",
    )(u, gate, hyena_d, kre, kim, cs)


def _merge_kernel(yh_ref, yp_ref, g0_ref, g1_ref, x_ref, wh_ref, wp_ref, wo_ref, gf_ref, o_ref):
    out_h = _dot(yh_ref[...], wh_ref[...])
    out_p = _dot(yp_ref[...], wp_ref[...])
    merged = (jax.nn.sigmoid(g0_ref[...].astype(F32)) * out_h
              + jax.nn.sigmoid(g1_ref[...].astype(F32)) * out_p)
    res = x_ref[...] + _dot(merged.astype(BF16), wo_ref[...])
    o_ref[...] = _rms_scale(res) * gf_ref[...]


def _merge(yh, yp, gates, x2, wh_bf, wp_bf, wo_bf, g_final, tm=1024):
    n_tok, d = x2.shape
    tile = lambda col: pl.BlockSpec((tm, d), lambda i: (i, col))
    resident = lambda a: pl.BlockSpec(a.shape, lambda i: (0, 0), pipeline_mode=pl.Buffered(1))
    return pl.pallas_call(
        _merge_kernel,
        grid=(n_tok // tm,),
        in_specs=[
            tile(0), tile(0), tile(0), tile(1), tile(0),
            resident(wh_bf), resident(wp_bf), resident(wo_bf),
            pl.BlockSpec((1, d), lambda i: (0, 0)),
        ],
        out_specs=tile(0),
        out_shape=jax.ShapeDtypeStruct((n_tok, d), F32),
        compiler_params=pltpu.CompilerParams(
            dimension_semantics=("parallel",), vmem_limit_bytes=VMEM_LIMIT),
        name="merge",
    )(yh, yp, gates, gates, x2, wh_bf, wp_bf, wo_bf, g_final)


def _filter_features(seq):
    t = jnp.linspace(0.0, 1.0, seq, dtype=F32)[:, None]
    bands = (FILTER_EMB - 1) // 2
    w = 2.0 * math.pi * jnp.arange(seq, dtype=F32) / seq
    f = jnp.linspace(1e-4, bands - 1, bands, dtype=F32)
    ang = w[:, None] * f[None, :]
    return jnp.concatenate([t, jnp.cos(ang), -jnp.sin(ang)], axis=-1)


def kernel(x, g_norm, w_in, b_in, conv_w, conv_b, filt_w1, filt_b1, filt_w2, filt_b2, filt_w3, filt_b3, filt_w4, filt_freq, hyena_d, w_hyena_out, pool_w, pool_b, pool_scale, w_pool_out, w_out, g_final):
    batch, seq, d = x.shape
    dh = hyena_d.shape[1]
    dp = pool_b.shape[1]
    assert g_norm.shape[0] == 1, "single-layer block"
    assert d == dh == dp and conv_b.shape[1] == 3 * dh and w_in.shape[2] == 4 * dh + 2 * dp + 2 * d
    assert dh % HYENA_COLS == 0 and seq % ROW_CHUNK == 0
    assert seq % CONV_BLOCKS == 0 and (seq // CONV_BLOCKS) % SPEC_ROWS == 0

    cs = _transform_matrices(seq // CONV_BLOCKS)
    z_pad = jnp.pad(_filter_features(seq), ((0, 0), (0, LANES - FILTER_EMB)))
    w1_pad = jnp.pad(filt_w1[0], ((0, LANES - FILTER_EMB), (0, 0)))
    deltas = jnp.abs(jnp.linspace(MIN_DECAY, MAX_DECAY, dh, dtype=F32))[None, :]
    kre, kim = _filter_spectra(z_pad, w1_pad, filt_b1, filt_w2[0], filt_b2, filt_w3[0], filt_b3,
                               filt_freq, filt_w4[0], deltas, cs)

    x2 = x.reshape(batch * seq, d)
    u, gate, yp, gates = _inproj(x2, g_norm, w_in[0].astype(BF16), b_in, conv_w[0], conv_b,
                                 pool_w[0].astype(BF16), pool_b, pool_scale, seq)
    yh = _hyena(u, gate, hyena_d, kre, kim, cs, batch, seq)
    out = _merge(yh, yp, gates, x2, w_hyena_out[0].astype(BF16), w_pool_out[0].astype(BF16),
                 w_out[0].astype(BF16), g_final[None, :])
    return out.reshape(batch, seq, d)
```

```python
import functools
import math

import jax
import jax.numpy as jnp
from jax import lax
from jax.experimental import pallas as pl
from jax.experimental.pallas import tpu as pltpu

FILTER_EMB = 33
FAST_DECAY_PCT = 0.3
SLOW_DECAY_PCT = 1.5
DECAY_TARGET = 1e-2
MAX_DECAY = math.log(DECAY_TARGET) / FAST_DECAY_PCT
MIN_DECAY = math.log(DECAY_TARGET) / SLOW_DECAY_PCT
POOL_WINDOWS = (2, 4, 8, 16)
NORM_EPS = 1e-6

LANES = 128
SUBLANES = 8
MXU_DIM = 256
HYENA_COLS = MXU_DIM
ROW_CHUNK = 1024
MLP_ROWS = 256
CONV_BLOCKS = 4
SPEC_ROWS = 64
MIB = 1024 * 1024
FILTER_VMEM = 44 * MIB
INPROJ_VMEM = 56 * MIB
HYENA_VMEM = 32 * MIB
MERGE_VMEM = 50 * MIB

F32 = jnp.float32
BF16 = jnp.bfloat16
SPEC_DTYPE = BF16


def _dot(a, b):
    return jnp.dot(a, b, preferred_element_type=F32)


def _dot_hi(a, b):
    return jnp.dot(a, b, preferred_element_type=F32, precision=lax.Precision.HIGHEST)


def _dot_split(a, b):
    a_hi, b_hi = a.astype(BF16), b.astype(BF16)
    a_lo = (a - a_hi.astype(F32)).astype(BF16)
    b_lo = (b - b_hi.astype(F32)).astype(BF16)
    return _dot(a_hi, b_hi) + _dot(a_lo, b_hi) + _dot(a_hi, b_lo)


def _rms_scale(x):
    return x * lax.rsqrt(jnp.mean(x * x, axis=-1, keepdims=True) + NORM_EPS)


def _transform_matrices(lb):
    period = 8 * lb
    odd = 2 * jnp.arange(lb, dtype=jnp.int32) + 1
    ang = ((odd[:, None] * odd[None, :]) % period).astype(F32) * (2.0 * math.pi / period)
    return jnp.concatenate([jnp.cos(ang), jnp.sin(ang)], axis=1).astype(BF16)


def _filter_kernel(z_ref, w1_ref, b1_ref, w2_ref, b2_ref, w3_ref, b3_ref, fr_ref,
                   w4f_ref, w4b_ref, dl_ref, cs_ref, kre_ref, kim_ref,
                   h_ref, cphi_ref, sphi_ref, sgn_ref, kk_ref, xr_ref, xi_ref):
    seq = z_ref.shape[0]
    lb = cs_ref.shape[0]
    nb = seq // lb
    cb = dl_ref.shape[1]
    scale = 1.0 / lb

    @pl.when(pl.program_id(0) == 0)
    def _():
        fr = fr_ref[...]

        def mlp(i, carry):
            rows = pl.ds(pl.multiple_of(i * MLP_ROWS, MLP_ROWS), MLP_ROWS)
            h = jnp.sin(fr * (_dot_hi(z_ref[rows, :], w1_ref[...]) + b1_ref[...]))
            h = jnp.sin(fr * (_dot_hi(h, w2_ref[...]) + b2_ref[...]))
            h_ref[rows, :] = jnp.sin(fr * (_dot_hi(h, w3_ref[...]) + b3_ref[...]))
            return carry

        lax.fori_loop(0, seq // MLP_ROWS, mlp, 0)
        f_idx = lax.broadcasted_iota(jnp.int32, cphi_ref.shape, 0)
        phi = (f_idx.astype(F32) + 0.5) * (math.pi / (2 * lb))
        cphi_ref[...] = jnp.cos(phi) * scale
        sphi_ref[...] = jnp.sin(phi) * scale
        sgn_ref[...] = (1 - 2 * (f_idx & 1)).astype(F32)

    h = h_ref[...]
    decay = jnp.exp(-z_ref[:, 0:1] * dl_ref[...])
    taps = (_dot_split(h, w4f_ref[...]) * decay, _dot_split(h, w4b_ref[...]) * decay)
    taps = tuple(k.astype(BF16) for k in taps)
    for side, k in enumerate(taps):
        for s_ in range(nb):
            c0 = (side * nb + s_) * cb
            kk_ref[:, c0:c0 + cb] = k[s_ * lb:(s_ + 1) * lb]

    group = 2 * cb
    tile = lambda ref, width: jnp.concatenate([ref[...]] * (width // LANES), axis=1)
    c_phi, s_phi = tile(cphi_ref, group), tile(sphi_ref, group)
    for g in range(2 * nb * cb // group):
        cols = slice(g * group, (g + 1) * group)
        kk = kk_ref[:, cols]
        a = _dot(cs_ref[:, 0:lb], kk)
        b = _dot(cs_ref[:, lb:2 * lb], kk)
        xr_ref[:, cols] = c_phi * a + s_phi * b
        xi_ref[:, cols] = s_phi * a - c_phi * b

    sgn = tile(sgn_ref, cb)

    def seg(ref, side, s_):
        c0 = (side * nb + s_) * cb
        return ref[:, c0:c0 + cb]

    def causal_block(side, delta):
        r0 = (delta - 1) * lb
        first = taps[side][r0:r0 + 1].astype(F32) * scale
        re = seg(xr_ref, side, delta) - sgn * seg(xi_ref, side, delta - 1)
        im = seg(xi_ref, side, delta) + sgn * (seg(xr_ref, side, delta - 1) - first)
        return re, im

    mid = nb - 1
    out = kre_ref.dtype
    kre_ref[mid] = (seg(xr_ref, 0, 0) + seg(xr_ref, 1, 0)).astype(out)
    kim_ref[mid] = (seg(xi_ref, 0, 0) - seg(xi_ref, 1, 0)).astype(out)
    for delta in range(1, nb):
        re, im = causal_block(0, delta)
        kre_ref[mid + delta] = re.astype(out)
        kim_ref[mid + delta] = im.astype(out)
        re, im = causal_block(1, delta)
        kre_ref[mid - delta] = re.astype(out)
        kim_ref[mid - delta] = (-im).astype(out)


def _filter_spectra(z_pad, w1_pad, b1, w2, b2, w3, b3, freq, w4, deltas, cs):
    seq = z_pad.shape[0]
    lb = cs.shape[0]
    nb = seq // lb
    hf = w2.shape[0]
    dh = deltas.shape[1]
    cb = MXU_DIM
    nblk = dh // cb
    full = lambda a: pl.BlockSpec(a.shape, lambda c: (0,) * a.ndim)
    return pl.pallas_call(
        _filter_kernel,
        grid=(nblk,),
        in_specs=[
            full(z_pad), full(w1_pad), full(b1), full(w2), full(b2), full(w3), full(b3), full(freq),
            pl.BlockSpec((hf, cb), lambda c: (0, c)),
            pl.BlockSpec((hf, cb), lambda c: (0, nblk + c)),
            pl.BlockSpec((1, cb), lambda c: (0, c)),
            full(cs),
        ],
        out_specs=[pl.BlockSpec((2 * nb - 1, lb, cb), lambda c: (0, 0, c))] * 2,
        out_shape=[jax.ShapeDtypeStruct((2 * nb - 1, lb, dh), SPEC_DTYPE)] * 2,
        scratch_shapes=[
            pltpu.VMEM((seq, hf), F32),
            pltpu.VMEM((lb, LANES), F32),
            pltpu.VMEM((lb, LANES), F32),
            pltpu.VMEM((lb, LANES), F32),
            pltpu.VMEM((lb, 2 * nb * cb), BF16),
            pltpu.VMEM((lb, 2 * nb * cb), F32),
            pltpu.VMEM((lb, 2 * nb * cb), F32),
        ],
        compiler_params=pltpu.CompilerParams(
            dimension_semantics=("arbitrary",), vmem_limit_bytes=FILTER_VMEM),
        name="filter_spectra",
    )(z_pad, w1_pad, b1, w2, b2, w3, b3, freq, w4, w4, deltas, cs)


def _conv3(p, w, bias, lo, n):
    n_rows = p.shape[0]
    assert 1 <= lo and lo + n < n_rows
    dn = pltpu.roll(p, 1, 0)
    up = pltpu.roll(p, n_rows - 1, 0)
    out = dn * w[0:1] + p * w[1:2] + up * w[2:3] + bias
    return out[lo:lo + n]


def _window_mean_minus_self(p, win, lo, n, at_start, at_end):
    n_rows = p.shape[0]
    half = win // 2
    sl = SUBLANES
    assert lo >= win and lo + n + win <= n_rows and half <= sl
    s = p
    w = 1
    while w < win:
        s = s + pltpu.roll(s, w, 0)
        w *= 2
    if half > 1:
        s = pltpu.roll(s, n_rows - (half - 1), 0)
    total = s[lo:lo + n]
    edge = lax.broadcasted_iota(jnp.int32, (sl, p.shape[1]), 0)
    cnt_first = jnp.where(at_start > 0, jnp.minimum(edge + half, win), win).astype(F32)
    cnt_last = jnp.where(at_end > 0, jnp.minimum(sl - edge + half, win), win).astype(F32)
    mean = jnp.concatenate([total[0:sl] / cnt_first,
                            total[sl:n - sl] * (1.0 / win),
                            total[n - sl:] / cnt_last], axis=0)
    return mean - p[lo:lo + n]


def _inproj_kernel(xp_ref, x_ref, xn_ref, g_ref, w_ref, b_ref, cw_ref, cb_ref,
                   pw_ref, pb_ref, ps_ref, u_ref, gate_ref, yp_ref, p_ref, *, chunks_per_seq):
    i = pl.program_id(0)
    rows = x_ref.shape[0]
    halo = xp_ref.shape[0]
    dh = u_ref.shape[1]
    cb = MXU_DIM
    keep_prev = (i % chunks_per_seq != 0).astype(F32)
    keep_next = (i % chunks_per_seq != chunks_per_seq - 1).astype(F32)

    xw = jnp.concatenate([xp_ref[...], x_ref[...], xn_ref[...]], axis=0)
    h = (_rms_scale(xw) * g_ref[...]).astype(BF16)

    h_main = h[halo:halo + rows]

    def project(rows_h, c0):
        return _dot(rows_h, w_ref[:, c0:c0 + cb]) + b_ref[:, c0:c0 + cb]

    def conv_input(c0):
        p = project(h, c0)
        return jnp.concatenate(
            [p[0:halo] * keep_prev, p[halo:halo + rows], p[halo + rows:] * keep_next], axis=0)

    def conv(c0):
        return _conv3(conv_input(c0), cw_ref[:, c0:c0 + cb], cb_ref[:, c0:c0 + cb], halo, rows)

    dp = yp_ref.shape[1]
    assert dp == len(POOL_WINDOWS) * cb
    def hyena_group(c):
        lanes = slice(c * cb, (c + 1) * cb)
        u_ref[:, lanes] = conv(dh + c * cb) * conv(2 * dh + c * cb)
        z = project(h_main, 3 * dh + c * cb)
        gate_ref[:, lanes] = (conv(c * cb) * jax.nn.silu(z)).astype(BF16)

    def pool_group(g, win):
        lanes = slice(g * cb, (g + 1) * cb)
        pooled = _window_mean_minus_self(conv_input(4 * dh + g * cb), win, halo, rows,
                                         1.0 - keep_prev, 1.0 - keep_next)
        y = _dot(pooled.astype(BF16), pw_ref[g]) + pb_ref[:, lanes]
        z = project(h_main, 4 * dh + dp + g * cb)
        yp_ref[:, lanes] = (y * ps_ref[:, lanes] * jax.nn.silu(z)).astype(BF16)

    for c in range(dh // cb):
        hyena_group(c)
    for g, win in enumerate(POOL_WINDOWS):
        pool_group(g, win)

    for c0 in range(0, p_ref.shape[1], cb):
        p_ref[:, c0:c0 + cb] = project(h_main, 4 * dh + 2 * dp + c0).astype(BF16)


def _inproj(x2, g_norm, w_in_bf, b_in, conv_w, conv_b, pool_w_bf, pool_b, pool_scale, seq):
    n_tok, d = x2.shape
    cols = w_in_bf.shape[1]
    dh = conv_b.shape[1] // 3
    dp = pool_b.shape[1]
    n_gate = cols - 4 * dh - 2 * dp
    rows = ROW_CHUNK
    halo = 2 * SUBLANES
    per_halo = rows // halo
    last_halo = n_tok // halo - 1
    full = lambda a: pl.BlockSpec(a.shape, lambda i: (0,) * a.ndim)
    tile = lambda width: pl.BlockSpec((rows, width), lambda i: (i, 0))
    return pl.pallas_call(
        functools.partial(_inproj_kernel, chunks_per_seq=seq // rows),
        grid=(n_tok // rows,),
        in_specs=[
            pl.BlockSpec((halo, d), lambda i: (jnp.maximum(i * per_halo - 1, 0), 0)),
            tile(d),
            pl.BlockSpec((halo, d), lambda i: (jnp.minimum((i + 1) * per_halo, last_halo), 0)),
            full(g_norm),
            pl.BlockSpec(w_in_bf.shape, lambda i: (0, 0), pipeline_mode=pl.Buffered(1)),
            full(b_in), full(conv_w), full(conv_b),
            full(pool_w_bf), full(pool_b), full(pool_scale),
        ],
        out_specs=[tile(dh), tile(dh), tile(dp), tile(n_gate)],
        out_shape=[
            jax.ShapeDtypeStruct((n_tok, dh), F32),
            jax.ShapeDtypeStruct((n_tok, dh), BF16),
            jax.ShapeDtypeStruct((n_tok, dp), BF16),
            jax.ShapeDtypeStruct((n_tok, n_gate), BF16),
        ],
        compiler_params=pltpu.CompilerParams(
            dimension_semantics=("parallel",), vmem_limit_bytes=INPROJ_VMEM),
        name="inproj",
    )(x2, x2, x2, g_norm, w_in_bf, b_in, conv_w, conv_b, pool_w_bf, pool_b, pool_scale)


def _hyena_kernel(u_ref, gate_ref, d_ref, kre_ref, kim_ref, cs_ref, y_ref, av_ref, acc_ref, pq_ref):
    seq = u_ref.shape[0]
    lb = cs_ref.shape[0]
    nb = seq // lb
    mid = nb - 1
    blocks = [slice(b * lb, (b + 1) * lb) for b in range(nb)]
    chunks = [slice(r0, r0 + SPEC_ROWS) for r0 in range(0, lb, SPEC_ROWS)]

    def forward(b):
        ub = u_ref[blocks[b], :].astype(BF16)
        av_ref[2 * b] = _dot(cs_ref[:, 0:lb], ub).astype(av_ref.dtype)
        av_ref[2 * b + 1] = _dot(cs_ref[:, lb:2 * lb], ub).astype(av_ref.dtype)

    def spectrum(b, rows):
        return av_ref[2 * b, rows, :], av_ref[2 * b + 1, rows, :]

    def products(a, b, rows, ur, vi):
        kr = kre_ref[mid + a - b, rows, :]
        ki = kim_ref[mid + a - b, rows, :]
        return kr * ur + ki * vi, kr * vi - ki * ur

    def accumulate(b):
        for rows in chunks:
            ur, vi = spectrum(b, rows)
            for a in range(nb):
                tp, tq = products(a, b, rows, ur, vi)
                if b == 0:
                    acc_ref[2 * a, rows, :] = tp
                    acc_ref[2 * a + 1, rows, :] = tq
                else:
                    acc_ref[2 * a, rows, :] += tp
                    acc_ref[2 * a + 1, rows, :] += tq

    def finish(a):
        for rows in chunks:
            tp, tq = products(a, nb - 1, rows, *spectrum(nb - 1, rows))
            pq_ref[a, rows, :] = (acc_ref[2 * a, rows, :] + tp).astype(BF16)
            pq_ref[a, lb + rows.start:lb + rows.stop, :] = (
                acc_ref[2 * a + 1, rows, :] + tq).astype(BF16)

    def inverse(a):
        rows = blocks[a]
        y = _dot(cs_ref[...], pq_ref[a])
        y = y + u_ref[rows, :] * d_ref[...]
        y_ref[rows, :] = (y * gate_ref[rows, :].astype(F32)).astype(BF16)

    forward(0)
    for b in range(nb - 1):
        forward(b + 1)
        accumulate(b)
    for a in range(nb):
        finish(a)
        if a > 0:
            inverse(a - 1)
    inverse(nb - 1)


def _hyena(u, gate, hyena_d, kre, kim, cs, batch, seq):
    dh = hyena_d.shape[1]
    cw = HYENA_COLS
    lb = cs.shape[0]
    nb = seq // lb
    act = pl.BlockSpec((seq, cw), lambda c, b: (b, c))
    spectra = pl.BlockSpec((2 * nb - 1, lb, cw), lambda c, b: (0, 0, c),
                           pipeline_mode=pl.Buffered(1))
    return pl.pallas_call(
        _hyena_kernel,
        grid=(dh // cw, batch),
        in_specs=[
            act, act,
            pl.BlockSpec((1, cw), lambda c, b: (0, c)),
            spectra, spectra,
            pl.BlockSpec(cs.shape, lambda c, b: (0, 0)),
        ],
        out_specs=act,
        out_shape=jax.ShapeDtypeStruct((batch * seq, dh), BF16),
        scratch_shapes=[
            pltpu.VMEM((2 * nb, lb, cw), SPEC_DTYPE),
            pltpu.VMEM((2 * nb, lb, cw), SPEC_DTYPE),
            pltpu.VMEM((nb, 2 * lb, cw), BF16),
        ],
        compiler_params=pltpu.CompilerParams(
            dimension_semantics=("arbitrary", "arbitrary"), vmem_limit_bytes=HYENA_VMEM),
        name="hyena",
    )(u, gate, hyena_d, kre, kim, cs)


def _merge_kernel(yh_ref, yp_ref, g0_ref, g1_ref, x_ref, wh_ref, wp_ref, wo_ref, gf_ref, o_ref):
    out_h = _dot(yh_ref[...], wh_ref[...])
    out_p = _dot(yp_ref[...], wp_ref[...])
    merged = (jax.nn.sigmoid(g0_ref[...].astype(F32)) * out_h
              + jax.nn.sigmoid(g1_ref[...].astype(F32)) * out_p)
    res = x_ref[...] + _dot(merged.astype(BF16), wo_ref[...])
    o_ref[...] = _rms_scale(res) * gf_ref[...]


def _merge(yh, yp, gates, x2, wh_bf, wp_bf, wo_bf, g_final, tm=1024):
    n_tok, d = x2.shape
    tile = lambda col: pl.BlockSpec((tm, d), lambda i: (i, col))
    resident = lambda a: pl.BlockSpec(a.shape, lambda i: (0, 0), pipeline_mode=pl.Buffered(1))
    return pl.pallas_call(
        _merge_kernel,
        grid=(n_tok // tm,),
        in_specs=[
            tile(0), tile(0), tile(0), tile(1), tile(0),
            resident(wh_bf), resident(wp_bf), resident(wo_bf),
            pl.BlockSpec((1, d), lambda i: (0, 0)),
        ],
        out_specs=tile(0),
        out_shape=jax.ShapeDtypeStruct((n_tok, d), F32),
        compiler_params=pltpu.CompilerParams(
            dimension_semantics=("parallel",), vmem_limit_bytes=MERGE_VMEM),
        name="merge",
    )(yh, yp, gates, gates, x2, wh_bf, wp_bf, wo_bf, g_final)


def _filter_features(seq):
    t = jnp.linspace(0.0, 1.0, seq, dtype=F32)[:, None]
    bands = (FILTER_EMB - 1) // 2
    w = 2.0 * math.pi * jnp.arange(seq, dtype=F32) / seq
    f = jnp.linspace(1e-4, bands - 1, bands, dtype=F32)
    ang = w[:, None] * f[None, :]
    return jnp.concatenate([t, jnp.cos(ang), -jnp.sin(ang)], axis=-1)


def kernel(x, g_norm, w_in, b_in, conv_w, conv_b, filt_w1, filt_b1, filt_w2, filt_b2, filt_w3, filt_b3, filt_w4, filt_freq, hyena_d, w_hyena_out, pool_w, pool_b, pool_scale, w_pool_out, w_out, g_final):
    batch, seq, d = x.shape
    dh = hyena_d.shape[1]
    dp = pool_b.shape[1]
    assert g_norm.shape[0] == 1, "single-layer block"
    assert d == dh == dp and conv_b.shape[1] == 3 * dh and w_in.shape[2] == 4 * dh + 2 * dp + 2 * d
    assert dh % HYENA_COLS == 0 and seq % ROW_CHUNK == 0
    assert seq % CONV_BLOCKS == 0 and (seq // CONV_BLOCKS) % SPEC_ROWS == 0

    cs = _transform_matrices(seq // CONV_BLOCKS)
    z_pad = jnp.pad(_filter_features(seq), ((0, 0), (0, LANES - FILTER_EMB)))
    w1_pad = jnp.pad(filt_w1[0], ((0, LANES - FILTER_EMB), (0, 0)))
    deltas = jnp.abs(jnp.linspace(MIN_DECAY, MAX_DECAY, dh, dtype=F32))[None, :]
    kre, kim = _filter_spectra(z_pad, w1_pad, filt_b1, filt_w2[0], filt_b2, filt_w3[0], filt_b3,
                               filt_freq, filt_w4[0], deltas, cs)

    x2 = x.reshape(batch * seq, d)
    u, gate, yp, gates = _inproj(x2, g_norm, w_in[0].astype(BF16), b_in, conv_w[0], conv_b,
                                 pool_w[0].astype(BF16), pool_b, pool_scale, seq)
    yh = _hyena(u, gate, hyena_d, kre, kim, cs, batch, seq)
    out = _merge(yh, yp, gates, x2, w_hyena_out[0].astype(BF16), w_pool_out[0].astype(BF16),
                 w_out[0].astype(BF16), g_final[None, :])
    return out.reshape(batch, seq, d)
```

```python
import functools
import math

import jax
import jax.numpy as jnp
from jax import lax
from jax.experimental import pallas as pl
from jax.experimental.pallas import tpu as pltpu

FILTER_EMB = 33
FAST_DECAY_PCT = 0.3
SLOW_DECAY_PCT = 1.5
DECAY_TARGET = 1e-2
MAX_DECAY = math.log(DECAY_TARGET) / FAST_DECAY_PCT
MIN_DECAY = math.log(DECAY_TARGET) / SLOW_DECAY_PCT
POOL_WINDOWS = (2, 4, 8, 16)
NORM_EPS = 1e-6

LANES = 128
SUBLANES = 8
MXU_DIM = 256
HYENA_COLS = MXU_DIM
ROW_CHUNK = 1024
MLP_ROWS = 256
CONV_BLOCKS = 4
SPEC_ROWS = 64
MIB = 1024 * 1024
FILTER_VMEM = 40 * MIB
INPROJ_VMEM = 55 * MIB
HYENA_VMEM = 24 * MIB
MERGE_VMEM = 47 * MIB

F32 = jnp.float32
BF16 = jnp.bfloat16
SPEC_DTYPE = BF16


def _dot(a, b):
    return jnp.dot(a, b, preferred_element_type=F32)


def _dot_hi(a, b):
    return jnp.dot(a, b, preferred_element_type=F32, precision=lax.Precision.HIGHEST)


def _dot_split(a, b):
    a_hi, b_hi = a.astype(BF16), b.astype(BF16)
    a_lo = (a - a_hi.astype(F32)).astype(BF16)
    b_lo = (b - b_hi.astype(F32)).astype(BF16)
    return _dot(a_hi, b_hi) + _dot(a_lo, b_hi) + _dot(a_hi, b_lo)


def _rms_scale(x):
    return x * lax.rsqrt(jnp.mean(x * x, axis=-1, keepdims=True) + NORM_EPS)


def _transform_matrices(lb):
    period = 8 * lb
    odd = 2 * jnp.arange(lb, dtype=jnp.int32) + 1
    ang = ((odd[:, None] * odd[None, :]) % period).astype(F32) * (2.0 * math.pi / period)
    return jnp.concatenate([jnp.cos(ang), jnp.sin(ang)], axis=1).astype(BF16)


def _filter_kernel(z_ref, w1_ref, b1_ref, w2_ref, b2_ref, w3_ref, b3_ref, fr_ref,
                   w4f_ref, w4b_ref, dl_ref, cs_ref, kre_ref, kim_ref,
                   h_ref, cphi_ref, sphi_ref, sgn_ref, kk_ref, xr_ref, xi_ref):
    seq = z_ref.shape[0]
    lb = cs_ref.shape[0]
    nb = seq // lb
    cb = dl_ref.shape[1]
    scale = 1.0 / lb

    @pl.when(pl.program_id(0) == 0)
    def _():
        fr = fr_ref[...]

        def mlp(i, carry):
            rows = pl.ds(pl.multiple_of(i * MLP_ROWS, MLP_ROWS), MLP_ROWS)
            h = jnp.sin(fr * (_dot_hi(z_ref[rows, :], w1_ref[...]) + b1_ref[...]))
            h = jnp.sin(fr * (_dot_hi(h, w2_ref[...]) + b2_ref[...]))
            h_ref[rows, :] = jnp.sin(fr * (_dot_hi(h, w3_ref[...]) + b3_ref[...]))
            return carry

        lax.fori_loop(0, seq // MLP_ROWS, mlp, 0)
        f_idx = lax.broadcasted_iota(jnp.int32, cphi_ref.shape, 0)
        phi = (f_idx.astype(F32) + 0.5) * (math.pi / (2 * lb))
        cphi_ref[...] = jnp.cos(phi) * scale
        sphi_ref[...] = jnp.sin(phi) * scale
        sgn_ref[...] = (1 - 2 * (f_idx & 1)).astype(F32)

    h = h_ref[...]
    decay = jnp.exp(-z_ref[:, 0:1] * dl_ref[...])
    taps = (_dot_split(h, w4f_ref[...]) * decay, _dot_split(h, w4b_ref[...]) * decay)
    taps = tuple(k.astype(BF16) for k in taps)
    for side, k in enumerate(taps):
        for s_ in range(nb):
            c0 = (side * nb + s_) * cb
            kk_ref[:, c0:c0 + cb] = k[s_ * lb:(s_ + 1) * lb]

    group = 2 * cb
    tile = lambda ref, width: jnp.concatenate([ref[...]] * (width // LANES), axis=1)
    c_phi, s_phi = tile(cphi_ref, group), tile(sphi_ref, group)
    for g in range(2 * nb * cb // group):
        cols = slice(g * group, (g + 1) * group)
        kk = kk_ref[:, cols]
        a = _dot(cs_ref[:, 0:lb], kk)
        b = _dot(cs_ref[:, lb:2 * lb], kk)
        xr_ref[:, cols] = c_phi * a + s_phi * b
        xi_ref[:, cols] = s_phi * a - c_phi * b

    sgn = tile(sgn_ref, cb)

    def seg(ref, side, s_):
        c0 = (side * nb + s_) * cb
        return ref[:, c0:c0 + cb]

    def causal_block(side, delta):
        r0 = (delta - 1) * lb
        first = taps[side][r0:r0 + 1].astype(F32) * scale
        re = seg(xr_ref, side, delta) - sgn * seg(xi_ref, side, delta - 1)
        im = seg(xi_ref, side, delta) + sgn * (seg(xr_ref, side, delta - 1) - first)
        return re, im

    mid = nb - 1
    out = kre_ref.dtype
    kre_ref[mid] = (seg(xr_ref, 0, 0) + seg(xr_ref, 1, 0)).astype(out)
    kim_ref[mid] = (seg(xi_ref, 0, 0) - seg(xi_ref, 1, 0)).astype(out)
    for delta in range(1, nb):
        re, im = causal_block(0, delta)
        kre_ref[mid + delta] = re.astype(out)
        kim_ref[mid + delta] = im.astype(out)
        re, im = causal_block(1, delta)
        kre_ref[mid - delta] = re.astype(out)
        kim_ref[mid - delta] = (-im).astype(out)


def _filter_spectra(z_pad, w1_pad, b1, w2, b2, w3, b3, freq, w4, deltas, cs):
    seq = z_pad.shape[0]
    lb = cs.shape[0]
    nb = seq // lb
    hf = w2.shape[0]
    dh = deltas.shape[1]
    cb = MXU_DIM
    nblk = dh // cb
    full = lambda a: pl.BlockSpec(a.shape, lambda c: (0,) * a.ndim)
    return pl.pallas_call(
        _filter_kernel,
        grid=(nblk,),
        in_specs=[
            full(z_pad), full(w1_pad), full(b1), full(w2), full(b2), full(w3), full(b3), full(freq),
            pl.BlockSpec((hf, cb), lambda c: (0, c)),
            pl.BlockSpec((hf, cb), lambda c: (0, nblk + c)),
            pl.BlockSpec((1, cb), lambda c: (0, c)),
            full(cs),
        ],
        out_specs=[pl.BlockSpec((2 * nb - 1, lb, cb), lambda c: (0, 0, c))] * 2,
        out_shape=[jax.ShapeDtypeStruct((2 * nb - 1, lb, dh), SPEC_DTYPE)] * 2,
        scratch_shapes=[
            pltpu.VMEM((seq, hf), F32),
            pltpu.VMEM((lb, LANES), F32),
            pltpu.VMEM((lb, LANES), F32),
            pltpu.VMEM((lb, LANES), F32),
            pltpu.VMEM((lb, 2 * nb * cb), BF16),
            pltpu.VMEM((lb, 2 * nb * cb), F32),
            pltpu.VMEM((lb, 2 * nb * cb), F32),
        ],
        compiler_params=pltpu.CompilerParams(
            dimension_semantics=("arbitrary",), vmem_limit_bytes=FILTER_VMEM),
        name="filter_spectra",
    )(z_pad, w1_pad, b1, w2, b2, w3, b3, freq, w4, w4, deltas, cs)


def _conv3(p, w, bias, lo, n):
    n_rows = p.shape[0]
    assert 1 <= lo and lo + n < n_rows
    dn = pltpu.roll(p, 1, 0)
    up = pltpu.roll(p, n_rows - 1, 0)
    out = dn * w[0:1] + p * w[1:2] + up * w[2:3] + bias
    return out[lo:lo + n]


def _window_mean_minus_self(p, win, lo, n, at_start, at_end):
    n_rows = p.shape[0]
    half = win // 2
    sl = SUBLANES
    assert lo >= win and lo + n + win <= n_rows and half <= sl
    s = p
    w = 1
    while w < win:
        s = s + pltpu.roll(s, w, 0)
        w *= 2
    if half > 1:
        s = pltpu.roll(s, n_rows - (half - 1), 0)
    total = s[lo:lo + n]
    edge = lax.broadcasted_iota(jnp.int32, (sl, p.shape[1]), 0)
    cnt_first = jnp.where(at_start > 0, jnp.minimum(edge + half, win), win).astype(F32)
    cnt_last = jnp.where(at_end > 0, jnp.minimum(sl - edge + half, win), win).astype(F32)
    mean = jnp.concatenate([total[0:sl] / cnt_first,
                            total[sl:n - sl] * (1.0 / win),
                            total[n - sl:] / cnt_last], axis=0)
    return mean - p[lo:lo + n]


def _inproj_kernel(xp_ref, x_ref, xn_ref, g_ref, w_ref, b_ref, cw_ref, cb_ref,
                   pw_ref, pb_ref, ps_ref, u_ref, gate_ref, yp_ref, p_ref, *, chunks_per_seq):
    i = pl.program_id(0)
    rows = x_ref.shape[0]
    halo = xp_ref.shape[0]
    dh = u_ref.shape[1]
    cb = MXU_DIM
    keep_prev = (i % chunks_per_seq != 0).astype(F32)
    keep_next = (i % chunks_per_seq != chunks_per_seq - 1).astype(F32)

    xw = jnp.concatenate([xp_ref[...], x_ref[...], xn_ref[...]], axis=0)
    h = (_rms_scale(xw) * g_ref[...]).astype(BF16)

    h_main = h[halo:halo + rows]

    def project(rows_h, c0):
        return _dot(rows_h, w_ref[:, c0:c0 + cb]) + b_ref[:, c0:c0 + cb]

    def conv_input(c0):
        p = project(h, c0)
        return jnp.concatenate(
            [p[0:halo] * keep_prev, p[halo:halo + rows], p[halo + rows:] * keep_next], axis=0)

    def conv(c0):
        return _conv3(conv_input(c0), cw_ref[:, c0:c0 + cb], cb_ref[:, c0:c0 + cb], halo, rows)

    dp = yp_ref.shape[1]
    assert dp == len(POOL_WINDOWS) * cb
    def hyena_group(c):
        lanes = slice(c * cb, (c + 1) * cb)
        u_ref[:, lanes] = conv(dh + c * cb) * conv(2 * dh + c * cb)
        z = project(h_main, 3 * dh + c * cb)
        gate_ref[:, lanes] = (conv(c * cb) * jax.nn.silu(z)).astype(BF16)

    def pool_group(g, win):
        lanes = slice(g * cb, (g + 1) * cb)
        pooled = _window_mean_minus_self(conv_input(4 * dh + g * cb), win, halo, rows,
                                         1.0 - keep_prev, 1.0 - keep_next)
        y = _dot(pooled.astype(BF16), pw_ref[g]) + pb_ref[:, lanes]
        z = project(h_main, 4 * dh + dp + g * cb)
        yp_ref[:, lanes] = (y * ps_ref[:, lanes] * jax.nn.silu(z)).astype(BF16)

    for c in range(dh // cb):
        hyena_group(c)
    for g, win in enumerate(POOL_WINDOWS):
        pool_group(g, win)

    for c0 in range(0, p_ref.shape[1], cb):
        p_ref[:, c0:c0 + cb] = project(h_main, 4 * dh + 2 * dp + c0).astype(BF16)


def _inproj(x2, g_norm, w_in_bf, b_in, conv_w, conv_b, pool_w_bf, pool_b, pool_scale, seq):
    n_tok, d = x2.shape
    cols = w_in_bf.shape[1]
    dh = conv_b.shape[1] // 3
    dp = pool_b.shape[1]
    n_gate = cols - 4 * dh - 2 * dp
    rows = ROW_CHUNK
    halo = 2 * SUBLANES
    per_halo = rows // halo
    last_halo = n_tok // halo - 1
    full = lambda a: pl.BlockSpec(a.shape, lambda i: (0,) * a.ndim)
    tile = lambda width: pl.BlockSpec((rows, width), lambda i: (i, 0))
    return pl.pallas_call(
        functools.partial(_inproj_kernel, chunks_per_seq=seq // rows),
        grid=(n_tok // rows,),
        in_specs=[
            pl.BlockSpec((halo, d), lambda i: (jnp.maximum(i * per_halo - 1, 0), 0)),
            tile(d),
            pl.BlockSpec((halo, d), lambda i: (jnp.minimum((i + 1) * per_halo, last_halo), 0)),
            full(g_norm),
            pl.BlockSpec(w_in_bf.shape, lambda i: (0, 0), pipeline_mode=pl.Buffered(1)),
            full(b_in), full(conv_w), full(conv_b),
            full(pool_w_bf), full(pool_b), full(pool_scale),
        ],
        out_specs=[tile(dh), tile(dh), tile(dp), tile(n_gate)],
        out_shape=[
            jax.ShapeDtypeStruct((n_tok, dh), F32),
            jax.ShapeDtypeStruct((n_tok, dh), BF16),
            jax.ShapeDtypeStruct((n_tok, dp), BF16),
            jax.ShapeDtypeStruct((n_tok, n_gate), BF16),
        ],
        compiler_params=pltpu.CompilerParams(
            dimension_semantics=("parallel",), vmem_limit_bytes=INPROJ_VMEM),
        name="inproj",
    )(x2, x2, x2, g_norm, w_in_bf, b_in, conv_w, conv_b, pool_w_bf, pool_b, pool_scale)


def _hyena_kernel(u_ref, gate_ref, d_ref, kre_ref, kim_ref, cs_ref, y_ref, av_ref, acc_ref, pq_ref):
    seq = u_ref.shape[0]
    lb = cs_ref.shape[0]
    nb = seq // lb
    mid = nb - 1
    blocks = [slice(b * lb, (b + 1) * lb) for b in range(nb)]
    chunks = [slice(r0, r0 + SPEC_ROWS) for r0 in range(0, lb, SPEC_ROWS)]

    def forward(b):
        ub = u_ref[blocks[b], :].astype(BF16)
        av_ref[2 * b] = _dot(cs_ref[:, 0:lb], ub).astype(av_ref.dtype)
        av_ref[2 * b + 1] = _dot(cs_ref[:, lb:2 * lb], ub).astype(av_ref.dtype)

    def spectrum(b, rows):
        return av_ref[2 * b, rows, :], av_ref[2 * b + 1, rows, :]

    def products(a, b, rows, ur, vi):
        kr = kre_ref[mid + a - b, rows, :]
        ki = kim_ref[mid + a - b, rows, :]
        return kr * ur + ki * vi, kr * vi - ki * ur

    def accumulate(b):
        for rows in chunks:
            ur, vi = spectrum(b, rows)
            for a in range(nb):
                tp, tq = products(a, b, rows, ur, vi)
                if b == 0:
                    acc_ref[2 * a, rows, :] = tp
                    acc_ref[2 * a + 1, rows, :] = tq
                else:
                    acc_ref[2 * a, rows, :] += tp
                    acc_ref[2 * a + 1, rows, :] += tq

    def finish(a):
        for rows in chunks:
            tp, tq = products(a, nb - 1, rows, *spectrum(nb - 1, rows))
            pq_ref[a, rows, :] = (acc_ref[2 * a, rows, :] + tp).astype(BF16)
            pq_ref[a, lb + rows.start:lb + rows.stop, :] = (
                acc_ref[2 * a + 1, rows, :] + tq).astype(BF16)

    def inverse(a):
        rows = blocks[a]
        y = _dot(cs_ref[...], pq_ref[a])
        y = y + u_ref[rows, :] * d_ref[...]
        y_ref[rows, :] = (y * gate_ref[rows, :].astype(F32)).astype(BF16)

    forward(0)
    for b in range(nb - 1):
        forward(b + 1)
        accumulate(b)
    for a in range(nb):
        finish(a)
        if a > 0:
            inverse(a - 1)
    inverse(nb - 1)


def _hyena(u, gate, hyena_d, kre, kim, cs, batch, seq):
    dh = hyena_d.shape[1]
    cw = HYENA_COLS
    lb = cs.shape[0]
    nb = seq // lb
    act = pl.BlockSpec((seq, cw), lambda c, b: (b, c))
    spectra = pl.BlockSpec((2 * nb - 1, lb, cw), lambda c, b: (0, 0, c),
                           pipeline_mode=pl.Buffered(1))
    return pl.pallas_call(
        _hyena_kernel,
        grid=(dh // cw, batch),
        in_specs=[
            act, act,
            pl.BlockSpec((1, cw), lambda c, b: (0, c)),
            spectra, spectra,
            pl.BlockSpec(cs.shape, lambda c, b: (0, 0)),
        ],
        out_specs=act,
        out_shape=jax.ShapeDtypeStruct((batch * seq, dh), BF16),
        scratch_shapes=[
            pltpu.VMEM((2 * nb, lb, cw), SPEC_DTYPE),
            pltpu.VMEM((2 * nb, lb, cw), SPEC_DTYPE),
            pltpu.VMEM((nb, 2 * lb, cw), BF16),
        ],
        compiler_params=pltpu.CompilerParams(
            dimension_semantics=("arbitrary", "arbitrary"), vmem_limit_bytes=HYENA_VMEM),
        name="hyena",
    )(u, gate, hyena_d, kre, kim, cs)


def _merge_kernel(yh_ref, yp_ref, g0_ref, g1_ref, x_ref, wh_ref, wp_ref, wo_ref, gf_ref, o_ref):
    out_h = _dot(yh_ref[...], wh_ref[...])
    out_p = _dot(yp_ref[...], wp_ref[...])
    merged = (jax.nn.sigmoid(g0_ref[...].astype(F32)) * out_h
              + jax.nn.sigmoid(g1_ref[...].astype(F32)) * out_p)
    res = x_ref[...] + _dot(merged.astype(BF16), wo_ref[...])
    o_ref[...] = _rms_scale(res) * gf_ref[...]


def _merge(yh, yp, gates, x2, wh_bf, wp_bf, wo_bf, g_final, tm=1024):
    n_tok, d = x2.shape
    tile = lambda col: pl.BlockSpec((tm, d), lambda i: (i, col))
    resident = lambda a: pl.BlockSpec(a.shape, lambda i: (0, 0), pipeline_mode=pl.Buffered(1))
    return pl.pallas_call(
        _merge_kernel,
        grid=(n_tok // tm,),
        in_specs=[
            tile(0), tile(0), tile(0), tile(1), tile(0),
            resident(wh_bf), resident(wp_bf), resident(wo_bf),
            pl.BlockSpec((1, d), lambda i: (0, 0)),
        ],
        out_specs=tile(0),
        out_shape=jax.ShapeDtypeStruct((n_tok, d), F32),
        compiler_params=pltpu.CompilerParams(
            dimension_semantics=("parallel",), vmem_limit_bytes=MERGE_VMEM),
        name="merge",
    )(yh, yp, gates, gates, x2, wh_bf, wp_bf, wo_bf, g_final)


def _filter_features(seq):
    t = jnp.linspace(0.0, 1.0, seq, dtype=F32)[:, None]
    bands = (FILTER_EMB - 1) // 2
    w = 2.0 * math.pi * jnp.arange(seq, dtype=F32) / seq
    f = jnp.linspace(1e-4, bands - 1, bands, dtype=F32)
    ang = w[:, None] * f[None, :]
    return jnp.concatenate([t, jnp.cos(ang), -jnp.sin(ang)], axis=-1)


def kernel(x, g_norm, w_in, b_in, conv_w, conv_b, filt_w1, filt_b1, filt_w2, filt_b2, filt_w3, filt_b3, filt_w4, filt_freq, hyena_d, w_hyena_out, pool_w, pool_b, pool_scale, w_pool_out, w_out, g_final):
    batch, seq, d = x.shape
    dh = hyena_d.shape[1]
    dp = pool_b.shape[1]
    assert g_norm.shape[0] == 1, "single-layer block"
    assert d == dh == dp and conv_b.shape[1] == 3 * dh and w_in.shape[2] == 4 * dh + 2 * dp + 2 * d
    assert dh % HYENA_COLS == 0 and seq % ROW_CHUNK == 0
    assert seq % CONV_BLOCKS == 0 and (seq // CONV_BLOCKS) % SPEC_ROWS == 0

    cs = _transform_matrices(seq // CONV_BLOCKS)
    z_pad = jnp.pad(_filter_features(seq), ((0, 0), (0, LANES - FILTER_EMB)))
    w1_pad = jnp.pad(filt_w1[0], ((0, LANES - FILTER_EMB), (0, 0)))
    deltas = jnp.abs(jnp.linspace(MIN_DECAY, MAX_DECAY, dh, dtype=F32))[None, :]
    x2 = x.reshape(batch * seq, d)
    u, gate, yp, gates = _inproj(x2, g_norm, w_in[0].astype(BF16), b_in, conv_w[0], conv_b,
                                 pool_w[0].astype(BF16), pool_b, pool_scale, seq)
    kre, kim = _filter_spectra(z_pad, w1_pad, filt_b1, filt_w2[0], filt_b2, filt_w3[0], filt_b3,
                               filt_freq, filt_w4[0], deltas, cs)
    yh = _hyena(u, gate, hyena_d, kre, kim, cs, batch, seq)
    out = _merge(yh, yp, gates, x2, w_hyena_out[0].astype(BF16), w_pool_out[0].astype(BF16),
                 w_out[0].astype(BF16), g_final[None, :])
    return out.reshape(batch, seq, d)
```

```python
import functools
import math

import jax
import jax.numpy as jnp
from jax import lax
from jax.experimental import pallas as pl
from jax.experimental.pallas import tpu as pltpu

FILTER_EMB = 33
FAST_DECAY_PCT = 0.3
SLOW_DECAY_PCT = 1.5
DECAY_TARGET = 1e-2
MAX_DECAY = math.log(DECAY_TARGET) / FAST_DECAY_PCT
MIN_DECAY = math.log(DECAY_TARGET) / SLOW_DECAY_PCT
POOL_WINDOWS = (2, 4, 8, 16)
NORM_EPS = 1e-6

LANES = 128
SUBLANES = 8
MXU_DIM = 256
HYENA_COLS = MXU_DIM
HYENA_SEQS = 2
ROW_CHUNK = 1024
MLP_ROWS = 256
CONV_BLOCKS = 4
SPEC_ROWS = 64
MIB = 1024 * 1024
FILTER_VMEM = 40 * MIB
INPROJ_VMEM = 55 * MIB
HYENA_VMEM = 36 * MIB
MERGE_VMEM = 47 * MIB

F32 = jnp.float32
BF16 = jnp.bfloat16
SPEC_DTYPE = BF16


def _dot(a, b):
    return jnp.dot(a, b, preferred_element_type=F32)


def _dot_hi(a, b):
    return jnp.dot(a, b, preferred_element_type=F32, precision=lax.Precision.HIGHEST)


def _dot_split(a, b):
    a_hi, b_hi = a.astype(BF16), b.astype(BF16)
    a_lo = (a - a_hi.astype(F32)).astype(BF16)
    b_lo = (b - b_hi.astype(F32)).astype(BF16)
    return _dot(a_hi, b_hi) + _dot(a_lo, b_hi) + _dot(a_hi, b_lo)


def _rms_scale(x):
    return x * lax.rsqrt(jnp.mean(x * x, axis=-1, keepdims=True) + NORM_EPS)


def _transform_matrices(lb):
    period = 8 * lb
    odd = 2 * jnp.arange(lb, dtype=jnp.int32) + 1
    ang = ((odd[:, None] * odd[None, :]) % period).astype(F32) * (2.0 * math.pi / period)
    return jnp.concatenate([jnp.cos(ang), jnp.sin(ang)], axis=1).astype(BF16)


def _filter_kernel(z_ref, w1_ref, b1_ref, w2_ref, b2_ref, w3_ref, b3_ref, fr_ref,
                   w4f_ref, w4b_ref, dl_ref, cs_ref, kre_ref, kim_ref,
                   h_ref, cphi_ref, sphi_ref, sgn_ref, kk_ref, xr_ref, xi_ref):
    seq = z_ref.shape[0]
    lb = cs_ref.shape[0]
    nb = seq // lb
    cb = dl_ref.shape[1]
    scale = 1.0 / lb

    @pl.when(pl.program_id(0) == 0)
    def _():
        fr = fr_ref[...]

        def mlp(i, carry):
            rows = pl.ds(pl.multiple_of(i * MLP_ROWS, MLP_ROWS), MLP_ROWS)
            h = jnp.sin(fr * (_dot_hi(z_ref[rows, :], w1_ref[...]) + b1_ref[...]))
            h = jnp.sin(fr * (_dot_hi(h, w2_ref[...]) + b2_ref[...]))
            h_ref[rows, :] = jnp.sin(fr * (_dot_hi(h, w3_ref[...]) + b3_ref[...]))
            return carry

        lax.fori_loop(0, seq // MLP_ROWS, mlp, 0)
        f_idx = lax.broadcasted_iota(jnp.int32, cphi_ref.shape, 0)
        phi = (f_idx.astype(F32) + 0.5) * (math.pi / (2 * lb))
        cphi_ref[...] = jnp.cos(phi) * scale
        sphi_ref[...] = jnp.sin(phi) * scale
        sgn_ref[...] = (1 - 2 * (f_idx & 1)).astype(F32)

    h = h_ref[...]
    decay = jnp.exp(-z_ref[:, 0:1] * dl_ref[...])
    taps = (_dot_split(h, w4f_ref[...]) * decay, _dot_split(h, w4b_ref[...]) * decay)
    taps = tuple(k.astype(BF16) for k in taps)
    for side, k in enumerate(taps):
        for s_ in range(nb):
            c0 = (side * nb + s_) * cb
            kk_ref[:, c0:c0 + cb] = k[s_ * lb:(s_ + 1) * lb]

    group = 2 * cb
    tile = lambda ref, width: jnp.concatenate([ref[...]] * (width // LANES), axis=1)
    c_phi, s_phi = tile(cphi_ref, group), tile(sphi_ref, group)
    for g in range(2 * nb * cb // group):
        cols = slice(g * group, (g + 1) * group)
        kk = kk_ref[:, cols]
        a = _dot(cs_ref[:, 0:lb], kk)
        b = _dot(cs_ref[:, lb:2 * lb], kk)
        xr_ref[:, cols] = c_phi * a + s_phi * b
        xi_ref[:, cols] = s_phi * a - c_phi * b

    sgn = tile(sgn_ref, cb)

    def seg(ref, side, s_):
        c0 = (side * nb + s_) * cb
        return ref[:, c0:c0 + cb]

    def causal_block(side, delta):
        r0 = (delta - 1) * lb
        first = taps[side][r0:r0 + 1].astype(F32) * scale
        re = seg(xr_ref, side, delta) - sgn * seg(xi_ref, side, delta - 1)
        im = seg(xi_ref, side, delta) + sgn * (seg(xr_ref, side, delta - 1) - first)
        return re, im

    mid = nb - 1
    out = kre_ref.dtype
    kre_ref[mid] = (seg(xr_ref, 0, 0) + seg(xr_ref, 1, 0)).astype(out)
    kim_ref[mid] = (seg(xi_ref, 0, 0) - seg(xi_ref, 1, 0)).astype(out)
    for delta in range(1, nb):
        re, im = causal_block(0, delta)
        kre_ref[mid + delta] = re.astype(out)
        kim_ref[mid + delta] = im.astype(out)
        re, im = causal_block(1, delta)
        kre_ref[mid - delta] = re.astype(out)
        kim_ref[mid - delta] = (-im).astype(out)


def _filter_spectra(z_pad, w1_pad, b1, w2, b2, w3, b3, freq, w4, deltas, cs):
    seq = z_pad.shape[0]
    lb = cs.shape[0]
    nb = seq // lb
    hf = w2.shape[0]
    dh = deltas.shape[1]
    cb = MXU_DIM
    nblk = dh // cb
    full = lambda a: pl.BlockSpec(a.shape, lambda c: (0,) * a.ndim)
    return pl.pallas_call(
        _filter_kernel,
        grid=(nblk,),
        in_specs=[
            full(z_pad), full(w1_pad), full(b1), full(w2), full(b2), full(w3), full(b3), full(freq),
            pl.BlockSpec((hf, cb), lambda c: (0, c)),
            pl.BlockSpec((hf, cb), lambda c: (0, nblk + c)),
            pl.BlockSpec((1, cb), lambda c: (0, c)),
            full(cs),
        ],
        out_specs=[pl.BlockSpec((2 * nb - 1, lb, cb), lambda c: (0, 0, c))] * 2,
        out_shape=[jax.ShapeDtypeStruct((2 * nb - 1, lb, dh), SPEC_DTYPE)] * 2,
        scratch_shapes=[
            pltpu.VMEM((seq, hf), F32),
            pltpu.VMEM((lb, LANES), F32),
            pltpu.VMEM((lb, LANES), F32),
            pltpu.VMEM((lb, LANES), F32),
            pltpu.VMEM((lb, 2 * nb * cb), BF16),
            pltpu.VMEM((lb, 2 * nb * cb), F32),
            pltpu.VMEM((lb, 2 * nb * cb), F32),
        ],
        compiler_params=pltpu.CompilerParams(
            dimension_semantics=("arbitrary",), vmem_limit_bytes=FILTER_VMEM),
        name="filter_spectra",
    )(z_pad, w1_pad, b1, w2, b2, w3, b3, freq, w4, w4, deltas, cs)


def _conv3(p, w, bias, lo, n):
    n_rows = p.shape[0]
    assert 1 <= lo and lo + n < n_rows
    dn = pltpu.roll(p, 1, 0)
    up = pltpu.roll(p, n_rows - 1, 0)
    out = dn * w[0:1] + p * w[1:2] + up * w[2:3] + bias
    return out[lo:lo + n]


def _window_mean_minus_self(p, win, lo, n, at_start, at_end):
    n_rows = p.shape[0]
    half = win // 2
    sl = SUBLANES
    assert lo >= win and lo + n + win <= n_rows and half <= sl
    s = p
    w = 1
    while w < win:
        s = s + pltpu.roll(s, w, 0)
        w *= 2
    if half > 1:
        s = pltpu.roll(s, n_rows - (half - 1), 0)
    total = s[lo:lo + n]
    edge = lax.broadcasted_iota(jnp.int32, (sl, p.shape[1]), 0)
    cnt_first = jnp.where(at_start > 0, jnp.minimum(edge + half, win), win).astype(F32)
    cnt_last = jnp.where(at_end > 0, jnp.minimum(sl - edge + half, win), win).astype(F32)
    mean = jnp.concatenate([total[0:sl] / cnt_first,
                            total[sl:n - sl] * (1.0 / win),
                            total[n - sl:] / cnt_last], axis=0)
    return mean - p[lo:lo + n]


def _inproj_kernel(xp_ref, x_ref, xn_ref, g_ref, w_ref, b_ref, cw_ref, cb_ref,
                   pw_ref, pb_ref, ps_ref, u_ref, gate_ref, yp_ref, p_ref, *, chunks_per_seq):
    i = pl.program_id(0)
    rows = x_ref.shape[0]
    halo = xp_ref.shape[0]
    dh = u_ref.shape[1]
    cb = MXU_DIM
    keep_prev = (i % chunks_per_seq != 0).astype(F32)
    keep_next = (i % chunks_per_seq != chunks_per_seq - 1).astype(F32)

    xw = jnp.concatenate([xp_ref[...], x_ref[...], xn_ref[...]], axis=0)
    h = (_rms_scale(xw) * g_ref[...]).astype(BF16)

    h_main = h[halo:halo + rows]

    def project(rows_h, c0):
        return _dot(rows_h, w_ref[:, c0:c0 + cb]) + b_ref[:, c0:c0 + cb]

    def conv_input(c0):
        p = project(h, c0)
        return jnp.concatenate(
            [p[0:halo] * keep_prev, p[halo:halo + rows], p[halo + rows:] * keep_next], axis=0)

    def conv(c0):
        return _conv3(conv_input(c0), cw_ref[:, c0:c0 + cb], cb_ref[:, c0:c0 + cb], halo, rows)

    dp = yp_ref.shape[1]
    assert dp == len(POOL_WINDOWS) * cb
    def hyena_group(c):
        lanes = slice(c * cb, (c + 1) * cb)
        u_ref[:, lanes] = conv(dh + c * cb) * conv(2 * dh + c * cb)
        z = project(h_main, 3 * dh + c * cb)
        gate_ref[:, lanes] = (conv(c * cb) * jax.nn.silu(z)).astype(BF16)

    def pool_group(g, win):
        lanes = slice(g * cb, (g + 1) * cb)
        pooled = _window_mean_minus_self(conv_input(4 * dh + g * cb), win, halo, rows,
                                         1.0 - keep_prev, 1.0 - keep_next)
        y = _dot(pooled.astype(BF16), pw_ref[g]) + pb_ref[:, lanes]
        z = project(h_main, 4 * dh + dp + g * cb)
        yp_ref[:, lanes] = (y * ps_ref[:, lanes] * jax.nn.silu(z)).astype(BF16)

    for c in range(dh // cb):
        hyena_group(c)
    for g, win in enumerate(POOL_WINDOWS):
        pool_group(g, win)

    for c0 in range(0, p_ref.shape[1], cb):
        p_ref[:, c0:c0 + cb] = project(h_main, 4 * dh + 2 * dp + c0).astype(BF16)


def _inproj(x2, g_norm, w_in_bf, b_in, conv_w, conv_b, pool_w_bf, pool_b, pool_scale, seq):
    n_tok, d = x2.shape
    cols = w_in_bf.shape[1]
    dh = conv_b.shape[1] // 3
    dp = pool_b.shape[1]
    n_gate = cols - 4 * dh - 2 * dp
    rows = ROW_CHUNK
    halo = 2 * SUBLANES
    per_halo = rows // halo
    last_halo = n_tok // halo - 1
    full = lambda a: pl.BlockSpec(a.shape, lambda i: (0,) * a.ndim)
    tile = lambda width: pl.BlockSpec((rows, width), lambda i: (i, 0))
    return pl.pallas_call(
        functools.partial(_inproj_kernel, chunks_per_seq=seq // rows),
        grid=(n_tok // rows,),
        in_specs=[
            pl.BlockSpec((halo, d), lambda i: (jnp.maximum(i * per_halo - 1, 0), 0)),
            tile(d),
            pl.BlockSpec((halo, d), lambda i: (jnp.minimum((i + 1) * per_halo, last_halo), 0)),
            full(g_norm),
            pl.BlockSpec(w_in_bf.shape, lambda i: (0, 0), pipeline_mode=pl.Buffered(1)),
            full(b_in), full(conv_w), full(conv_b),
            full(pool_w_bf), full(pool_b), full(pool_scale),
        ],
        out_specs=[tile(dh), tile(dh), tile(dp), tile(n_gate)],
        out_shape=[
            jax.ShapeDtypeStruct((n_tok, dh), F32),
            jax.ShapeDtypeStruct((n_tok, dh), BF16),
            jax.ShapeDtypeStruct((n_tok, dp), BF16),
            jax.ShapeDtypeStruct((n_tok, n_gate), BF16),
        ],
        compiler_params=pltpu.CompilerParams(
            dimension_semantics=("parallel",), vmem_limit_bytes=INPROJ_VMEM),
        name="inproj",
    )(x2, x2, x2, g_norm, w_in_bf, b_in, conv_w, conv_b, pool_w_bf, pool_b, pool_scale)


def _hyena_kernel(u_ref, gate_ref, d_ref, kre_ref, kim_ref, cs_ref, y_ref, av_all, acc_all, pq_all,
                  *, seq):
    for q in range(u_ref.shape[0] // seq):
        _hyena_sequence(q * seq, seq, u_ref, gate_ref, d_ref, kre_ref, kim_ref, cs_ref, y_ref,
                        av_all.at[q], acc_all.at[q], pq_all.at[q])


def _hyena_sequence(row0, seq, u_ref, gate_ref, d_ref, kre_ref, kim_ref, cs_ref, y_ref,
                    av_ref, acc_ref, pq_ref):
    lb = cs_ref.shape[0]
    nb = seq // lb
    mid = nb - 1
    blocks = [slice(row0 + b * lb, row0 + (b + 1) * lb) for b in range(nb)]
    chunks = [slice(r0, r0 + SPEC_ROWS) for r0 in range(0, lb, SPEC_ROWS)]

    def forward(b):
        ub = u_ref[blocks[b], :].astype(BF16)
        av_ref[2 * b] = _dot(cs_ref[:, 0:lb], ub).astype(av_ref.dtype)
        av_ref[2 * b + 1] = _dot(cs_ref[:, lb:2 * lb], ub).astype(av_ref.dtype)

    def spectrum(b, rows):
        return av_ref[2 * b, rows, :], av_ref[2 * b + 1, rows, :]

    def products(a, b, rows, ur, vi):
        kr = kre_ref[mid + a - b, rows, :]
        ki = kim_ref[mid + a - b, rows, :]
        return kr * ur + ki * vi, kr * vi - ki * ur

    def accumulate(b):
        for rows in chunks:
            ur, vi = spectrum(b, rows)
            for a in range(nb):
                tp, tq = products(a, b, rows, ur, vi)
                if b == 0:
                    acc_ref[2 * a, rows, :] = tp
                    acc_ref[2 * a + 1, rows, :] = tq
                else:
                    acc_ref[2 * a, rows, :] += tp
                    acc_ref[2 * a + 1, rows, :] += tq

    def finish(a):
        for rows in chunks:
            tp, tq = products(a, nb - 1, rows, *spectrum(nb - 1, rows))
            pq_ref[a, rows, :] = (acc_ref[2 * a, rows, :] + tp).astype(BF16)
            pq_ref[a, lb + rows.start:lb + rows.stop, :] = (
                acc_ref[2 * a + 1, rows, :] + tq).astype(BF16)

    def inverse(a):
        rows = blocks[a]
        y = _dot(cs_ref[...], pq_ref[a])
        y = y + u_ref[rows, :] * d_ref[...]
        y_ref[rows, :] = (y * gate_ref[rows, :].astype(F32)).astype(BF16)

    forward(0)
    for b in range(nb - 1):
        forward(b + 1)
        accumulate(b)
    for a in range(nb):
        finish(a)
        if a > 0:
            inverse(a - 1)
    inverse(nb - 1)


def _hyena(u, gate, hyena_d, kre, kim, cs, batch, seq):
    dh = hyena_d.shape[1]
    cw = HYENA_COLS
    lb = cs.shape[0]
    nb = seq // lb
    n_seqs = HYENA_SEQS
    act = pl.BlockSpec((n_seqs * seq, cw), lambda c, b: (b, c))
    spectra = pl.BlockSpec((2 * nb - 1, lb, cw), lambda c, b: (0, 0, c),
                           pipeline_mode=pl.Buffered(1))
    return pl.pallas_call(
        functools.partial(_hyena_kernel, seq=seq),
        grid=(dh // cw, batch // n_seqs),
        in_specs=[
            act, act,
            pl.BlockSpec((1, cw), lambda c, b: (0, c)),
            spectra, spectra,
            pl.BlockSpec(cs.shape, lambda c, b: (0, 0)),
        ],
        out_specs=act,
        out_shape=jax.ShapeDtypeStruct((batch * seq, dh), BF16),
        scratch_shapes=[
            pltpu.VMEM((n_seqs, 2 * nb, lb, cw), SPEC_DTYPE),
            pltpu.VMEM((n_seqs, 2 * nb, lb, cw), SPEC_DTYPE),
            pltpu.VMEM((n_seqs, nb, 2 * lb, cw), BF16),
        ],
        compiler_params=pltpu.CompilerParams(
            dimension_semantics=("arbitrary", "arbitrary"), vmem_limit_bytes=HYENA_VMEM),
        name="hyena",
    )(u, gate, hyena_d, kre, kim, cs)


def _merge_kernel(yh_ref, yp_ref, g0_ref, g1_ref, x_ref, wh_ref, wp_ref, wo_ref, gf_ref, o_ref):
    out_h = _dot(yh_ref[...], wh_ref[...])
    out_p = _dot(yp_ref[...], wp_ref[...])
    merged = (jax.nn.sigmoid(g0_ref[...].astype(F32)) * out_h
              + jax.nn.sigmoid(g1_ref[...].astype(F32)) * out_p)
    res = x_ref[...] + _dot(merged.astype(BF16), wo_ref[...])
    o_ref[...] = _rms_scale(res) * gf_ref[...]


def _merge(yh, yp, gates, x2, wh_bf, wp_bf, wo_bf, g_final, tm=1024):
    n_tok, d = x2.shape
    tile = lambda col: pl.BlockSpec((tm, d), lambda i: (i, col))
    resident = lambda a: pl.BlockSpec(a.shape, lambda i: (0, 0), pipeline_mode=pl.Buffered(1))
    return pl.pallas_call(
        _merge_kernel,
        grid=(n_tok // tm,),
        in_specs=[
            tile(0), tile(0), tile(0), tile(1), tile(0),
            resident(wh_bf), resident(wp_bf), resident(wo_bf),
            pl.BlockSpec((1, d), lambda i: (0, 0)),
        ],
        out_specs=tile(0),
        out_shape=jax.ShapeDtypeStruct((n_tok, d), F32),
        compiler_params=pltpu.CompilerParams(
            dimension_semantics=("parallel",), vmem_limit_bytes=MERGE_VMEM),
        name="merge",
    )(yh, yp, gates, gates, x2, wh_bf, wp_bf, wo_bf, g_final)


def _filter_features(seq):
    t = jnp.linspace(0.0, 1.0, seq, dtype=F32)[:, None]
    bands = (FILTER_EMB - 1) // 2
    w = 2.0 * math.pi * jnp.arange(seq, dtype=F32) / seq
    f = jnp.linspace(1e-4, bands - 1, bands, dtype=F32)
    ang = w[:, None] * f[None, :]
    return jnp.concatenate([t, jnp.cos(ang), -jnp.sin(ang)], axis=-1)


def kernel(x, g_norm, w_in, b_in, conv_w, conv_b, filt_w1, filt_b1, filt_w2, filt_b2, filt_w3, filt_b3, filt_w4, filt_freq, hyena_d, w_hyena_out, pool_w, pool_b, pool_scale, w_pool_out, w_out, g_final):
    batch, seq, d = x.shape
    dh = hyena_d.shape[1]
    dp = pool_b.shape[1]
    assert g_norm.shape[0] == 1, "single-layer block"
    assert d == dh == dp and conv_b.shape[1] == 3 * dh and w_in.shape[2] == 4 * dh + 2 * dp + 2 * d
    assert dh % HYENA_COLS == 0 and batch % HYENA_SEQS == 0 and seq % ROW_CHUNK == 0
    assert seq % CONV_BLOCKS == 0 and (seq // CONV_BLOCKS) % SPEC_ROWS == 0

    cs = _transform_matrices(seq // CONV_BLOCKS)
    z_pad = jnp.pad(_filter_features(seq), ((0, 0), (0, LANES - FILTER_EMB)))
    w1_pad = jnp.pad(filt_w1[0], ((0, LANES - FILTER_EMB), (0, 0)))
    deltas = jnp.abs(jnp.linspace(MIN_DECAY, MAX_DECAY, dh, dtype=F32))[None, :]
    x2 = x.reshape(batch * seq, d)
    u, gate, yp, gates = _inproj(x2, g_norm, w_in[0].astype(BF16), b_in, conv_w[0], conv_b,
                                 pool_w[0].astype(BF16), pool_b, pool_scale, seq)
    kre, kim = _filter_spectra(z_pad, w1_pad, filt_b1, filt_w2[0], filt_b2, filt_w3[0], filt_b3,
                               filt_freq, filt_w4[0], deltas, cs)
    yh = _hyena(u, gate, hyena_d, kre, kim, cs, batch, seq)
    out = _merge(yh, yp, gates, x2, w_hyena_out[0].astype(BF16), w_pool_out[0].astype(BF16),
                 w_out[0].astype(BF16), g_final[None, :])
    return out.reshape(batch, seq, d)
```

```python
import functools
import math

import jax
import jax.numpy as jnp
from jax import lax
from jax.experimental import pallas as pl
from jax.experimental.pallas import tpu as pltpu

FILTER_EMB = 33
FAST_DECAY_PCT = 0.3
SLOW_DECAY_PCT = 1.5
DECAY_TARGET = 1e-2
MAX_DECAY = math.log(DECAY_TARGET) / FAST_DECAY_PCT
MIN_DECAY = math.log(DECAY_TARGET) / SLOW_DECAY_PCT
POOL_WINDOWS = (2, 4, 8, 16)
NORM_EPS = 1e-6

LANES = 128
SUBLANES = 8
MXU_DIM = 256
HYENA_COLS = MXU_DIM
ROW_CHUNK = 1024
MLP_ROWS = 256
CONV_BLOCKS = 4
SPEC_ROWS = 64
MIB = 1024 * 1024
FILTER_VMEM = 40 * MIB
INPROJ_VMEM = 55 * MIB
HYENA_VMEM = 24 * MIB
MERGE_VMEM = 47 * MIB

F32 = jnp.float32
BF16 = jnp.bfloat16
SPEC_DTYPE = BF16


def _dot(a, b):
    return jnp.dot(a, b, preferred_element_type=F32)


def _dot_hi(a, b):
    return jnp.dot(a, b, preferred_element_type=F32, precision=lax.Precision.HIGHEST)


def _dot_split(a, b):
    a_hi, b_hi = a.astype(BF16), b.astype(BF16)
    a_lo = (a - a_hi.astype(F32)).astype(BF16)
    b_lo = (b - b_hi.astype(F32)).astype(BF16)
    return _dot(a_hi, b_hi) + _dot(a_lo, b_hi) + _dot(a_hi, b_lo)


def _rms_scale(x):
    return x * lax.rsqrt(jnp.mean(x * x, axis=-1, keepdims=True) + NORM_EPS)


def _transform_matrices(lb):
    period = 8 * lb
    odd = 2 * jnp.arange(lb, dtype=jnp.int32) + 1
    ang = ((odd[:, None] * odd[None, :]) % period).astype(F32) * (2.0 * math.pi / period)
    return jnp.concatenate([jnp.cos(ang), jnp.sin(ang)], axis=1).astype(BF16)


def _filter_kernel(z_ref, w1_ref, b1_ref, w2_ref, b2_ref, w3_ref, b3_ref, fr_ref,
                   w4f_ref, w4b_ref, dl_ref, cs_ref, kre_ref, kim_ref,
                   h_ref, cphi_ref, sphi_ref, sgn_ref, kk_ref, xr_ref, xi_ref):
    seq = z_ref.shape[0]
    lb = cs_ref.shape[0]
    nb = seq // lb
    cb = dl_ref.shape[1]
    scale = 1.0 / lb

    @pl.when(pl.program_id(0) == 0)
    def _():
        fr = fr_ref[...]

        def mlp(i, carry):
            rows = pl.ds(pl.multiple_of(i * MLP_ROWS, MLP_ROWS), MLP_ROWS)
            h = jnp.sin(fr * (_dot_hi(z_ref[rows, :], w1_ref[...]) + b1_ref[...]))
            h = jnp.sin(fr * (_dot_hi(h, w2_ref[...]) + b2_ref[...]))
            h_ref[rows, :] = jnp.sin(fr * (_dot_hi(h, w3_ref[...]) + b3_ref[...]))
            return carry

        lax.fori_loop(0, seq // MLP_ROWS, mlp, 0)
        f_idx = lax.broadcasted_iota(jnp.int32, cphi_ref.shape, 0)
        phi = (f_idx.astype(F32) + 0.5) * (math.pi / (2 * lb))
        cphi_ref[...] = jnp.cos(phi) * scale
        sphi_ref[...] = jnp.sin(phi) * scale
        sgn_ref[...] = (1 - 2 * (f_idx & 1)).astype(F32)

    h = h_ref[...]
    decay = jnp.exp(-z_ref[:, 0:1] * dl_ref[...])
    taps = (_dot_split(h, w4f_ref[...]) * decay, _dot_split(h, w4b_ref[...]) * decay)
    taps = tuple(k.astype(BF16) for k in taps)
    for side, k in enumerate(taps):
        for s_ in range(nb):
            c0 = (side * nb + s_) * cb
            kk_ref[:, c0:c0 + cb] = k[s_ * lb:(s_ + 1) * lb]

    group = 2 * cb
    tile = lambda ref, width: jnp.concatenate([ref[...]] * (width // LANES), axis=1)
    c_phi, s_phi = tile(cphi_ref, group), tile(sphi_ref, group)
    for g in range(2 * nb * cb // group):
        cols = slice(g * group, (g + 1) * group)
        kk = kk_ref[:, cols]
        a = _dot(cs_ref[:, 0:lb], kk)
        b = _dot(cs_ref[:, lb:2 * lb], kk)
        xr_ref[:, cols] = c_phi * a + s_phi * b
        xi_ref[:, cols] = s_phi * a - c_phi * b

    sgn = tile(sgn_ref, cb)

    def seg(ref, side, s_):
        c0 = (side * nb + s_) * cb
        return ref[:, c0:c0 + cb]

    def causal_block(side, delta):
        r0 = (delta - 1) * lb
        first = taps[side][r0:r0 + 1].astype(F32) * scale
        re = seg(xr_ref, side, delta) - sgn * seg(xi_ref, side, delta - 1)
        im = seg(xi_ref, side, delta) + sgn * (seg(xr_ref, side, delta - 1) - first)
        return re, im

    mid = nb - 1
    out = kre_ref.dtype
    kre_ref[mid] = (seg(xr_ref, 0, 0) + seg(xr_ref, 1, 0)).astype(out)
    kim_ref[mid] = (seg(xi_ref, 0, 0) - seg(xi_ref, 1, 0)).astype(out)
    for delta in range(1, nb):
        re, im = causal_block(0, delta)
        kre_ref[mid + delta] = re.astype(out)
        kim_ref[mid + delta] = im.astype(out)
        re, im = causal_block(1, delta)
        kre_ref[mid - delta] = re.astype(out)
        kim_ref[mid - delta] = (-im).astype(out)


def _filter_spectra(z_pad, w1_pad, b1, w2, b2, w3, b3, freq, w4, deltas, cs):
    seq = z_pad.shape[0]
    lb = cs.shape[0]
    nb = seq // lb
    hf = w2.shape[0]
    dh = deltas.shape[1]
    cb = MXU_DIM
    nblk = dh // cb
    full = lambda a: pl.BlockSpec(a.shape, lambda c: (0,) * a.ndim)
    return pl.pallas_call(
        _filter_kernel,
        grid=(nblk,),
        in_specs=[
            full(z_pad), full(w1_pad), full(b1), full(w2), full(b2), full(w3), full(b3), full(freq),
            pl.BlockSpec((hf, cb), lambda c: (0, c)),
            pl.BlockSpec((hf, cb), lambda c: (0, nblk + c)),
            pl.BlockSpec((1, cb), lambda c: (0, c)),
            full(cs),
        ],
        out_specs=[pl.BlockSpec((2 * nb - 1, lb, cb), lambda c: (0, 0, c))] * 2,
        out_shape=[jax.ShapeDtypeStruct((2 * nb - 1, lb, dh), SPEC_DTYPE)] * 2,
        scratch_shapes=[
            pltpu.VMEM((seq, hf), F32),
            pltpu.VMEM((lb, LANES), F32),
            pltpu.VMEM((lb, LANES), F32),
            pltpu.VMEM((lb, LANES), F32),
            pltpu.VMEM((lb, 2 * nb * cb), BF16),
            pltpu.VMEM((lb, 2 * nb * cb), F32),
            pltpu.VMEM((lb, 2 * nb * cb), F32),
        ],
        compiler_params=pltpu.CompilerParams(
            dimension_semantics=("arbitrary",), vmem_limit_bytes=FILTER_VMEM),
        name="filter_spectra",
    )(z_pad, w1_pad, b1, w2, b2, w3, b3, freq, w4, w4, deltas, cs)


def _conv3(p, w, bias, lo, n):
    n_rows = p.shape[0]
    assert 1 <= lo and lo + n < n_rows
    dn = pltpu.roll(p, 1, 0)
    up = pltpu.roll(p, n_rows - 1, 0)
    out = dn * w[0:1] + p * w[1:2] + up * w[2:3] + bias
    return out[lo:lo + n]


def _window_mean_minus_self(p, win, lo, n, at_start, at_end):
    n_rows = p.shape[0]
    half = win // 2
    sl = SUBLANES
    assert lo >= win and lo + n + win <= n_rows and half <= sl
    s = p
    w = 1
    while w < win:
        s = s + pltpu.roll(s, w, 0)
        w *= 2
    if half > 1:
        s = pltpu.roll(s, n_rows - (half - 1), 0)
    total = s[lo:lo + n]
    edge = lax.broadcasted_iota(jnp.int32, (sl, p.shape[1]), 0)
    cnt_first = jnp.where(at_start, jnp.minimum(edge + half, win), win).astype(F32)
    cnt_last = jnp.where(at_end, jnp.minimum(sl - edge + half, win), win).astype(F32)
    mean = jnp.concatenate([total[0:sl] / cnt_first,
                            total[sl:n - sl] * (1.0 / win),
                            total[n - sl:] / cnt_last], axis=0)
    return mean - p[lo:lo + n]


def _inproj_kernel(xp_ref, x_ref, xn_ref, g_ref, w_ref, b_ref, cw_ref, cb_ref,
                   pw_ref, pb_ref, ps_ref, u_ref, gate_ref, yp_ref, p_ref, *, chunks_per_seq):
    i = pl.program_id(0)
    rows = x_ref.shape[0]
    halo = xp_ref.shape[0]
    dh = u_ref.shape[1]
    cb = MXU_DIM
    at_start = i % chunks_per_seq == 0
    at_end = i % chunks_per_seq == chunks_per_seq - 1

    xw = jnp.concatenate([xp_ref[...], x_ref[...], xn_ref[...]], axis=0)
    h = (_rms_scale(xw) * g_ref[...]).astype(BF16)

    h_main = h[halo:halo + rows]

    def project(rows_h, c0):
        return _dot(rows_h, w_ref[:, c0:c0 + cb]) + b_ref[:, c0:c0 + cb]

    def conv_input(c0):
        p = project(h, c0)
        return jnp.concatenate(
            [jnp.where(at_start, 0.0, p[0:halo]), p[halo:halo + rows],
             jnp.where(at_end, 0.0, p[halo + rows:])], axis=0)

    def conv(c0):
        return _conv3(conv_input(c0), cw_ref[:, c0:c0 + cb], cb_ref[:, c0:c0 + cb], halo, rows)

    dp = yp_ref.shape[1]
    assert dp == len(POOL_WINDOWS) * cb
    def hyena_group(c):
        lanes = slice(c * cb, (c + 1) * cb)
        u_ref[:, lanes] = conv(dh + c * cb) * conv(2 * dh + c * cb)
        z = project(h_main, 3 * dh + c * cb)
        gate_ref[:, lanes] = (conv(c * cb) * jax.nn.silu(z)).astype(BF16)

    def pool_group(g, win):
        lanes = slice(g * cb, (g + 1) * cb)
        pooled = _window_mean_minus_self(conv_input(4 * dh + g * cb), win, halo, rows,
                                         at_start, at_end)
        y = _dot(pooled.astype(BF16), pw_ref[g]) + pb_ref[:, lanes]
        z = project(h_main, 4 * dh + dp + g * cb)
        yp_ref[:, lanes] = (y * ps_ref[:, lanes] * jax.nn.silu(z)).astype(BF16)

    for c in range(dh // cb):
        hyena_group(c)
    for g, win in enumerate(POOL_WINDOWS):
        pool_group(g, win)

    for c0 in range(0, p_ref.shape[1], cb):
        p_ref[:, c0:c0 + cb] = project(h_main, 4 * dh + 2 * dp + c0).astype(BF16)


def _inproj(x2, g_norm, w_in_bf, b_in, conv_w, conv_b, pool_w_bf, pool_b, pool_scale, seq):
    n_tok, d = x2.shape
    cols = w_in_bf.shape[1]
    dh = conv_b.shape[1] // 3
    dp = pool_b.shape[1]
    n_gate = cols - 4 * dh - 2 * dp
    rows = ROW_CHUNK
    halo = 2 * SUBLANES
    per_halo = rows // halo
    last_halo = n_tok // halo - 1
    full = lambda a: pl.BlockSpec(a.shape, lambda i: (0,) * a.ndim)
    tile = lambda width: pl.BlockSpec((rows, width), lambda i: (i, 0))
    return pl.pallas_call(
        functools.partial(_inproj_kernel, chunks_per_seq=seq // rows),
        grid=(n_tok // rows,),
        in_specs=[
            pl.BlockSpec((halo, d), lambda i: (jnp.maximum(i * per_halo - 1, 0), 0)),
            tile(d),
            pl.BlockSpec((halo, d), lambda i: (jnp.minimum((i + 1) * per_halo, last_halo), 0)),
            full(g_norm),
            pl.BlockSpec(w_in_bf.shape, lambda i: (0, 0), pipeline_mode=pl.Buffered(1)),
            full(b_in), full(conv_w), full(conv_b),
            full(pool_w_bf), full(pool_b), full(pool_scale),
        ],
        out_specs=[tile(dh), tile(dh), tile(dp), tile(n_gate)],
        out_shape=[
            jax.ShapeDtypeStruct((n_tok, dh), F32),
            jax.ShapeDtypeStruct((n_tok, dh), BF16),
            jax.ShapeDtypeStruct((n_tok, dp), BF16),
            jax.ShapeDtypeStruct((n_tok, n_gate), BF16),
        ],
        compiler_params=pltpu.CompilerParams(
            dimension_semantics=("parallel",), vmem_limit_bytes=INPROJ_VMEM),
        name="inproj",
    )(x2, x2, x2, g_norm, w_in_bf, b_in, conv_w, conv_b, pool_w_bf, pool_b, pool_scale)


def _hyena_kernel(u_ref, gate_ref, d_ref, kre_ref, kim_ref, cs_ref, y_ref, av_ref, acc_ref, pq_ref):
    seq = u_ref.shape[0]
    lb = cs_ref.shape[0]
    nb = seq // lb
    mid = nb - 1
    blocks = [slice(b * lb, (b + 1) * lb) for b in range(nb)]
    chunks = [slice(r0, r0 + SPEC_ROWS) for r0 in range(0, lb, SPEC_ROWS)]

    def forward(b):
        ub = u_ref[blocks[b], :].astype(BF16)
        av_ref[2 * b] = _dot(cs_ref[:, 0:lb], ub).astype(av_ref.dtype)
        av_ref[2 * b + 1] = _dot(cs_ref[:, lb:2 * lb], ub).astype(av_ref.dtype)

    def spectrum(b, rows):
        return av_ref[2 * b, rows, :], av_ref[2 * b + 1, rows, :]

    def products(a, b, rows, ur, vi):
        kr = kre_ref[mid + a - b, rows, :]
        ki = kim_ref[mid + a - b, rows, :]
        return kr * ur + ki * vi, kr * vi - ki * ur

    def accumulate(b):
        for rows in chunks:
            ur, vi = spectrum(b, rows)
            for a in range(nb):
                tp, tq = products(a, b, rows, ur, vi)
                if b == 0:
                    acc_ref[2 * a, rows, :] = tp
                    acc_ref[2 * a + 1, rows, :] = tq
                else:
                    acc_ref[2 * a, rows, :] += tp
                    acc_ref[2 * a + 1, rows, :] += tq

    def finish(a):
        for rows in chunks:
            tp, tq = products(a, nb - 1, rows, *spectrum(nb - 1, rows))
            pq_ref[a, rows, :] = (acc_ref[2 * a, rows, :] + tp).astype(BF16)
            pq_ref[a, lb + rows.start:lb + rows.stop, :] = (
                acc_ref[2 * a + 1, rows, :] + tq).astype(BF16)

    def inverse(a):
        rows = blocks[a]
        y = _dot(cs_ref[...], pq_ref[a])
        y = y + u_ref[rows, :] * d_ref[...]
        y_ref[rows, :] = (y * gate_ref[rows, :].astype(F32)).astype(BF16)

    forward(0)
    for b in range(nb - 1):
        forward(b + 1)
        accumulate(b)
    for a in range(nb):
        finish(a)
        if a > 0:
            inverse(a - 1)
    inverse(nb - 1)


def _hyena(u, gate, hyena_d, kre, kim, cs, batch, seq):
    dh = hyena_d.shape[1]
    cw = HYENA_COLS
    lb = cs.shape[0]
    nb = seq // lb
    act = pl.BlockSpec((seq, cw), lambda c, b: (b, c))
    spectra = pl.BlockSpec((2 * nb - 1, lb, cw), lambda c, b: (0, 0, c),
                           pipeline_mode=pl.Buffered(1))
    return pl.pallas_call(
        _hyena_kernel,
        grid=(dh // cw, batch),
        in_specs=[
            act, act,
            pl.BlockSpec((1, cw), lambda c, b: (0, c)),
            spectra, spectra,
            pl.BlockSpec(cs.shape, lambda c, b: (0, 0)),
        ],
        out_specs=act,
        out_shape=jax.ShapeDtypeStruct((batch * seq, dh), BF16),
        scratch_shapes=[
            pltpu.VMEM((2 * nb, lb, cw), SPEC_DTYPE),
            pltpu.VMEM((2 * nb, lb, cw), SPEC_DTYPE),
            pltpu.VMEM((nb, 2 * lb, cw), BF16),
        ],
        compiler_params=pltpu.CompilerParams(
            dimension_semantics=("arbitrary", "arbitrary"), vmem_limit_bytes=HYENA_VMEM),
        name="hyena",
    )(u, gate, hyena_d, kre, kim, cs)


def _merge_kernel(yh_ref, yp_ref, g0_ref, g1_ref, x_ref, wh_ref, wp_ref, wo_ref, gf_ref, o_ref):
    out_h = _dot(yh_ref[...], wh_ref[...])
    out_p = _dot(yp_ref[...], wp_ref[...])
    merged = (jax.nn.sigmoid(g0_ref[...].astype(F32)) * out_h
              + jax.nn.sigmoid(g1_ref[...].astype(F32)) * out_p)
    res = x_ref[...] + _dot(merged.astype(BF16), wo_ref[...])
    o_ref[...] = _rms_scale(res) * gf_ref[...]


def _merge(yh, yp, gates, x2, wh_bf, wp_bf, wo_bf, g_final, tm=1024):
    n_tok, d = x2.shape
    tile = lambda col: pl.BlockSpec((tm, d), lambda i: (i, col))
    resident = lambda a: pl.BlockSpec(a.shape, lambda i: (0, 0), pipeline_mode=pl.Buffered(1))
    return pl.pallas_call(
        _merge_kernel,
        grid=(n_tok // tm,),
        in_specs=[
            tile(0), tile(0), tile(0), tile(1), tile(0),
            resident(wh_bf), resident(wp_bf), resident(wo_bf),
            pl.BlockSpec((1, d), lambda i: (0, 0)),
        ],
        out_specs=tile(0),
        out_shape=jax.ShapeDtypeStruct((n_tok, d), F32),
        compiler_params=pltpu.CompilerParams(
            dimension_semantics=("parallel",), vmem_limit_bytes=MERGE_VMEM),
        name="merge",
    )(yh, yp, gates, gates, x2, wh_bf, wp_bf, wo_bf, g_final)


def _filter_features(seq):
    t = jnp.linspace(0.0, 1.0, seq, dtype=F32)[:, None]
    bands = (FILTER_EMB - 1) // 2
    w = 2.0 * math.pi * jnp.arange(seq, dtype=F32) / seq
    f = jnp.linspace(1e-4, bands - 1, bands, dtype=F32)
    ang = w[:, None] * f[None, :]
    return jnp.concatenate([t, jnp.cos(ang), -jnp.sin(ang)], axis=-1)


def kernel(x, g_norm, w_in, b_in, conv_w, conv_b, filt_w1, filt_b1, filt_w2, filt_b2, filt_w3, filt_b3, filt_w4, filt_freq, hyena_d, w_hyena_out, pool_w, pool_b, pool_scale, w_pool_out, w_out, g_final):
    batch, seq, d = x.shape
    dh = hyena_d.shape[1]
    dp = pool_b.shape[1]
    assert g_norm.shape[0] == 1, "single-layer block"
    assert d == dh == dp and conv_b.shape[1] == 3 * dh and w_in.shape[2] == 4 * dh + 2 * dp + 2 * d
    assert dh % HYENA_COLS == 0 and seq % ROW_CHUNK == 0
    assert seq % CONV_BLOCKS == 0 and (seq // CONV_BLOCKS) % SPEC_ROWS == 0

    cs = _transform_matrices(seq // CONV_BLOCKS)
    z_pad = jnp.pad(_filter_features(seq), ((0, 0), (0, LANES - FILTER_EMB)))
    w1_pad = jnp.pad(filt_w1[0], ((0, LANES - FILTER_EMB), (0, 0)))
    deltas = jnp.abs(jnp.linspace(MIN_DECAY, MAX_DECAY, dh, dtype=F32))[None, :]
    x2 = x.reshape(batch * seq, d)
    u, gate, yp, gates = _inproj(x2, g_norm, w_in[0].astype(BF16), b_in, conv_w[0], conv_b,
                                 pool_w[0].astype(BF16), pool_b, pool_scale, seq)
    kre, kim = _filter_spectra(z_pad, w1_pad, filt_b1, filt_w2[0], filt_b2, filt_w3[0], filt_b3,
                               filt_freq, filt_w4[0], deltas, cs)
    yh = _hyena(u, gate, hyena_d, kre, kim, cs, batch, seq)
    out = _merge(yh, yp, gates, x2, w_hyena_out[0].astype(BF16), w_pool_out[0].astype(BF16),
                 w_out[0].astype(BF16), g_final[None, :])
    return out.reshape(batch, seq, d)
```

```python
import functools
import math

import jax
import jax.numpy as jnp
from jax import lax
from jax.experimental import pallas as pl
from jax.experimental.pallas import tpu as pltpu

FILTER_EMB = 33
FAST_DECAY_PCT = 0.3
SLOW_DECAY_PCT = 1.5
DECAY_TARGET = 1e-2
MAX_DECAY = math.log(DECAY_TARGET) / FAST_DECAY_PCT
MIN_DECAY = math.log(DECAY_TARGET) / SLOW_DECAY_PCT
POOL_WINDOWS = (2, 4, 8, 16)
NORM_EPS = 1e-6

LANES = 128
SUBLANES = 8
MXU_DIM = 256
HYENA_COLS = MXU_DIM
ROW_CHUNK = 1024
MLP_ROWS = 256
CONV_BLOCKS = 4
SPEC_ROWS = 64
MIB = 1024 * 1024
FILTER_VMEM = 40 * MIB
INPROJ_VMEM = 55 * MIB
HYENA_VMEM = 24 * MIB
MERGE_VMEM = 47 * MIB

F32 = jnp.float32
BF16 = jnp.bfloat16
SPEC_DTYPE = BF16


def _dot(a, b):
    return jnp.dot(a, b, preferred_element_type=F32)


def _dot_split(a, b):
    a_hi, b_hi = a.astype(BF16), b.astype(BF16)
    a_lo = (a - a_hi.astype(F32)).astype(BF16)
    b_lo = (b - b_hi.astype(F32)).astype(BF16)
    return _dot(a_hi, b_hi) + _dot(a_lo, b_hi) + _dot(a_hi, b_lo)


def _rms_scale(x):
    return x * lax.rsqrt(jnp.mean(x * x, axis=-1, keepdims=True) + NORM_EPS)


def _transform_matrices(lb):
    period = 8 * lb
    odd = 2 * jnp.arange(lb, dtype=jnp.int32) + 1
    ang = ((odd[:, None] * odd[None, :]) % period).astype(F32) * (2.0 * math.pi / period)
    return jnp.concatenate([jnp.cos(ang), jnp.sin(ang)], axis=1).astype(BF16)


def _filter_kernel(z_ref, w1_ref, b1_ref, w2_ref, b2_ref, w3_ref, b3_ref, fr_ref,
                   w4f_ref, w4b_ref, dl_ref, cs_ref, kre_ref, kim_ref,
                   h_ref, cphi_ref, sphi_ref, sgn_ref, kk_ref, xr_ref, xi_ref):
    seq = z_ref.shape[0]
    lb = cs_ref.shape[0]
    nb = seq // lb
    cb = dl_ref.shape[1]
    scale = 1.0 / lb

    @pl.when(pl.program_id(0) == 0)
    def _():
        fr = fr_ref[...]

        def mlp(i, carry):
            rows = pl.ds(pl.multiple_of(i * MLP_ROWS, MLP_ROWS), MLP_ROWS)
            h = jnp.sin(fr * (_dot_split(z_ref[rows, :], w1_ref[...]) + b1_ref[...]))
            h = jnp.sin(fr * (_dot_split(h, w2_ref[...]) + b2_ref[...]))
            h_ref[rows, :] = jnp.sin(fr * (_dot_split(h, w3_ref[...]) + b3_ref[...]))
            return carry

        lax.fori_loop(0, seq // MLP_ROWS, mlp, 0)
        f_idx = lax.broadcasted_iota(jnp.int32, cphi_ref.shape, 0)
        phi = (f_idx.astype(F32) + 0.5) * (math.pi / (2 * lb))
        cphi_ref[...] = jnp.cos(phi) * scale
        sphi_ref[...] = jnp.sin(phi) * scale
        sgn_ref[...] = (1 - 2 * (f_idx & 1)).astype(F32)

    h = h_ref[...]
    decay = jnp.exp(-z_ref[:, 0:1] * dl_ref[...])
    taps = (_dot_split(h, w4f_ref[...]) * decay, _dot_split(h, w4b_ref[...]) * decay)
    taps = tuple(k.astype(BF16) for k in taps)
    for side, k in enumerate(taps):
        for s_ in range(nb):
            c0 = (side * nb + s_) * cb
            kk_ref[:, c0:c0 + cb] = k[s_ * lb:(s_ + 1) * lb]

    group = 2 * cb
    tile = lambda ref, width: jnp.concatenate([ref[...]] * (width // LANES), axis=1)
    c_phi, s_phi = tile(cphi_ref, group), tile(sphi_ref, group)
    for g in range(2 * nb * cb // group):
        cols = slice(g * group, (g + 1) * group)
        kk = kk_ref[:, cols]
        a = _dot(cs_ref[:, 0:lb], kk)
        b = _dot(cs_ref[:, lb:2 * lb], kk)
        xr_ref[:, cols] = c_phi * a + s_phi * b
        xi_ref[:, cols] = s_phi * a - c_phi * b

    sgn = tile(sgn_ref, cb)

    def seg(ref, side, s_):
        c0 = (side * nb + s_) * cb
        return ref[:, c0:c0 + cb]

    def causal_block(side, delta):
        r0 = (delta - 1) * lb
        first = taps[side][r0:r0 + 1].astype(F32) * scale
        re = seg(xr_ref, side, delta) - sgn * seg(xi_ref, side, delta - 1)
        im = seg(xi_ref, side, delta) + sgn * (seg(xr_ref, side, delta - 1) - first)
        return re, im

    mid = nb - 1
    out = kre_ref.dtype
    kre_ref[mid] = (seg(xr_ref, 0, 0) + seg(xr_ref, 1, 0)).astype(out)
    kim_ref[mid] = (seg(xi_ref, 0, 0) - seg(xi_ref, 1, 0)).astype(out)
    for delta in range(1, nb):
        re, im = causal_block(0, delta)
        kre_ref[mid + delta] = re.astype(out)
        kim_ref[mid + delta] = im.astype(out)
        re, im = causal_block(1, delta)
        kre_ref[mid - delta] = re.astype(out)
        kim_ref[mid - delta] = (-im).astype(out)


def _filter_spectra(z_pad, w1_pad, b1, w2, b2, w3, b3, freq, w4, deltas, cs):
    seq = z_pad.shape[0]
    lb = cs.shape[0]
    nb = seq // lb
    hf = w2.shape[0]
    dh = deltas.shape[1]
    cb = MXU_DIM
    nblk = dh // cb
    full = lambda a: pl.BlockSpec(a.shape, lambda c: (0,) * a.ndim)
    return pl.pallas_call(
        _filter_kernel,
        grid=(nblk,),
        in_specs=[
            full(z_pad), full(w1_pad), full(b1), full(w2), full(b2), full(w3), full(b3), full(freq),
            pl.BlockSpec((hf, cb), lambda c: (0, c)),
            pl.BlockSpec((hf, cb), lambda c: (0, nblk + c)),
            pl.BlockSpec((1, cb), lambda c: (0, c)),
            full(cs),
        ],
        out_specs=[pl.BlockSpec((2 * nb - 1, lb, cb), lambda c: (0, 0, c))] * 2,
        out_shape=[jax.ShapeDtypeStruct((2 * nb - 1, lb, dh), SPEC_DTYPE)] * 2,
        scratch_shapes=[
            pltpu.VMEM((seq, hf), F32),
            pltpu.VMEM((lb, LANES), F32),
            pltpu.VMEM((lb, LANES), F32),
            pltpu.VMEM((lb, LANES), F32),
            pltpu.VMEM((lb, 2 * nb * cb), BF16),
            pltpu.VMEM((lb, 2 * nb * cb), F32),
            pltpu.VMEM((lb, 2 * nb * cb), F32),
        ],
        compiler_params=pltpu.CompilerParams(
            dimension_semantics=("arbitrary",), vmem_limit_bytes=FILTER_VMEM),
        name="filter_spectra",
    )(z_pad, w1_pad, b1, w2, b2, w3, b3, freq, w4, w4, deltas, cs)


def _conv3(p, w, bias, lo, n):
    n_rows = p.shape[0]
    assert 1 <= lo and lo + n < n_rows
    dn = pltpu.roll(p, 1, 0)
    up = pltpu.roll(p, n_rows - 1, 0)
    out = dn * w[0:1] + p * w[1:2] + up * w[2:3] + bias
    return out[lo:lo + n]


def _window_mean_minus_self(p, win, lo, n, at_start, at_end):
    n_rows = p.shape[0]
    half = win // 2
    sl = SUBLANES
    assert lo >= win and lo + n + win <= n_rows and half <= sl
    s = p
    w = 1
    while w < win:
        s = s + pltpu.roll(s, w, 0)
        w *= 2
    if half > 1:
        s = pltpu.roll(s, n_rows - (half - 1), 0)
    total = s[lo:lo + n]
    edge = lax.broadcasted_iota(jnp.int32, (sl, p.shape[1]), 0)
    cnt_first = jnp.where(at_start, jnp.minimum(edge + half, win), win).astype(F32)
    cnt_last = jnp.where(at_end, jnp.minimum(sl - edge + half, win), win).astype(F32)
    mean = jnp.concatenate([total[0:sl] / cnt_first,
                            total[sl:n - sl] * (1.0 / win),
                            total[n - sl:] / cnt_last], axis=0)
    return mean - p[lo:lo + n]


def _inproj_kernel(xp_ref, x_ref, xn_ref, g_ref, w_ref, b_ref, cw_ref, cb_ref,
                   pw_ref, pb_ref, ps_ref, u_ref, gate_ref, yp_ref, p_ref, *, chunks_per_seq):
    i = pl.program_id(0)
    rows = x_ref.shape[0]
    halo = xp_ref.shape[0]
    dh = u_ref.shape[1]
    cb = MXU_DIM
    at_start = i % chunks_per_seq == 0
    at_end = i % chunks_per_seq == chunks_per_seq - 1

    xw = jnp.concatenate([xp_ref[...], x_ref[...], xn_ref[...]], axis=0)
    h = (_rms_scale(xw) * g_ref[...]).astype(BF16)

    h_main = h[halo:halo + rows]

    def project(rows_h, c0):
        return _dot(rows_h, w_ref[:, c0:c0 + cb]) + b_ref[:, c0:c0 + cb]

    def conv_input(c0):
        p = project(h, c0)
        return jnp.concatenate(
            [jnp.where(at_start, 0.0, p[0:halo]), p[halo:halo + rows],
             jnp.where(at_end, 0.0, p[halo + rows:])], axis=0)

    def conv(c0):
        return _conv3(conv_input(c0), cw_ref[:, c0:c0 + cb], cb_ref[:, c0:c0 + cb], halo, rows)

    dp = yp_ref.shape[1]
    assert dp == len(POOL_WINDOWS) * cb
    def hyena_group(c):
        lanes = slice(c * cb, (c + 1) * cb)
        u_ref[:, lanes] = conv(dh + c * cb) * conv(2 * dh + c * cb)
        z = project(h_main, 3 * dh + c * cb)
        gate_ref[:, lanes] = (conv(c * cb) * jax.nn.silu(z)).astype(BF16)

    def pool_group(g, win):
        lanes = slice(g * cb, (g + 1) * cb)
        pooled = _window_mean_minus_self(conv_input(4 * dh + g * cb), win, halo, rows,
                                         at_start, at_end)
        y = _dot(pooled.astype(BF16), pw_ref[g]) + pb_ref[:, lanes]
        z = project(h_main, 4 * dh + dp + g * cb)
        yp_ref[:, lanes] = (y * ps_ref[:, lanes] * jax.nn.silu(z)).astype(BF16)

    for c in range(dh // cb):
        hyena_group(c)
    for g, win in enumerate(POOL_WINDOWS):
        pool_group(g, win)

    for c0 in range(0, p_ref.shape[1], cb):
        p_ref[:, c0:c0 + cb] = project(h_main, 4 * dh + 2 * dp + c0).astype(BF16)


def _inproj(x2, g_norm, w_in_bf, b_in, conv_w, conv_b, pool_w_bf, pool_b, pool_scale, seq):
    n_tok, d = x2.shape
    cols = w_in_bf.shape[1]
    dh = conv_b.shape[1] // 3
    dp = pool_b.shape[1]
    n_gate = cols - 4 * dh - 2 * dp
    rows = ROW_CHUNK
    halo = 2 * SUBLANES
    per_halo = rows // halo
    last_halo = n_tok // halo - 1
    full = lambda a: pl.BlockSpec(a.shape, lambda i: (0,) * a.ndim)
    tile = lambda width: pl.BlockSpec((rows, width), lambda i: (i, 0))
    return pl.pallas_call(
        functools.partial(_inproj_kernel, chunks_per_seq=seq // rows),
        grid=(n_tok // rows,),
        in_specs=[
            pl.BlockSpec((halo, d), lambda i: (jnp.maximum(i * per_halo - 1, 0), 0)),
            tile(d),
            pl.BlockSpec((halo, d), lambda i: (jnp.minimum((i + 1) * per_halo, last_halo), 0)),
            full(g_norm),
            pl.BlockSpec(w_in_bf.shape, lambda i: (0, 0), pipeline_mode=pl.Buffered(1)),
            full(b_in), full(conv_w), full(conv_b),
            full(pool_w_bf), full(pool_b), full(pool_scale),
        ],
        out_specs=[tile(dh), tile(dh), tile(dp), tile(n_gate)],
        out_shape=[
            jax.ShapeDtypeStruct((n_tok, dh), F32),
            jax.ShapeDtypeStruct((n_tok, dh), BF16),
            jax.ShapeDtypeStruct((n_tok, dp), BF16),
            jax.ShapeDtypeStruct((n_tok, n_gate), BF16),
        ],
        compiler_params=pltpu.CompilerParams(
            dimension_semantics=("parallel",), vmem_limit_bytes=INPROJ_VMEM),
        name="inproj",
    )(x2, x2, x2, g_norm, w_in_bf, b_in, conv_w, conv_b, pool_w_bf, pool_b, pool_scale)


def _hyena_kernel(u_ref, gate_ref, d_ref, kre_ref, kim_ref, cs_ref, y_ref, av_ref, acc_ref, pq_ref):
    seq = u_ref.shape[0]
    lb = cs_ref.shape[0]
    nb = seq // lb
    mid = nb - 1
    blocks = [slice(b * lb, (b + 1) * lb) for b in range(nb)]
    chunks = [slice(r0, r0 + SPEC_ROWS) for r0 in range(0, lb, SPEC_ROWS)]

    def forward(b):
        ub = u_ref[blocks[b], :].astype(BF16)
        av_ref[2 * b] = _dot(cs_ref[:, 0:lb], ub).astype(av_ref.dtype)
        av_ref[2 * b + 1] = _dot(cs_ref[:, lb:2 * lb], ub).astype(av_ref.dtype)

    def spectrum(b, rows):
        return av_ref[2 * b, rows, :], av_ref[2 * b + 1, rows, :]

    def products(a, b, rows, ur, vi):
        kr = kre_ref[mid + a - b, rows, :]
        ki = kim_ref[mid + a - b, rows, :]
        return kr * ur + ki * vi, kr * vi - ki * ur

    def accumulate(b):
        for rows in chunks:
            ur, vi = spectrum(b, rows)
            for a in range(nb):
                tp, tq = products(a, b, rows, ur, vi)
                if b == 0:
                    acc_ref[2 * a, rows, :] = tp
                    acc_ref[2 * a + 1, rows, :] = tq
                else:
                    acc_ref[2 * a, rows, :] += tp
                    acc_ref[2 * a + 1, rows, :] += tq

    def finish(a):
        for rows in chunks:
            tp, tq = products(a, nb - 1, rows, *spectrum(nb - 1, rows))
            pq_ref[a, rows, :] = (acc_ref[2 * a, rows, :] + tp).astype(BF16)
            pq_ref[a, lb + rows.start:lb + rows.stop, :] = (
                acc_ref[2 * a + 1, rows, :] + tq).astype(BF16)

    def inverse(a):
        rows = blocks[a]
        y = _dot(cs_ref[...], pq_ref[a])
        y = y + u_ref[rows, :] * d_ref[...]
        y_ref[rows, :] = (y * gate_ref[rows, :].astype(F32)).astype(BF16)

    forward(0)
    for b in range(nb - 1):
        forward(b + 1)
        accumulate(b)
    for a in range(nb):
        finish(a)
        if a > 0:
            inverse(a - 1)
    inverse(nb - 1)


def _hyena(u, gate, hyena_d, kre, kim, cs, batch, seq):
    dh = hyena_d.shape[1]
    cw = HYENA_COLS
    lb = cs.shape[0]
    nb = seq // lb
    act = pl.BlockSpec((seq, cw), lambda c, b: (b, c))
    spectra = pl.BlockSpec((2 * nb - 1, lb, cw), lambda c, b: (0, 0, c),
                           pipeline_mode=pl.Buffered(1))
    return pl.pallas_call(
        _hyena_kernel,
        grid=(dh // cw, batch),
        in_specs=[
            act, act,
            pl.BlockSpec((1, cw), lambda c, b: (0, c)),
            spectra, spectra,
            pl.BlockSpec(cs.shape, lambda c, b: (0, 0)),
        ],
        out_specs=act,
        out_shape=jax.ShapeDtypeStruct((batch * seq, dh), BF16),
        scratch_shapes=[
            pltpu.VMEM((2 * nb, lb, cw), SPEC_DTYPE),
            pltpu.VMEM((2 * nb, lb, cw), SPEC_DTYPE),
            pltpu.VMEM((nb, 2 * lb, cw), BF16),
        ],
        compiler_params=pltpu.CompilerParams(
            dimension_semantics=("arbitrary", "arbitrary"), vmem_limit_bytes=HYENA_VMEM),
        name="hyena",
    )(u, gate, hyena_d, kre, kim, cs)


def _merge_kernel(yh_ref, yp_ref, g0_ref, g1_ref, x_ref, wh_ref, wp_ref, wo_ref, gf_ref, o_ref):
    out_h = _dot(yh_ref[...], wh_ref[...])
    out_p = _dot(yp_ref[...], wp_ref[...])
    merged = (jax.nn.sigmoid(g0_ref[...].astype(F32)) * out_h
              + jax.nn.sigmoid(g1_ref[...].astype(F32)) * out_p)
    res = x_ref[...] + _dot(merged.astype(BF16), wo_ref[...])
    o_ref[...] = _rms_scale(res) * gf_ref[...]


def _merge(yh, yp, gates, x2, wh_bf, wp_bf, wo_bf, g_final, tm=1024):
    n_tok, d = x2.shape
    tile = lambda col: pl.BlockSpec((tm, d), lambda i: (i, col))
    resident = lambda a: pl.BlockSpec(a.shape, lambda i: (0, 0), pipeline_mode=pl.Buffered(1))
    return pl.pallas_call(
        _merge_kernel,
        grid=(n_tok // tm,),
        in_specs=[
            tile(0), tile(0), tile(0), tile(1), tile(0),
            resident(wh_bf), resident(wp_bf), resident(wo_bf),
            pl.BlockSpec((1, d), lambda i: (0, 0)),
        ],
        out_specs=tile(0),
        out_shape=jax.ShapeDtypeStruct((n_tok, d), F32),
        compiler_params=pltpu.CompilerParams(
            dimension_semantics=("parallel",), vmem_limit_bytes=MERGE_VMEM),
        name="merge",
    )(yh, yp, gates, gates, x2, wh_bf, wp_bf, wo_bf, g_final)


def _filter_features(seq):
    t = jnp.linspace(0.0, 1.0, seq, dtype=F32)[:, None]
    bands = (FILTER_EMB - 1) // 2
    w = 2.0 * math.pi * jnp.arange(seq, dtype=F32) / seq
    f = jnp.linspace(1e-4, bands - 1, bands, dtype=F32)
    ang = w[:, None] * f[None, :]
    return jnp.concatenate([t, jnp.cos(ang), -jnp.sin(ang)], axis=-1)


def kernel(x, g_norm, w_in, b_in, conv_w, conv_b, filt_w1, filt_b1, filt_w2, filt_b2, filt_w3, filt_b3, filt_w4, filt_freq, hyena_d, w_hyena_out, pool_w, pool_b, pool_scale, w_pool_out, w_out, g_final):
    batch, seq, d = x.shape
    dh = hyena_d.shape[1]
    dp = pool_b.shape[1]
    assert g_norm.shape[0] == 1, "single-layer block"
    assert d == dh == dp and conv_b.shape[1] == 3 * dh and w_in.shape[2] == 4 * dh + 2 * dp + 2 * d
    assert dh % HYENA_COLS == 0 and seq % ROW_CHUNK == 0
    assert seq % CONV_BLOCKS == 0 and (seq // CONV_BLOCKS) % SPEC_ROWS == 0

    cs = _transform_matrices(seq // CONV_BLOCKS)
    z_pad = jnp.pad(_filter_features(seq), ((0, 0), (0, LANES - FILTER_EMB)))
    w1_pad = jnp.pad(filt_w1[0], ((0, LANES - FILTER_EMB), (0, 0)))
    deltas = jnp.abs(jnp.linspace(MIN_DECAY, MAX_DECAY, dh, dtype=F32))[None, :]
    x2 = x.reshape(batch * seq, d)
    u, gate, yp, gates = _inproj(x2, g_norm, w_in[0].astype(BF16), b_in, conv_w[0], conv_b,
                                 pool_w[0].astype(BF16), pool_b, pool_scale, seq)
    kre, kim = _filter_spectra(z_pad, w1_pad, filt_b1, filt_w2[0], filt_b2, filt_w3[0], filt_b3,
                               filt_freq, filt_w4[0], deltas, cs)
    yh = _hyena(u, gate, hyena_d, kre, kim, cs, batch, seq)
    out = _merge(yh, yp, gates, x2, w_hyena_out[0].astype(BF16), w_pool_out[0].astype(BF16),
                 w_out[0].astype(BF16), g_final[None, :])
    return out.reshape(batch, seq, d)
```

```python
import functools
import math

import jax
import jax.numpy as jnp
from jax import lax
from jax.experimental import pallas as pl
from jax.experimental.pallas import tpu as pltpu

FILTER_EMB = 33
FAST_DECAY_PCT = 0.3
SLOW_DECAY_PCT = 1.5
DECAY_TARGET = 1e-2
MAX_DECAY = math.log(DECAY_TARGET) / FAST_DECAY_PCT
MIN_DECAY = math.log(DECAY_TARGET) / SLOW_DECAY_PCT
POOL_WINDOWS = (2, 4, 8, 16)
NORM_EPS = 1e-6

LANES = 128
SUBLANES = 8
MXU_DIM = 256
HYENA_COLS = MXU_DIM
ROW_CHUNK = 1024
MLP_ROWS = 256
CONV_BLOCKS = 4
SPEC_ROWS = 64
MIB = 1024 * 1024
FILTER_VMEM = 40 * MIB
INPROJ_VMEM = 55 * MIB
HYENA_VMEM = 24 * MIB
MERGE_VMEM = 26 * MIB

F32 = jnp.float32
BF16 = jnp.bfloat16
SPEC_DTYPE = BF16


def _dot(a, b):
    return jnp.dot(a, b, preferred_element_type=F32)


def _dot_split(a, b):
    a_hi, b_hi = a.astype(BF16), b.astype(BF16)
    a_lo = (a - a_hi.astype(F32)).astype(BF16)
    b_lo = (b - b_hi.astype(F32)).astype(BF16)
    return _dot(a_hi, b_hi) + _dot(a_lo, b_hi) + _dot(a_hi, b_lo)


def _rms_scale(x):
    return x * lax.rsqrt(jnp.mean(x * x, axis=-1, keepdims=True) + NORM_EPS)


def _transform_matrices(lb):
    period = 8 * lb
    odd = 2 * jnp.arange(lb, dtype=jnp.int32) + 1
    ang = ((odd[:, None] * odd[None, :]) % period).astype(F32) * (2.0 * math.pi / period)
    return jnp.concatenate([jnp.cos(ang), jnp.sin(ang)], axis=1).astype(BF16)


def _filter_kernel(z_ref, w1_ref, b1_ref, w2_ref, b2_ref, w3_ref, b3_ref, fr_ref,
                   w4f_ref, w4b_ref, dl_ref, cs_ref, kre_ref, kim_ref,
                   h_ref, cphi_ref, sphi_ref, sgn_ref, kk_ref, xr_ref, xi_ref):
    seq = z_ref.shape[0]
    lb = cs_ref.shape[0]
    nb = seq // lb
    cb = dl_ref.shape[1]
    scale = 1.0 / lb

    @pl.when(pl.program_id(0) == 0)
    def _():
        fr = fr_ref[...]

        def mlp(i, carry):
            rows = pl.ds(pl.multiple_of(i * MLP_ROWS, MLP_ROWS), MLP_ROWS)
            h = jnp.sin(fr * (_dot_split(z_ref[rows, :], w1_ref[...]) + b1_ref[...]))
            h = jnp.sin(fr * (_dot_split(h, w2_ref[...]) + b2_ref[...]))
            h_ref[rows, :] = jnp.sin(fr * (_dot_split(h, w3_ref[...]) + b3_ref[...]))
            return carry

        lax.fori_loop(0, seq // MLP_ROWS, mlp, 0)
        f_idx = lax.broadcasted_iota(jnp.int32, cphi_ref.shape, 0)
        phi = (f_idx.astype(F32) + 0.5) * (math.pi / (2 * lb))
        cphi_ref[...] = jnp.cos(phi) * scale
        sphi_ref[...] = jnp.sin(phi) * scale
        sgn_ref[...] = (1 - 2 * (f_idx & 1)).astype(F32)

    h = h_ref[...]
    decay = jnp.exp(-z_ref[:, 0:1] * dl_ref[...])
    taps = (_dot_split(h, w4f_ref[...]) * decay, _dot_split(h, w4b_ref[...]) * decay)
    taps = tuple(k.astype(BF16) for k in taps)
    for side, k in enumerate(taps):
        for s_ in range(nb):
            c0 = (side * nb + s_) * cb
            kk_ref[:, c0:c0 + cb] = k[s_ * lb:(s_ + 1) * lb]

    group = 2 * cb
    tile = lambda ref, width: jnp.concatenate([ref[...]] * (width // LANES), axis=1)
    c_phi, s_phi = tile(cphi_ref, group), tile(sphi_ref, group)
    for g in range(2 * nb * cb // group):
        cols = slice(g * group, (g + 1) * group)
        kk = kk_ref[:, cols]
        a = _dot(cs_ref[:, 0:lb], kk)
        b = _dot(cs_ref[:, lb:2 * lb], kk)
        xr_ref[:, cols] = c_phi * a + s_phi * b
        xi_ref[:, cols] = s_phi * a - c_phi * b

    sgn = tile(sgn_ref, cb)

    def seg(ref, side, s_):
        c0 = (side * nb + s_) * cb
        return ref[:, c0:c0 + cb]

    def causal_block(side, delta):
        r0 = (delta - 1) * lb
        first = taps[side][r0:r0 + 1].astype(F32) * scale
        re = seg(xr_ref, side, delta) - sgn * seg(xi_ref, side, delta - 1)
        im = seg(xi_ref, side, delta) + sgn * (seg(xr_ref, side, delta - 1) - first)
        return re, im

    mid = nb - 1
    out = kre_ref.dtype
    kre_ref[mid] = (seg(xr_ref, 0, 0) + seg(xr_ref, 1, 0)).astype(out)
    kim_ref[mid] = (seg(xi_ref, 0, 0) - seg(xi_ref, 1, 0)).astype(out)
    for delta in range(1, nb):
        re, im = causal_block(0, delta)
        kre_ref[mid + delta] = re.astype(out)
        kim_ref[mid + delta] = im.astype(out)
        re, im = causal_block(1, delta)
        kre_ref[mid - delta] = re.astype(out)
        kim_ref[mid - delta] = (-im).astype(out)


def _filter_spectra(z_pad, w1_pad, b1, w2, b2, w3, b3, freq, w4, deltas, cs):
    seq = z_pad.shape[0]
    lb = cs.shape[0]
    nb = seq // lb
    hf = w2.shape[0]
    dh = deltas.shape[1]
    cb = MXU_DIM
    nblk = dh // cb
    full = lambda a: pl.BlockSpec(a.shape, lambda c: (0,) * a.ndim)
    return pl.pallas_call(
        _filter_kernel,
        grid=(nblk,),
        in_specs=[
            full(z_pad), full(w1_pad), full(b1), full(w2), full(b2), full(w3), full(b3), full(freq),
            pl.BlockSpec((hf, cb), lambda c: (0, c)),
            pl.BlockSpec((hf, cb), lambda c: (0, nblk + c)),
            pl.BlockSpec((1, cb), lambda c: (0, c)),
            full(cs),
        ],
        out_specs=[pl.BlockSpec((2 * nb - 1, lb, cb), lambda c: (0, 0, c))] * 2,
        out_shape=[jax.ShapeDtypeStruct((2 * nb - 1, lb, dh), SPEC_DTYPE)] * 2,
        scratch_shapes=[
            pltpu.VMEM((seq, hf), F32),
            pltpu.VMEM((lb, LANES), F32),
            pltpu.VMEM((lb, LANES), F32),
            pltpu.VMEM((lb, LANES), F32),
            pltpu.VMEM((lb, 2 * nb * cb), BF16),
            pltpu.VMEM((lb, 2 * nb * cb), F32),
            pltpu.VMEM((lb, 2 * nb * cb), F32),
        ],
        compiler_params=pltpu.CompilerParams(
            dimension_semantics=("arbitrary",), vmem_limit_bytes=FILTER_VMEM),
        name="filter_spectra",
    )(z_pad, w1_pad, b1, w2, b2, w3, b3, freq, w4, w4, deltas, cs)


def _conv3(p, w, bias, lo, n):
    n_rows = p.shape[0]
    assert 1 <= lo and lo + n < n_rows
    dn = pltpu.roll(p, 1, 0)
    up = pltpu.roll(p, n_rows - 1, 0)
    out = dn * w[0:1] + p * w[1:2] + up * w[2:3] + bias
    return out[lo:lo + n]


def _window_mean_minus_self(p, win, lo, n, at_start, at_end):
    n_rows = p.shape[0]
    half = win // 2
    sl = SUBLANES
    assert lo >= win and lo + n + win <= n_rows and half <= sl
    s = p
    w = 1
    while w < win:
        s = s + pltpu.roll(s, w, 0)
        w *= 2
    if half > 1:
        s = pltpu.roll(s, n_rows - (half - 1), 0)
    total = s[lo:lo + n]
    edge = lax.broadcasted_iota(jnp.int32, (sl, p.shape[1]), 0)
    cnt_first = jnp.where(at_start, jnp.minimum(edge + half, win), win).astype(F32)
    cnt_last = jnp.where(at_end, jnp.minimum(sl - edge + half, win), win).astype(F32)
    mean = jnp.concatenate([total[0:sl] / cnt_first,
                            total[sl:n - sl] * (1.0 / win),
                            total[n - sl:] / cnt_last], axis=0)
    return mean - p[lo:lo + n]


def _inproj_kernel(xp_ref, x_ref, xn_ref, g_ref, w_ref, b_ref, cw_ref, cb_ref,
                   pw_ref, pb_ref, ps_ref, u_ref, gate_ref, yp_ref, p_ref, *, chunks_per_seq):
    i = pl.program_id(0)
    rows = x_ref.shape[0]
    halo = xp_ref.shape[0]
    dh = u_ref.shape[1]
    cb = MXU_DIM
    at_start = i % chunks_per_seq == 0
    at_end = i % chunks_per_seq == chunks_per_seq - 1

    xw = jnp.concatenate([xp_ref[...], x_ref[...], xn_ref[...]], axis=0)
    h = (_rms_scale(xw) * g_ref[...]).astype(BF16)

    h_main = h[halo:halo + rows]

    def project(rows_h, c0):
        return _dot(rows_h, w_ref[:, c0:c0 + cb]) + b_ref[:, c0:c0 + cb]

    def conv_input(c0):
        p = project(h, c0)
        return jnp.concatenate(
            [jnp.where(at_start, 0.0, p[0:halo]), p[halo:halo + rows],
             jnp.where(at_end, 0.0, p[halo + rows:])], axis=0)

    def conv(c0):
        return _conv3(conv_input(c0), cw_ref[:, c0:c0 + cb], cb_ref[:, c0:c0 + cb], halo, rows)

    dp = yp_ref.shape[1]
    assert dp == len(POOL_WINDOWS) * cb
    def hyena_group(c):
        lanes = slice(c * cb, (c + 1) * cb)
        u_ref[:, lanes] = conv(dh + c * cb) * conv(2 * dh + c * cb)
        z = project(h_main, 3 * dh + c * cb)
        gate_ref[:, lanes] = (conv(c * cb) * jax.nn.silu(z)).astype(BF16)

    def pool_group(g, win):
        lanes = slice(g * cb, (g + 1) * cb)
        pooled = _window_mean_minus_self(conv_input(4 * dh + g * cb), win, halo, rows,
                                         at_start, at_end)
        y = _dot(pooled.astype(BF16), pw_ref[g]) + pb_ref[:, lanes]
        z = project(h_main, 4 * dh + dp + g * cb)
        yp_ref[:, lanes] = (y * ps_ref[:, lanes] * jax.nn.silu(z)).astype(BF16)

    for c in range(dh // cb):
        hyena_group(c)
    for g, win in enumerate(POOL_WINDOWS):
        pool_group(g, win)

    for c0 in range(0, p_ref.shape[1], cb):
        p_ref[:, c0:c0 + cb] = project(h_main, 4 * dh + 2 * dp + c0).astype(BF16)


def _inproj(x2, g_norm, w_in_bf, b_in, conv_w, conv_b, pool_w_bf, pool_b, pool_scale, seq):
    n_tok, d = x2.shape
    cols = w_in_bf.shape[1]
    dh = conv_b.shape[1] // 3
    dp = pool_b.shape[1]
    n_gate = cols - 4 * dh - 2 * dp
    rows = ROW_CHUNK
    halo = 2 * SUBLANES
    per_halo = rows // halo
    last_halo = n_tok // halo - 1
    full = lambda a: pl.BlockSpec(a.shape, lambda i: (0,) * a.ndim)
    tile = lambda width: pl.BlockSpec((rows, width), lambda i: (i, 0))
    return pl.pallas_call(
        functools.partial(_inproj_kernel, chunks_per_seq=seq // rows),
        grid=(n_tok // rows,),
        in_specs=[
            pl.BlockSpec((halo, d), lambda i: (jnp.maximum(i * per_halo - 1, 0), 0)),
            tile(d),
            pl.BlockSpec((halo, d), lambda i: (jnp.minimum((i + 1) * per_halo, last_halo), 0)),
            full(g_norm),
            pl.BlockSpec(w_in_bf.shape, lambda i: (0, 0), pipeline_mode=pl.Buffered(1)),
            full(b_in), full(conv_w), full(conv_b),
            full(pool_w_bf), full(pool_b), full(pool_scale),
        ],
        out_specs=[tile(dh), tile(dh), tile(dp), tile(n_gate)],
        out_shape=[
            jax.ShapeDtypeStruct((n_tok, dh), F32),
            jax.ShapeDtypeStruct((n_tok, dh), BF16),
            jax.ShapeDtypeStruct((n_tok, dp), BF16),
            jax.ShapeDtypeStruct((n_tok, n_gate), BF16),
        ],
        compiler_params=pltpu.CompilerParams(
            dimension_semantics=("parallel",), vmem_limit_bytes=INPROJ_VMEM),
        name="inproj",
    )(x2, x2, x2, g_norm, w_in_bf, b_in, conv_w, conv_b, pool_w_bf, pool_b, pool_scale)


def _hyena_kernel(u_ref, gate_ref, d_ref, kre_ref, kim_ref, cs_ref, y_ref, av_ref, acc_ref, pq_ref):
    seq = u_ref.shape[0]
    lb = cs_ref.shape[0]
    nb = seq // lb
    mid = nb - 1
    blocks = [slice(b * lb, (b + 1) * lb) for b in range(nb)]
    chunks = [slice(r0, r0 + SPEC_ROWS) for r0 in range(0, lb, SPEC_ROWS)]

    def forward(b):
        ub = u_ref[blocks[b], :].astype(BF16)
        av_ref[2 * b] = _dot(cs_ref[:, 0:lb], ub).astype(av_ref.dtype)
        av_ref[2 * b + 1] = _dot(cs_ref[:, lb:2 * lb], ub).astype(av_ref.dtype)

    def spectrum(b, rows):
        return av_ref[2 * b, rows, :], av_ref[2 * b + 1, rows, :]

    def products(a, b, rows, ur, vi):
        kr = kre_ref[mid + a - b, rows, :]
        ki = kim_ref[mid + a - b, rows, :]
        return kr * ur + ki * vi, kr * vi - ki * ur

    def accumulate(b):
        for rows in chunks:
            ur, vi = spectrum(b, rows)
            for a in range(nb):
                tp, tq = products(a, b, rows, ur, vi)
                if b == 0:
                    acc_ref[2 * a, rows, :] = tp
                    acc_ref[2 * a + 1, rows, :] = tq
                else:
                    acc_ref[2 * a, rows, :] += tp
                    acc_ref[2 * a + 1, rows, :] += tq

    def finish(a):
        for rows in chunks:
            tp, tq = products(a, nb - 1, rows, *spectrum(nb - 1, rows))
            pq_ref[a, rows, :] = (acc_ref[2 * a, rows, :] + tp).astype(BF16)
            pq_ref[a, lb + rows.start:lb + rows.stop, :] = (
                acc_ref[2 * a + 1, rows, :] + tq).astype(BF16)

    def inverse(a):
        rows = blocks[a]
        y = _dot(cs_ref[...], pq_ref[a])
        y = y + u_ref[rows, :] * d_ref[...]
        y_ref[rows, :] = (y * gate_ref[rows, :].astype(F32)).astype(BF16)

    forward(0)
    for b in range(nb - 1):
        forward(b + 1)
        accumulate(b)
    for a in range(nb):
        finish(a)
        if a > 0:
            inverse(a - 1)
    inverse(nb - 1)


def _hyena(u, gate, hyena_d, kre, kim, cs, batch, seq):
    dh = hyena_d.shape[1]
    cw = HYENA_COLS
    lb = cs.shape[0]
    nb = seq // lb
    act = pl.BlockSpec((seq, cw), lambda c, b: (b, c))
    spectra = pl.BlockSpec((2 * nb - 1, lb, cw), lambda c, b: (0, 0, c),
                           pipeline_mode=pl.Buffered(1))
    return pl.pallas_call(
        _hyena_kernel,
        grid=(dh // cw, batch),
        in_specs=[
            act, act,
            pl.BlockSpec((1, cw), lambda c, b: (0, c)),
            spectra, spectra,
            pl.BlockSpec(cs.shape, lambda c, b: (0, 0)),
        ],
        out_specs=act,
        out_shape=jax.ShapeDtypeStruct((batch * seq, dh), BF16),
        scratch_shapes=[
            pltpu.VMEM((2 * nb, lb, cw), SPEC_DTYPE),
            pltpu.VMEM((2 * nb, lb, cw), SPEC_DTYPE),
            pltpu.VMEM((nb, 2 * lb, cw), BF16),
        ],
        compiler_params=pltpu.CompilerParams(
            dimension_semantics=("arbitrary", "arbitrary"), vmem_limit_bytes=HYENA_VMEM),
        name="hyena",
    )(u, gate, hyena_d, kre, kim, cs)


def _merge_kernel(yh_ref, yp_ref, g0_ref, g1_ref, x_ref, wh_ref, wp_ref, wo_ref, gf_ref, o_ref):
    out_h = _dot(yh_ref[...], wh_ref[...])
    out_p = _dot(yp_ref[...], wp_ref[...])
    merged = (jax.nn.sigmoid(g0_ref[...].astype(F32)) * out_h
              + jax.nn.sigmoid(g1_ref[...].astype(F32)) * out_p)
    res = x_ref[...] + _dot(merged.astype(BF16), wo_ref[...])
    o_ref[...] = _rms_scale(res) * gf_ref[...]


def _merge(yh, yp, gates, x2, wh_bf, wp_bf, wo_bf, g_final, tm=512):
    n_tok, d = x2.shape
    tile = lambda col: pl.BlockSpec((tm, d), lambda i: (i, col))
    resident = lambda a: pl.BlockSpec(a.shape, lambda i: (0, 0), pipeline_mode=pl.Buffered(1))
    return pl.pallas_call(
        _merge_kernel,
        grid=(n_tok // tm,),
        in_specs=[
            tile(0), tile(0), tile(0), tile(1), tile(0),
            resident(wh_bf), resident(wp_bf), resident(wo_bf),
            pl.BlockSpec((1, d), lambda i: (0, 0)),
        ],
        out_specs=tile(0),
        out_shape=jax.ShapeDtypeStruct((n_tok, d), F32),
        compiler_params=pltpu.CompilerParams(
            dimension_semantics=("parallel",), vmem_limit_bytes=MERGE_VMEM),
        name="merge",
    )(yh, yp, gates, gates, x2, wh_bf, wp_bf, wo_bf, g_final)


def _filter_features(seq):
    t = jnp.linspace(0.0, 1.0, seq, dtype=F32)[:, None]
    bands = (FILTER_EMB - 1) // 2
    w = 2.0 * math.pi * jnp.arange(seq, dtype=F32) / seq
    f = jnp.linspace(1e-4, bands - 1, bands, dtype=F32)
    ang = w[:, None] * f[None, :]
    return jnp.concatenate([t, jnp.cos(ang), -jnp.sin(ang)], axis=-1)


def kernel(x, g_norm, w_in, b_in, conv_w, conv_b, filt_w1, filt_b1, filt_w2, filt_b2, filt_w3, filt_b3, filt_w4, filt_freq, hyena_d, w_hyena_out, pool_w, pool_b, pool_scale, w_pool_out, w_out, g_final):
    batch, seq, d = x.shape
    dh = hyena_d.shape[1]
    dp = pool_b.shape[1]
    assert g_norm.shape[0] == 1, "single-layer block"
    assert d == dh == dp and conv_b.shape[1] == 3 * dh and w_in.shape[2] == 4 * dh + 2 * dp + 2 * d
    assert dh % HYENA_COLS == 0 and seq % ROW_CHUNK == 0
    assert seq % CONV_BLOCKS == 0 and (seq // CONV_BLOCKS) % SPEC_ROWS == 0

    cs = _transform_matrices(seq // CONV_BLOCKS)
    z_pad = jnp.pad(_filter_features(seq), ((0, 0), (0, LANES - FILTER_EMB)))
    w1_pad = jnp.pad(filt_w1[0], ((0, LANES - FILTER_EMB), (0, 0)))
    deltas = jnp.abs(jnp.linspace(MIN_DECAY, MAX_DECAY, dh, dtype=F32))[None, :]
    x2 = x.reshape(batch * seq, d)
    u, gate, yp, gates = _inproj(x2, g_norm, w_in[0].astype(BF16), b_in, conv_w[0], conv_b,
                                 pool_w[0].astype(BF16), pool_b, pool_scale, seq)
    kre, kim = _filter_spectra(z_pad, w1_pad, filt_b1, filt_w2[0], filt_b2, filt_w3[0], filt_b3,
                               filt_freq, filt_w4[0], deltas, cs)
    yh = _hyena(u, gate, hyena_d, kre, kim, cs, batch, seq)
    out = _merge(yh, yp, gates, x2, w_hyena_out[0].astype(BF16), w_pool_out[0].astype(BF16),
                 w_out[0].astype(BF16), g_final[None, :])
    return out.reshape(batch, seq, d)
```

```python
import functools
import math

import jax
import jax.numpy as jnp
from jax import lax
from jax.experimental import pallas as pl
from jax.experimental.pallas import tpu as pltpu

FILTER_EMB = 33
FAST_DECAY_PCT = 0.3
SLOW_DECAY_PCT = 1.5
DECAY_TARGET = 1e-2
MAX_DECAY = math.log(DECAY_TARGET) / FAST_DECAY_PCT
MIN_DECAY = math.log(DECAY_TARGET) / SLOW_DECAY_PCT
POOL_WINDOWS = (2, 4, 8, 16)
NORM_EPS = 1e-6

LANES = 128
SUBLANES = 8
MXU_DIM = 256
HYENA_COLS = MXU_DIM
ROW_CHUNK = 1024
MLP_ROWS = 256
CONV_BLOCKS = 4
MIB = 1024 * 1024
FILTER_VMEM = 40 * MIB
INPROJ_VMEM = 55 * MIB
HYENA_VMEM = 24 * MIB
MERGE_VMEM = 47 * MIB

F32 = jnp.float32
BF16 = jnp.bfloat16
SPEC_DTYPE = BF16


def _dot(a, b):
    return jnp.dot(a, b, preferred_element_type=F32)


def _dot_split(a, b):
    a_hi, b_hi = a.astype(BF16), b.astype(BF16)
    a_lo = (a - a_hi.astype(F32)).astype(BF16)
    b_lo = (b - b_hi.astype(F32)).astype(BF16)
    return _dot(a_hi, b_hi) + _dot(a_lo, b_hi) + _dot(a_hi, b_lo)


def _rms_scale(x):
    return x * lax.rsqrt(jnp.mean(x * x, axis=-1, keepdims=True) + NORM_EPS)


def _transform_matrices(lb):
    period = 8 * lb
    odd = 2 * jnp.arange(lb, dtype=jnp.int32) + 1
    ang = ((odd[:, None] * odd[None, :]) % period).astype(F32) * (2.0 * math.pi / period)
    return jnp.concatenate([jnp.cos(ang), jnp.sin(ang)], axis=1).astype(BF16)


def _filter_kernel(z_ref, w1_ref, b1_ref, w2_ref, b2_ref, w3_ref, b3_ref, fr_ref,
                   w4f_ref, w4b_ref, dl_ref, cs_ref, kre_ref, kim_ref,
                   h_ref, cphi_ref, sphi_ref, sgn_ref, kk_ref, xr_ref, xi_ref):
    seq = z_ref.shape[0]
    lb = cs_ref.shape[0]
    nb = seq // lb
    cb = dl_ref.shape[1]
    scale = 1.0 / lb

    @pl.when(pl.program_id(0) == 0)
    def _():
        fr = fr_ref[...]

        def mlp(i, carry):
            rows = pl.ds(pl.multiple_of(i * MLP_ROWS, MLP_ROWS), MLP_ROWS)
            h = jnp.sin(fr * (_dot_split(z_ref[rows, :], w1_ref[...]) + b1_ref[...]))
            h = jnp.sin(fr * (_dot_split(h, w2_ref[...]) + b2_ref[...]))
            h_ref[rows, :] = jnp.sin(fr * (_dot_split(h, w3_ref[...]) + b3_ref[...]))
            return carry

        lax.fori_loop(0, seq // MLP_ROWS, mlp, 0)
        f_idx = lax.broadcasted_iota(jnp.int32, cphi_ref.shape, 0)
        phi = (f_idx.astype(F32) + 0.5) * (math.pi / (2 * lb))
        cphi_ref[...] = jnp.cos(phi) * scale
        sphi_ref[...] = jnp.sin(phi) * scale
        sgn_ref[...] = (1 - 2 * (f_idx & 1)).astype(F32)

    h = h_ref[...]
    decay = jnp.exp(-z_ref[:, 0:1] * dl_ref[...])
    taps = (_dot_split(h, w4f_ref[...]) * decay, _dot_split(h, w4b_ref[...]) * decay)
    taps = tuple(k.astype(BF16) for k in taps)
    for side, k in enumerate(taps):
        for s_ in range(nb):
            c0 = (side * nb + s_) * cb
            kk_ref[:, c0:c0 + cb] = k[s_ * lb:(s_ + 1) * lb]

    group = 2 * cb
    tile = lambda ref, width: jnp.concatenate([ref[...]] * (width // LANES), axis=1)
    c_phi, s_phi = tile(cphi_ref, group), tile(sphi_ref, group)
    for g in range(2 * nb * cb // group):
        cols = slice(g * group, (g + 1) * group)
        kk = kk_ref[:, cols]
        a = _dot(cs_ref[:, 0:lb], kk)
        b = _dot(cs_ref[:, lb:2 * lb], kk)
        xr_ref[:, cols] = c_phi * a + s_phi * b
        xi_ref[:, cols] = s_phi * a - c_phi * b

    sgn = tile(sgn_ref, cb)

    def seg(ref, side, s_):
        c0 = (side * nb + s_) * cb
        return ref[:, c0:c0 + cb]

    def causal_block(side, delta):
        r0 = (delta - 1) * lb
        first = taps[side][r0:r0 + 1].astype(F32) * scale
        re = seg(xr_ref, side, delta) - sgn * seg(xi_ref, side, delta - 1)
        im = seg(xi_ref, side, delta) + sgn * (seg(xr_ref, side, delta - 1) - first)
        return re, im

    mid = nb - 1
    out = kre_ref.dtype
    kre_ref[mid] = (seg(xr_ref, 0, 0) + seg(xr_ref, 1, 0)).astype(out)
    kim_ref[mid] = (seg(xi_ref, 0, 0) - seg(xi_ref, 1, 0)).astype(out)
    for delta in range(1, nb):
        re, im = causal_block(0, delta)
        kre_ref[mid + delta] = re.astype(out)
        kim_ref[mid + delta] = im.astype(out)
        re, im = causal_block(1, delta)
        kre_ref[mid - delta] = re.astype(out)
        kim_ref[mid - delta] = (-im).astype(out)


def _filter_spectra(z_pad, w1_pad, b1, w2, b2, w3, b3, freq, w4, deltas, cs):
    seq = z_pad.shape[0]
    lb = cs.shape[0]
    nb = seq // lb
    hf = w2.shape[0]
    dh = deltas.shape[1]
    cb = MXU_DIM
    nblk = dh // cb
    full = lambda a: pl.BlockSpec(a.shape, lambda c: (0,) * a.ndim)
    return pl.pallas_call(
        _filter_kernel,
        grid=(nblk,),
        in_specs=[
            full(z_pad), full(w1_pad), full(b1), full(w2), full(b2), full(w3), full(b3), full(freq),
            pl.BlockSpec((hf, cb), lambda c: (0, c)),
            pl.BlockSpec((hf, cb), lambda c: (0, nblk + c)),
            pl.BlockSpec((1, cb), lambda c: (0, c)),
            full(cs),
        ],
        out_specs=[pl.BlockSpec((2 * nb - 1, lb, cb), lambda c: (0, 0, c))] * 2,
        out_shape=[jax.ShapeDtypeStruct((2 * nb - 1, lb, dh), SPEC_DTYPE)] * 2,
        scratch_shapes=[
            pltpu.VMEM((seq, hf), F32),
            pltpu.VMEM((lb, LANES), F32),
            pltpu.VMEM((lb, LANES), F32),
            pltpu.VMEM((lb, LANES), F32),
            pltpu.VMEM((lb, 2 * nb * cb), BF16),
            pltpu.VMEM((lb, 2 * nb * cb), F32),
            pltpu.VMEM((lb, 2 * nb * cb), F32),
        ],
        compiler_params=pltpu.CompilerParams(
            dimension_semantics=("arbitrary",), vmem_limit_bytes=FILTER_VMEM),
        name="filter_spectra",
    )(z_pad, w1_pad, b1, w2, b2, w3, b3, freq, w4, w4, deltas, cs)


def _conv3(p, w, bias, lo, n):
    n_rows = p.shape[0]
    assert 1 <= lo and lo + n < n_rows
    dn = pltpu.roll(p, 1, 0)
    up = pltpu.roll(p, n_rows - 1, 0)
    out = dn * w[0:1] + p * w[1:2] + up * w[2:3] + bias
    return out[lo:lo + n]


def _window_mean_minus_self(p, win, lo, n, at_start, at_end):
    n_rows = p.shape[0]
    half = win // 2
    sl = SUBLANES
    assert lo >= win and lo + n + win <= n_rows and half <= sl
    s = p
    w = 1
    while w < win:
        s = s + pltpu.roll(s, w, 0)
        w *= 2
    if half > 1:
        s = pltpu.roll(s, n_rows - (half - 1), 0)
    total = s[lo:lo + n]
    edge = lax.broadcasted_iota(jnp.int32, (sl, p.shape[1]), 0)
    cnt_first = jnp.where(at_start, jnp.minimum(edge + half, win), win).astype(F32)
    cnt_last = jnp.where(at_end, jnp.minimum(sl - edge + half, win), win).astype(F32)
    mean = jnp.concatenate([total[0:sl] / cnt_first,
                            total[sl:n - sl] * (1.0 / win),
                            total[n - sl:] / cnt_last], axis=0)
    return mean - p[lo:lo + n]


def _inproj_kernel(xp_ref, x_ref, xn_ref, g_ref, w_ref, b_ref, cw_ref, cb_ref,
                   pw_ref, pb_ref, ps_ref, u_ref, gate_ref, yp_ref, p_ref, *, chunks_per_seq):
    i = pl.program_id(0)
    rows = x_ref.shape[0]
    halo = xp_ref.shape[0]
    dh = u_ref.shape[1]
    cb = MXU_DIM
    at_start = i % chunks_per_seq == 0
    at_end = i % chunks_per_seq == chunks_per_seq - 1

    xw = jnp.concatenate([xp_ref[...], x_ref[...], xn_ref[...]], axis=0)
    h = (_rms_scale(xw) * g_ref[...]).astype(BF16)

    h_main = h[halo:halo + rows]

    def project(rows_h, c0):
        return _dot(rows_h, w_ref[:, c0:c0 + cb]) + b_ref[:, c0:c0 + cb]

    def conv_input(c0):
        p = project(h, c0)
        return jnp.concatenate(
            [jnp.where(at_start, 0.0, p[0:halo]), p[halo:halo + rows],
             jnp.where(at_end, 0.0, p[halo + rows:])], axis=0)

    def conv(c0):
        return _conv3(conv_input(c0), cw_ref[:, c0:c0 + cb], cb_ref[:, c0:c0 + cb], halo, rows)

    dp = yp_ref.shape[1]
    assert dp == len(POOL_WINDOWS) * cb
    def hyena_group(c):
        lanes = slice(c * cb, (c + 1) * cb)
        u_ref[:, lanes] = conv(dh + c * cb) * conv(2 * dh + c * cb)
        z = project(h_main, 3 * dh + c * cb)
        gate_ref[:, lanes] = (conv(c * cb) * jax.nn.silu(z)).astype(BF16)

    def pool_group(g, win):
        lanes = slice(g * cb, (g + 1) * cb)
        pooled = _window_mean_minus_self(conv_input(4 * dh + g * cb), win, halo, rows,
                                         at_start, at_end)
        y = _dot(pooled.astype(BF16), pw_ref[g]) + pb_ref[:, lanes]
        z = project(h_main, 4 * dh + dp + g * cb)
        yp_ref[:, lanes] = (y * ps_ref[:, lanes] * jax.nn.silu(z)).astype(BF16)

    for c in range(dh // cb):
        hyena_group(c)
    for g, win in enumerate(POOL_WINDOWS):
        pool_group(g, win)

    for c0 in range(0, p_ref.shape[1], cb):
        p_ref[:, c0:c0 + cb] = project(h_main, 4 * dh + 2 * dp + c0).astype(BF16)


def _inproj(x2, g_norm, w_in_bf, b_in, conv_w, conv_b, pool_w_bf, pool_b, pool_scale, seq):
    n_tok, d = x2.shape
    cols = w_in_bf.shape[1]
    dh = conv_b.shape[1] // 3
    dp = pool_b.shape[1]
    n_gate = cols - 4 * dh - 2 * dp
    rows = ROW_CHUNK
    halo = 2 * SUBLANES
    per_halo = rows // halo
    last_halo = n_tok // halo - 1
    full = lambda a: pl.BlockSpec(a.shape, lambda i: (0,) * a.ndim)
    tile = lambda width: pl.BlockSpec((rows, width), lambda i: (i, 0))
    return pl.pallas_call(
        functools.partial(_inproj_kernel, chunks_per_seq=seq // rows),
        grid=(n_tok // rows,),
        in_specs=[
            pl.BlockSpec((halo, d), lambda i: (jnp.maximum(i * per_halo - 1, 0), 0)),
            tile(d),
            pl.BlockSpec((halo, d), lambda i: (jnp.minimum((i + 1) * per_halo, last_halo), 0)),
            full(g_norm),
            pl.BlockSpec(w_in_bf.shape, lambda i: (0, 0), pipeline_mode=pl.Buffered(1)),
            full(b_in), full(conv_w), full(conv_b),
            full(pool_w_bf), full(pool_b), full(pool_scale),
        ],
        out_specs=[tile(dh), tile(dh), tile(dp), tile(n_gate)],
        out_shape=[
            jax.ShapeDtypeStruct((n_tok, dh), F32),
            jax.ShapeDtypeStruct((n_tok, dh), BF16),
            jax.ShapeDtypeStruct((n_tok, dp), BF16),
            jax.ShapeDtypeStruct((n_tok, n_gate), BF16),
        ],
        compiler_params=pltpu.CompilerParams(
            dimension_semantics=("parallel",), vmem_limit_bytes=INPROJ_VMEM),
        name="inproj",
    )(x2, x2, x2, g_norm, w_in_bf, b_in, conv_w, conv_b, pool_w_bf, pool_b, pool_scale)


def _hyena_kernel(u_ref, gate_ref, d_ref, kre_ref, kim_ref, cs_ref, y_ref, av_ref, pq_ref):
    seq = u_ref.shape[0]
    lb = cs_ref.shape[0]
    nb = seq // lb
    mid = nb - 1
    blocks = [slice(b * lb, (b + 1) * lb) for b in range(nb)]

    for b, rows in enumerate(blocks):
        ub = u_ref[rows, :].astype(BF16)
        av_ref[2 * b] = _dot(cs_ref[:, 0:lb], ub).astype(av_ref.dtype)
        av_ref[2 * b + 1] = _dot(cs_ref[:, lb:2 * lb], ub).astype(av_ref.dtype)

    for a, rows in enumerate(blocks):
        p = q = None
        for b in range(nb):
            kr, ki = kre_ref[mid + a - b], kim_ref[mid + a - b]
            ur, vi = av_ref[2 * b], av_ref[2 * b + 1]
            tp = kr * ur + ki * vi
            tq = kr * vi - ki * ur
            p = tp if p is None else p + tp
            q = tq if q is None else q + tq
        pq_ref[a, 0:lb, :] = p.astype(BF16)
        pq_ref[a, lb:2 * lb, :] = q.astype(BF16)
        y = _dot(cs_ref[...], pq_ref[a])
        y = y + u_ref[rows, :] * d_ref[...]
        y_ref[rows, :] = (y * gate_ref[rows, :].astype(F32)).astype(BF16)


def _hyena(u, gate, hyena_d, kre, kim, cs, batch, seq):
    dh = hyena_d.shape[1]
    cw = HYENA_COLS
    lb = cs.shape[0]
    nb = seq // lb
    act = pl.BlockSpec((seq, cw), lambda c, b: (b, c))
    spectra = pl.BlockSpec((2 * nb - 1, lb, cw), lambda c, b: (0, 0, c),
                           pipeline_mode=pl.Buffered(1))
    return pl.pallas_call(
        _hyena_kernel,
        grid=(dh // cw, batch),
        in_specs=[
            act, act,
            pl.BlockSpec((1, cw), lambda c, b: (0, c)),
            spectra, spectra,
            pl.BlockSpec(cs.shape, lambda c, b: (0, 0)),
        ],
        out_specs=act,
        out_shape=jax.ShapeDtypeStruct((batch * seq, dh), BF16),
        scratch_shapes=[
            pltpu.VMEM((2 * nb, lb, cw), SPEC_DTYPE),
            pltpu.VMEM((nb, 2 * lb, cw), BF16),
        ],
        compiler_params=pltpu.CompilerParams(
            dimension_semantics=("arbitrary", "arbitrary"), vmem_limit_bytes=HYENA_VMEM),
        name="hyena",
    )(u, gate, hyena_d, kre, kim, cs)


def _merge_kernel(yh_ref, yp_ref, g0_ref, g1_ref, x_ref, wh_ref, wp_ref, wo_ref, gf_ref, o_ref):
    out_h = _dot(yh_ref[...], wh_ref[...])
    out_p = _dot(yp_ref[...], wp_ref[...])
    merged = (jax.nn.sigmoid(g0_ref[...].astype(F32)) * out_h
              + jax.nn.sigmoid(g1_ref[...].astype(F32)) * out_p)
    res = x_ref[...] + _dot(merged.astype(BF16), wo_ref[...])
    o_ref[...] = _rms_scale(res) * gf_ref[...]


def _merge(yh, yp, gates, x2, wh_bf, wp_bf, wo_bf, g_final, tm=1024):
    n_tok, d = x2.shape
    tile = lambda col: pl.BlockSpec((tm, d), lambda i: (i, col))
    resident = lambda a: pl.BlockSpec(a.shape, lambda i: (0, 0), pipeline_mode=pl.Buffered(1))
    return pl.pallas_call(
        _merge_kernel,
        grid=(n_tok // tm,),
        in_specs=[
            tile(0), tile(0), tile(0), tile(1), tile(0),
            resident(wh_bf), resident(wp_bf), resident(wo_bf),
            pl.BlockSpec((1, d), lambda i: (0, 0)),
        ],
        out_specs=tile(0),
        out_shape=jax.ShapeDtypeStruct((n_tok, d), F32),
        compiler_params=pltpu.CompilerParams(
            dimension_semantics=("parallel",), vmem_limit_bytes=MERGE_VMEM),
        name="merge",
    )(yh, yp, gates, gates, x2, wh_bf, wp_bf, wo_bf, g_final)


def _filter_features(seq):
    t = jnp.linspace(0.0, 1.0, seq, dtype=F32)[:, None]
    bands = (FILTER_EMB - 1) // 2
    w = 2.0 * math.pi * jnp.arange(seq, dtype=F32) / seq
    f = jnp.linspace(1e-4, bands - 1, bands, dtype=F32)
    ang = w[:, None] * f[None, :]
    return jnp.concatenate([t, jnp.cos(ang), -jnp.sin(ang)], axis=-1)


def kernel(x, g_norm, w_in, b_in, conv_w, conv_b, filt_w1, filt_b1, filt_w2, filt_b2, filt_w3, filt_b3, filt_w4, filt_freq, hyena_d, w_hyena_out, pool_w, pool_b, pool_scale, w_pool_out, w_out, g_final):
    batch, seq, d = x.shape
    dh = hyena_d.shape[1]
    dp = pool_b.shape[1]
    assert g_norm.shape[0] == 1, "single-layer block"
    assert d == dh == dp and conv_b.shape[1] == 3 * dh and w_in.shape[2] == 4 * dh + 2 * dp + 2 * d
    assert dh % HYENA_COLS == 0 and seq % ROW_CHUNK == 0
    assert seq % CONV_BLOCKS == 0

    cs = _transform_matrices(seq // CONV_BLOCKS)
    z_pad = jnp.pad(_filter_features(seq), ((0, 0), (0, LANES - FILTER_EMB)))
    w1_pad = jnp.pad(filt_w1[0], ((0, LANES - FILTER_EMB), (0, 0)))
    deltas = jnp.abs(jnp.linspace(MIN_DECAY, MAX_DECAY, dh, dtype=F32))[None, :]
    x2 = x.reshape(batch * seq, d)
    u, gate, yp, gates = _inproj(x2, g_norm, w_in[0].astype(BF16), b_in, conv_w[0], conv_b,
                                 pool_w[0].astype(BF16), pool_b, pool_scale, seq)
    kre, kim = _filter_spectra(z_pad, w1_pad, filt_b1, filt_w2[0], filt_b2, filt_w3[0], filt_b3,
                               filt_freq, filt_w4[0], deltas, cs)
    yh = _hyena(u, gate, hyena_d, kre, kim, cs, batch, seq)
    out = _merge(yh, yp, gates, x2, w_hyena_out[0].astype(BF16), w_pool_out[0].astype(BF16),
                 w_out[0].astype(BF16), g_final[None, :])
    return out.reshape(batch, seq, d)
```

```python
import functools
import math

import jax
import jax.numpy as jnp
from jax import lax
from jax.experimental import pallas as pl
from jax.experimental.pallas import tpu as pltpu

FILTER_EMB = 33
FAST_DECAY_PCT = 0.3
SLOW_DECAY_PCT = 1.5
DECAY_TARGET = 1e-2
MAX_DECAY = math.log(DECAY_TARGET) / FAST_DECAY_PCT
MIN_DECAY = math.log(DECAY_TARGET) / SLOW_DECAY_PCT
POOL_WINDOWS = (2, 4, 8, 16)
NORM_EPS = 1e-6

LANES = 128
SUBLANES = 8
MXU_DIM = 256
HYENA_COLS = MXU_DIM
ROW_CHUNK = 1024
MLP_ROWS = 256
CONV_BLOCKS = 4
MIB = 1024 * 1024
FILTER_VMEM = 40 * MIB
INPROJ_VMEM = 55 * MIB
HYENA_VMEM = 52 * MIB
MERGE_VMEM = 47 * MIB

F32 = jnp.float32
BF16 = jnp.bfloat16
SPEC_DTYPE = BF16


def _dot(a, b):
    return jnp.dot(a, b, preferred_element_type=F32)


def _dot_split(a, b):
    a_hi, b_hi = a.astype(BF16), b.astype(BF16)
    a_lo = (a - a_hi.astype(F32)).astype(BF16)
    b_lo = (b - b_hi.astype(F32)).astype(BF16)
    return _dot(a_hi, b_hi) + _dot(a_lo, b_hi) + _dot(a_hi, b_lo)


def _rms_scale(x):
    return x * lax.rsqrt(jnp.mean(x * x, axis=-1, keepdims=True) + NORM_EPS)


def _transform_matrices(lb):
    period = 8 * lb
    odd = 2 * jnp.arange(lb, dtype=jnp.int32) + 1
    ang = ((odd[:, None] * odd[None, :]) % period).astype(F32) * (2.0 * math.pi / period)
    return jnp.concatenate([jnp.cos(ang), jnp.sin(ang)], axis=1).astype(BF16)


def _filter_kernel(z_ref, w1_ref, b1_ref, w2_ref, b2_ref, w3_ref, b3_ref, fr_ref,
                   w4f_ref, w4b_ref, dl_ref, cs_ref, kre_ref, kim_ref,
                   h_ref, cphi_ref, sphi_ref, sgn_ref, kk_ref, xr_ref, xi_ref):
    seq = z_ref.shape[0]
    lb = cs_ref.shape[0]
    nb = seq // lb
    cb = dl_ref.shape[1]
    scale = 1.0 / lb

    @pl.when(pl.program_id(0) == 0)
    def _():
        fr = fr_ref[...]

        def mlp(i, carry):
            rows = pl.ds(pl.multiple_of(i * MLP_ROWS, MLP_ROWS), MLP_ROWS)
            h = jnp.sin(fr * (_dot_split(z_ref[rows, :], w1_ref[...]) + b1_ref[...]))
            h = jnp.sin(fr * (_dot_split(h, w2_ref[...]) + b2_ref[...]))
            h_ref[rows, :] = jnp.sin(fr * (_dot_split(h, w3_ref[...]) + b3_ref[...]))
            return carry

        lax.fori_loop(0, seq // MLP_ROWS, mlp, 0)
        f_idx = lax.broadcasted_iota(jnp.int32, cphi_ref.shape, 0)
        phi = (f_idx.astype(F32) + 0.5) * (math.pi / (2 * lb))
        cphi_ref[...] = jnp.cos(phi) * scale
        sphi_ref[...] = jnp.sin(phi) * scale
        sgn_ref[...] = (1 - 2 * (f_idx & 1)).astype(F32)

    h = h_ref[...]
    decay = jnp.exp(-z_ref[:, 0:1] * dl_ref[...])
    taps = (_dot_split(h, w4f_ref[...]) * decay, _dot_split(h, w4b_ref[...]) * decay)
    taps = tuple(k.astype(BF16) for k in taps)
    for side, k in enumerate(taps):
        for s_ in range(nb):
            c0 = (side * nb + s_) * cb
            kk_ref[:, c0:c0 + cb] = k[s_ * lb:(s_ + 1) * lb]

    group = 2 * cb
    tile = lambda ref, width: jnp.concatenate([ref[...]] * (width // LANES), axis=1)
    c_phi, s_phi = tile(cphi_ref, group), tile(sphi_ref, group)
    for g in range(2 * nb * cb // group):
        cols = slice(g * group, (g + 1) * group)
        kk = kk_ref[:, cols]
        a = _dot(cs_ref[:, 0:lb], kk)
        b = _dot(cs_ref[:, lb:2 * lb], kk)
        xr_ref[:, cols] = c_phi * a + s_phi * b
        xi_ref[:, cols] = s_phi * a - c_phi * b

    sgn = tile(sgn_ref, cb)

    def seg(ref, side, s_):
        c0 = (side * nb + s_) * cb
        return ref[:, c0:c0 + cb]

    def causal_block(side, delta):
        r0 = (delta - 1) * lb
        first = taps[side][r0:r0 + 1].astype(F32) * scale
        re = seg(xr_ref, side, delta) - sgn * seg(xi_ref, side, delta - 1)
        im = seg(xi_ref, side, delta) + sgn * (seg(xr_ref, side, delta - 1) - first)
        return re, im

    mid = nb - 1
    out = kre_ref.dtype
    kre_ref[mid] = (seg(xr_ref, 0, 0) + seg(xr_ref, 1, 0)).astype(out)
    kim_ref[mid] = (seg(xi_ref, 0, 0) - seg(xi_ref, 1, 0)).astype(out)
    for delta in range(1, nb):
        re, im = causal_block(0, delta)
        kre_ref[mid + delta] = re.astype(out)
        kim_ref[mid + delta] = im.astype(out)
        re, im = causal_block(1, delta)
        kre_ref[mid - delta] = re.astype(out)
        kim_ref[mid - delta] = (-im).astype(out)


def _filter_spectra(z_pad, w1_pad, b1, w2, b2, w3, b3, freq, w4, deltas, cs):
    seq = z_pad.shape[0]
    lb = cs.shape[0]
    nb = seq // lb
    hf = w2.shape[0]
    dh = deltas.shape[1]
    cb = MXU_DIM
    nblk = dh // cb
    full = lambda a: pl.BlockSpec(a.shape, lambda c: (0,) * a.ndim)
    return pl.pallas_call(
        _filter_kernel,
        grid=(nblk,),
        in_specs=[
            full(z_pad), full(w1_pad), full(b1), full(w2), full(b2), full(w3), full(b3), full(freq),
            pl.BlockSpec((hf, cb), lambda c: (0, c)),
            pl.BlockSpec((hf, cb), lambda c: (0, nblk + c)),
            pl.BlockSpec((1, cb), lambda c: (0, c)),
            full(cs),
        ],
        out_specs=[pl.BlockSpec((2 * nb - 1, lb, cb), lambda c: (0, 0, c))] * 2,
        out_shape=[jax.ShapeDtypeStruct((2 * nb - 1, lb, dh), SPEC_DTYPE)] * 2,
        scratch_shapes=[
            pltpu.VMEM((seq, hf), F32),
            pltpu.VMEM((lb, LANES), F32),
            pltpu.VMEM((lb, LANES), F32),
            pltpu.VMEM((lb, LANES), F32),
            pltpu.VMEM((lb, 2 * nb * cb), BF16),
            pltpu.VMEM((lb, 2 * nb * cb), F32),
            pltpu.VMEM((lb, 2 * nb * cb), F32),
        ],
        compiler_params=pltpu.CompilerParams(
            dimension_semantics=("arbitrary",), vmem_limit_bytes=FILTER_VMEM),
        name="filter_spectra",
    )(z_pad, w1_pad, b1, w2, b2, w3, b3, freq, w4, w4, deltas, cs)


def _conv3(p, w, bias, lo, n):
    n_rows = p.shape[0]
    assert 1 <= lo and lo + n < n_rows
    dn = pltpu.roll(p, 1, 0)
    up = pltpu.roll(p, n_rows - 1, 0)
    out = dn * w[0:1] + p * w[1:2] + up * w[2:3] + bias
    return out[lo:lo + n]


def _window_mean_minus_self(p, win, lo, n, at_start, at_end):
    n_rows = p.shape[0]
    half = win // 2
    sl = SUBLANES
    assert lo >= win and lo + n + win <= n_rows and half <= sl
    s = p
    w = 1
    while w < win:
        s = s + pltpu.roll(s, w, 0)
        w *= 2
    if half > 1:
        s = pltpu.roll(s, n_rows - (half - 1), 0)
    total = s[lo:lo + n]
    edge = lax.broadcasted_iota(jnp.int32, (sl, p.shape[1]), 0)
    cnt_first = jnp.where(at_start, jnp.minimum(edge + half, win), win).astype(F32)
    cnt_last = jnp.where(at_end, jnp.minimum(sl - edge + half, win), win).astype(F32)
    mean = jnp.concatenate([total[0:sl] / cnt_first,
                            total[sl:n - sl] * (1.0 / win),
                            total[n - sl:] / cnt_last], axis=0)
    return mean - p[lo:lo + n]


def _inproj_kernel(xp_ref, x_ref, xn_ref, g_ref, w_ref, b_ref, cw_ref, cb_ref,
                   pw_ref, pb_ref, ps_ref, u_ref, gate_ref, yp_ref, p_ref, *, chunks_per_seq):
    i = pl.program_id(0)
    rows = x_ref.shape[0]
    halo = xp_ref.shape[0]
    dh = u_ref.shape[1]
    cb = MXU_DIM
    at_start = i % chunks_per_seq == 0
    at_end = i % chunks_per_seq == chunks_per_seq - 1

    xw = jnp.concatenate([xp_ref[...], x_ref[...], xn_ref[...]], axis=0)
    h = (_rms_scale(xw) * g_ref[...]).astype(BF16)

    h_main = h[halo:halo + rows]

    def project(rows_h, c0):
        return _dot(rows_h, w_ref[:, c0:c0 + cb]) + b_ref[:, c0:c0 + cb]

    def conv_input(c0):
        p = project(h, c0)
        return jnp.concatenate(
            [jnp.where(at_start, 0.0, p[0:halo]), p[halo:halo + rows],
             jnp.where(at_end, 0.0, p[halo + rows:])], axis=0)

    def conv(c0):
        return _conv3(conv_input(c0), cw_ref[:, c0:c0 + cb], cb_ref[:, c0:c0 + cb], halo, rows)

    dp = yp_ref.shape[1]
    assert dp == len(POOL_WINDOWS) * cb
    def hyena_group(c):
        lanes = slice(c * cb, (c + 1) * cb)
        u_ref[:, lanes] = conv(dh + c * cb) * conv(2 * dh + c * cb)
        z = project(h_main, 3 * dh + c * cb)
        gate_ref[:, lanes] = (conv(c * cb) * jax.nn.silu(z)).astype(BF16)

    def pool_group(g, win):
        lanes = slice(g * cb, (g + 1) * cb)
        pooled = _window_mean_minus_self(conv_input(4 * dh + g * cb), win, halo, rows,
                                         at_start, at_end)
        y = _dot(pooled.astype(BF16), pw_ref[g]) + pb_ref[:, lanes]
        z = project(h_main, 4 * dh + dp + g * cb)
        yp_ref[:, lanes] = (y * ps_ref[:, lanes] * jax.nn.silu(z)).astype(BF16)

    for c in range(dh // cb):
        hyena_group(c)
    for g, win in enumerate(POOL_WINDOWS):
        pool_group(g, win)

    for c0 in range(0, p_ref.shape[1], cb):
        p_ref[:, c0:c0 + cb] = project(h_main, 4 * dh + 2 * dp + c0).astype(BF16)


def _inproj(x2, g_norm, w_in_bf, b_in, conv_w, conv_b, pool_w_bf, pool_b, pool_scale, seq):
    n_tok, d = x2.shape
    cols = w_in_bf.shape[1]
    dh = conv_b.shape[1] // 3
    dp = pool_b.shape[1]
    n_gate = cols - 4 * dh - 2 * dp
    rows = ROW_CHUNK
    halo = 2 * SUBLANES
    per_halo = rows // halo
    last_halo = n_tok // halo - 1
    full = lambda a: pl.BlockSpec(a.shape, lambda i: (0,) * a.ndim)
    tile = lambda width: pl.BlockSpec((rows, width), lambda i: (i, 0))
    return pl.pallas_call(
        functools.partial(_inproj_kernel, chunks_per_seq=seq // rows),
        grid=(n_tok // rows,),
        in_specs=[
            pl.BlockSpec((halo, d), lambda i: (jnp.maximum(i * per_halo - 1, 0), 0)),
            tile(d),
            pl.BlockSpec((halo, d), lambda i: (jnp.minimum((i + 1) * per_halo, last_halo), 0)),
            full(g_norm),
            pl.BlockSpec(w_in_bf.shape, lambda i: (0, 0), pipeline_mode=pl.Buffered(1)),
            full(b_in), full(conv_w), full(conv_b),
            full(pool_w_bf), full(pool_b), full(pool_scale),
        ],
        out_specs=[tile(dh), tile(dh), tile(dp), tile(n_gate)],
        out_shape=[
            jax.ShapeDtypeStruct((n_tok, dh), F32),
            jax.ShapeDtypeStruct((n_tok, dh), BF16),
            jax.ShapeDtypeStruct((n_tok, dp), BF16),
            jax.ShapeDtypeStruct((n_tok, n_gate), BF16),
        ],
        compiler_params=pltpu.CompilerParams(
            dimension_semantics=("parallel",), vmem_limit_bytes=INPROJ_VMEM),
        name="inproj",
    )(x2, x2, x2, g_norm, w_in_bf, b_in, conv_w, conv_b, pool_w_bf, pool_b, pool_scale)


def _hyena_kernel(u_ref, gate_ref, d_ref, kre_ref, kim_ref, cs_ref, y_ref, av_ref, pq_ref):
    seq = u_ref.shape[0]
    lb = cs_ref.shape[0]
    nb = seq // lb
    mid = nb - 1
    blocks = [slice(b * lb, (b + 1) * lb) for b in range(nb)]

    for b, rows in enumerate(blocks):
        ub = u_ref[rows, :].astype(BF16)
        av_ref[2 * b] = _dot(cs_ref[:, 0:lb], ub).astype(av_ref.dtype)
        av_ref[2 * b + 1] = _dot(cs_ref[:, lb:2 * lb], ub).astype(av_ref.dtype)

    for a, rows in enumerate(blocks):
        p = q = None
        for b in range(nb):
            kr, ki = kre_ref[mid + a - b], kim_ref[mid + a - b]
            ur, vi = av_ref[2 * b], av_ref[2 * b + 1]
            tp = kr * ur + ki * vi
            tq = kr * vi - ki * ur
            p = tp if p is None else p + tp
            q = tq if q is None else q + tq
        pq_ref[a, 0:lb, :] = p.astype(BF16)
        pq_ref[a, lb:2 * lb, :] = q.astype(BF16)
        y = _dot(cs_ref[...], pq_ref[a])
        y = y + u_ref[rows, :] * d_ref[...]
        y_ref[rows, :] = (y * gate_ref[rows, :].astype(F32)).astype(BF16)


def _hyena(u, gate, hyena_d, kre, kim, cs, batch, seq):
    dh = hyena_d.shape[1]
    cw = HYENA_COLS
    lb = cs.shape[0]
    nb = seq // lb
    act = pl.BlockSpec((seq, cw), lambda c, b: (b, c))
    spectra = pl.BlockSpec((2 * nb - 1, lb, cw), lambda c, b: (0, 0, c),
                           pipeline_mode=pl.Buffered(1))
    return pl.pallas_call(
        _hyena_kernel,
        grid=(dh // cw, batch),
        in_specs=[
            act, act,
            pl.BlockSpec((1, cw), lambda c, b: (0, c)),
            spectra, spectra,
            pl.BlockSpec(cs.shape, lambda c, b: (0, 0)),
        ],
        out_specs=act,
        out_shape=jax.ShapeDtypeStruct((batch * seq, dh), BF16),
        scratch_shapes=[
            pltpu.VMEM((2 * nb, lb, cw), SPEC_DTYPE),
            pltpu.VMEM((nb, 2 * lb, cw), BF16),
        ],
        compiler_params=pltpu.CompilerParams(
            dimension_semantics=("arbitrary", "arbitrary"), vmem_limit_bytes=HYENA_VMEM),
        name="hyena",
    )(u, gate, hyena_d, kre, kim, cs)


def _merge_kernel(yh_ref, yp_ref, g0_ref, g1_ref, x_ref, wh_ref, wp_ref, wo_ref, gf_ref, o_ref):
    out_h = _dot(yh_ref[...], wh_ref[...])
    out_p = _dot(yp_ref[...], wp_ref[...])
    merged = (jax.nn.sigmoid(g0_ref[...].astype(F32)) * out_h
              + jax.nn.sigmoid(g1_ref[...].astype(F32)) * out_p)
    res = x_ref[...] + _dot(merged.astype(BF16), wo_ref[...])
    o_ref[...] = _rms_scale(res) * gf_ref[...]


def _merge(yh, yp, gates, x2, wh_bf, wp_bf, wo_bf, g_final, tm=1024):
    n_tok, d = x2.shape
    tile = lambda col: pl.BlockSpec((tm, d), lambda i: (i, col))
    resident = lambda a: pl.BlockSpec(a.shape, lambda i: (0, 0), pipeline_mode=pl.Buffered(1))
    return pl.pallas_call(
        _merge_kernel,
        grid=(n_tok // tm,),
        in_specs=[
            tile(0), tile(0), tile(0), tile(1), tile(0),
            resident(wh_bf), resident(wp_bf), resident(wo_bf),
            pl.BlockSpec((1, d), lambda i: (0, 0)),
        ],
        out_specs=tile(0),
        out_shape=jax.ShapeDtypeStruct((n_tok, d), F32),
        compiler_params=pltpu.CompilerParams(
            dimension_semantics=("parallel",), vmem_limit_bytes=MERGE_VMEM),
        name="merge",
    )(yh, yp, gates, gates, x2, wh_bf, wp_bf, wo_bf, g_final)


def _filter_features(seq):
    t = jnp.linspace(0.0, 1.0, seq, dtype=F32)[:, None]
    bands = (FILTER_EMB - 1) // 2
    w = 2.0 * math.pi * jnp.arange(seq, dtype=F32) / seq
    f = jnp.linspace(1e-4, bands - 1, bands, dtype=F32)
    ang = w[:, None] * f[None, :]
    return jnp.concatenate([t, jnp.cos(ang), -jnp.sin(ang)], axis=-1)


def kernel(x, g_norm, w_in, b_in, conv_w, conv_b, filt_w1, filt_b1, filt_w2, filt_b2, filt_w3, filt_b3, filt_w4, filt_freq, hyena_d, w_hyena_out, pool_w, pool_b, pool_scale, w_pool_out, w_out, g_final):
    batch, seq, d = x.shape
    dh = hyena_d.shape[1]
    dp = pool_b.shape[1]
    assert g_norm.shape[0] == 1, "single-layer block"
    assert d == dh == dp and conv_b.shape[1] == 3 * dh and w_in.shape[2] == 4 * dh + 2 * dp + 2 * d
    assert dh % HYENA_COLS == 0 and seq % ROW_CHUNK == 0
    assert seq % CONV_BLOCKS == 0

    cs = _transform_matrices(seq // CONV_BLOCKS)
    z_pad = jnp.pad(_filter_features(seq), ((0, 0), (0, LANES - FILTER_EMB)))
    w1_pad = jnp.pad(filt_w1[0], ((0, LANES - FILTER_EMB), (0, 0)))
    deltas = jnp.abs(jnp.linspace(MIN_DECAY, MAX_DECAY, dh, dtype=F32))[None, :]
    x2 = x.reshape(batch * seq, d)
    u, gate, yp, gates = _inproj(x2, g_norm, w_in[0].astype(BF16), b_in, conv_w[0], conv_b,
                                 pool_w[0].astype(BF16), pool_b, pool_scale, seq)
    kre, kim = _filter_spectra(z_pad, w1_pad, filt_b1, filt_w2[0], filt_b2, filt_w3[0], filt_b3,
                               filt_freq, filt_w4[0], deltas, cs)
    yh = _hyena(u, gate, hyena_d, kre, kim, cs, batch, seq)
    out = _merge(yh, yp, gates, x2, w_hyena_out[0].astype(BF16), w_pool_out[0].astype(BF16),
                 w_out[0].astype(BF16), g_final[None, :])
    return out.reshape(batch, seq, d)
```

```python
import functools
import math

import jax
import jax.numpy as jnp
from jax import lax
from jax.experimental import pallas as pl
from jax.experimental.pallas import tpu as pltpu

FILTER_EMB = 33
FAST_DECAY_PCT = 0.3
SLOW_DECAY_PCT = 1.5
DECAY_TARGET = 1e-2
MAX_DECAY = math.log(DECAY_TARGET) / FAST_DECAY_PCT
MIN_DECAY = math.log(DECAY_TARGET) / SLOW_DECAY_PCT
POOL_WINDOWS = (2, 4, 8, 16)
NORM_EPS = 1e-6

LANES = 128
SUBLANES = 8
MXU_DIM = 256
HYENA_COLS = MXU_DIM
ROW_CHUNK = 1024
MLP_ROWS = 256
CONV_BLOCKS = 4
MIB = 1024 * 1024
INPROJ_VMEM = 55 * MIB
HYENA_VMEM = 36 * MIB
MERGE_VMEM = 47 * MIB

F32 = jnp.float32
BF16 = jnp.bfloat16
SPEC_DTYPE = BF16


def _dot(a, b):
    return jnp.dot(a, b, preferred_element_type=F32)


def _dot_split(a, b):
    a_hi, b_hi = a.astype(BF16), b.astype(BF16)
    a_lo = (a - a_hi.astype(F32)).astype(BF16)
    b_lo = (b - b_hi.astype(F32)).astype(BF16)
    return _dot(a_hi, b_hi) + _dot(a_lo, b_hi) + _dot(a_hi, b_lo)


def _rms_scale(x):
    return x * lax.rsqrt(jnp.mean(x * x, axis=-1, keepdims=True) + NORM_EPS)


def _transform_matrices(lb):
    period = 8 * lb
    odd = 2 * jnp.arange(lb, dtype=jnp.int32) + 1
    ang = ((odd[:, None] * odd[None, :]) % period).astype(F32) * (2.0 * math.pi / period)
    return jnp.concatenate([jnp.cos(ang), jnp.sin(ang)], axis=1).astype(BF16)


def _filter_block(first, z_ref, w1_ref, b1_ref, w2_ref, b2_ref, w3_ref, b3_ref, fr_ref,
                  w4f_ref, w4b_ref, dl_ref, cs_ref, kre_ref, kim_ref,
                  h_ref, cphi_ref, sphi_ref, sgn_ref, kk_ref, xr_ref, xi_ref):
    seq = z_ref.shape[0]
    lb = cs_ref.shape[0]
    nb = seq // lb
    cb = dl_ref.shape[1]
    scale = 1.0 / lb

    @pl.when(first)
    def _():
        fr = fr_ref[...]

        def mlp(i, carry):
            rows = pl.ds(pl.multiple_of(i * MLP_ROWS, MLP_ROWS), MLP_ROWS)
            h = jnp.sin(fr * (_dot_split(z_ref[rows, :], w1_ref[...]) + b1_ref[...]))
            h = jnp.sin(fr * (_dot_split(h, w2_ref[...]) + b2_ref[...]))
            h_ref[rows, :] = jnp.sin(fr * (_dot_split(h, w3_ref[...]) + b3_ref[...]))
            return carry

        lax.fori_loop(0, seq // MLP_ROWS, mlp, 0)
        f_idx = lax.broadcasted_iota(jnp.int32, cphi_ref.shape, 0)
        phi = (f_idx.astype(F32) + 0.5) * (math.pi / (2 * lb))
        cphi_ref[...] = jnp.cos(phi) * scale
        sphi_ref[...] = jnp.sin(phi) * scale
        sgn_ref[...] = (1 - 2 * (f_idx & 1)).astype(F32)

    h = h_ref[...]
    decay = jnp.exp(-z_ref[:, 0:1] * dl_ref[...])
    taps = (_dot_split(h, w4f_ref[...]) * decay, _dot_split(h, w4b_ref[...]) * decay)
    taps = tuple(k.astype(BF16) for k in taps)
    for side, k in enumerate(taps):
        for s_ in range(nb):
            c0 = (side * nb + s_) * cb
            kk_ref[:, c0:c0 + cb] = k[s_ * lb:(s_ + 1) * lb]

    group = 2 * cb
    tile = lambda ref, width: jnp.concatenate([ref[...]] * (width // LANES), axis=1)
    c_phi, s_phi = tile(cphi_ref, group), tile(sphi_ref, group)
    for g in range(2 * nb * cb // group):
        cols = slice(g * group, (g + 1) * group)
        kk = kk_ref[:, cols]
        a = _dot(cs_ref[:, 0:lb], kk)
        b = _dot(cs_ref[:, lb:2 * lb], kk)
        xr_ref[:, cols] = c_phi * a + s_phi * b
        xi_ref[:, cols] = s_phi * a - c_phi * b

    sgn = tile(sgn_ref, cb)

    def seg(ref, side, s_):
        c0 = (side * nb + s_) * cb
        return ref[:, c0:c0 + cb]

    def causal_block(side, delta):
        r0 = (delta - 1) * lb
        first = taps[side][r0:r0 + 1].astype(F32) * scale
        re = seg(xr_ref, side, delta) - sgn * seg(xi_ref, side, delta - 1)
        im = seg(xi_ref, side, delta) + sgn * (seg(xr_ref, side, delta - 1) - first)
        return re, im

    mid = nb - 1
    out = kre_ref.dtype
    kre_ref[mid] = (seg(xr_ref, 0, 0) + seg(xr_ref, 1, 0)).astype(out)
    kim_ref[mid] = (seg(xi_ref, 0, 0) - seg(xi_ref, 1, 0)).astype(out)
    for delta in range(1, nb):
        re, im = causal_block(0, delta)
        kre_ref[mid + delta] = re.astype(out)
        kim_ref[mid + delta] = im.astype(out)
        re, im = causal_block(1, delta)
        kre_ref[mid - delta] = re.astype(out)
        kim_ref[mid - delta] = (-im).astype(out)


def _conv3(p, w, bias, lo, n):
    n_rows = p.shape[0]
    assert 1 <= lo and lo + n < n_rows
    dn = pltpu.roll(p, 1, 0)
    up = pltpu.roll(p, n_rows - 1, 0)
    out = dn * w[0:1] + p * w[1:2] + up * w[2:3] + bias
    return out[lo:lo + n]


def _window_mean_minus_self(p, win, lo, n, at_start, at_end):
    n_rows = p.shape[0]
    half = win // 2
    sl = SUBLANES
    assert lo >= win and lo + n + win <= n_rows and half <= sl
    s = p
    w = 1
    while w < win:
        s = s + pltpu.roll(s, w, 0)
        w *= 2
    if half > 1:
        s = pltpu.roll(s, n_rows - (half - 1), 0)
    total = s[lo:lo + n]
    edge = lax.broadcasted_iota(jnp.int32, (sl, p.shape[1]), 0)
    cnt_first = jnp.where(at_start, jnp.minimum(edge + half, win), win).astype(F32)
    cnt_last = jnp.where(at_end, jnp.minimum(sl - edge + half, win), win).astype(F32)
    mean = jnp.concatenate([total[0:sl] / cnt_first,
                            total[sl:n - sl] * (1.0 / win),
                            total[n - sl:] / cnt_last], axis=0)
    return mean - p[lo:lo + n]


def _inproj_kernel(xp_ref, x_ref, xn_ref, g_ref, w_ref, b_ref, cw_ref, cb_ref,
                   pw_ref, pb_ref, ps_ref, u_ref, gate_ref, yp_ref, p_ref, *, chunks_per_seq):
    i = pl.program_id(0)
    rows = x_ref.shape[0]
    halo = xp_ref.shape[0]
    dh = u_ref.shape[1]
    cb = MXU_DIM
    at_start = i % chunks_per_seq == 0
    at_end = i % chunks_per_seq == chunks_per_seq - 1

    xw = jnp.concatenate([xp_ref[...], x_ref[...], xn_ref[...]], axis=0)
    h = (_rms_scale(xw) * g_ref[...]).astype(BF16)

    h_main = h[halo:halo + rows]

    def project(rows_h, c0):
        return _dot(rows_h, w_ref[:, c0:c0 + cb]) + b_ref[:, c0:c0 + cb]

    def conv_input(c0):
        p = project(h, c0)
        return jnp.concatenate(
            [jnp.where(at_start, 0.0, p[0:halo]), p[halo:halo + rows],
             jnp.where(at_end, 0.0, p[halo + rows:])], axis=0)

    def conv(c0):
        return _conv3(conv_input(c0), cw_ref[:, c0:c0 + cb], cb_ref[:, c0:c0 + cb], halo, rows)

    dp = yp_ref.shape[1]
    assert dp == len(POOL_WINDOWS) * cb
    def hyena_group(c):
        lanes = slice(c * cb, (c + 1) * cb)
        u_ref[:, lanes] = conv(dh + c * cb) * conv(2 * dh + c * cb)
        z = project(h_main, 3 * dh + c * cb)
        gate_ref[:, lanes] = (conv(c * cb) * jax.nn.silu(z)).astype(BF16)

    def pool_group(g, win):
        lanes = slice(g * cb, (g + 1) * cb)
        pooled = _window_mean_minus_self(conv_input(4 * dh + g * cb), win, halo, rows,
                                         at_start, at_end)
        y = _dot(pooled.astype(BF16), pw_ref[g]) + pb_ref[:, lanes]
        z = project(h_main, 4 * dh + dp + g * cb)
        yp_ref[:, lanes] = (y * ps_ref[:, lanes] * jax.nn.silu(z)).astype(BF16)

    for c in range(dh // cb):
        hyena_group(c)
    for g, win in enumerate(POOL_WINDOWS):
        pool_group(g, win)

    for c0 in range(0, p_ref.shape[1], cb):
        p_ref[:, c0:c0 + cb] = project(h_main, 4 * dh + 2 * dp + c0).astype(BF16)


def _inproj(x2, g_norm, w_in_bf, b_in, conv_w, conv_b, pool_w_bf, pool_b, pool_scale, seq):
    n_tok, d = x2.shape
    cols = w_in_bf.shape[1]
    dh = conv_b.shape[1] // 3
    dp = pool_b.shape[1]
    n_gate = cols - 4 * dh - 2 * dp
    rows = ROW_CHUNK
    halo = 2 * SUBLANES
    per_halo = rows // halo
    last_halo = n_tok // halo - 1
    full = lambda a: pl.BlockSpec(a.shape, lambda i: (0,) * a.ndim)
    tile = lambda width: pl.BlockSpec((rows, width), lambda i: (i, 0))
    return pl.pallas_call(
        functools.partial(_inproj_kernel, chunks_per_seq=seq // rows),
        grid=(n_tok // rows,),
        in_specs=[
            pl.BlockSpec((halo, d), lambda i: (jnp.maximum(i * per_halo - 1, 0), 0)),
            tile(d),
            pl.BlockSpec((halo, d), lambda i: (jnp.minimum((i + 1) * per_halo, last_halo), 0)),
            full(g_norm),
            pl.BlockSpec(w_in_bf.shape, lambda i: (0, 0), pipeline_mode=pl.Buffered(1)),
            full(b_in), full(conv_w), full(conv_b),
            full(pool_w_bf), full(pool_b), full(pool_scale),
        ],
        out_specs=[tile(dh), tile(dh), tile(dp), tile(n_gate)],
        out_shape=[
            jax.ShapeDtypeStruct((n_tok, dh), F32),
            jax.ShapeDtypeStruct((n_tok, dh), BF16),
            jax.ShapeDtypeStruct((n_tok, dp), BF16),
            jax.ShapeDtypeStruct((n_tok, n_gate), BF16),
        ],
        compiler_params=pltpu.CompilerParams(
            dimension_semantics=("parallel",), vmem_limit_bytes=INPROJ_VMEM),
        name="inproj",
    )(x2, x2, x2, g_norm, w_in_bf, b_in, conv_w, conv_b, pool_w_bf, pool_b, pool_scale)


def _hyena_kernel(u_ref, gate_ref, d_ref, cs_ref, *rest):
    filter_inputs, (y_ref, av_ref, pq_ref, kre_ref, kim_ref), filter_scratch = (
        rest[:11], rest[11:16], rest[16:])
    seq = u_ref.shape[0]
    lb = cs_ref.shape[0]
    nb = seq // lb
    mid = nb - 1
    blocks = [slice(b * lb, (b + 1) * lb) for b in range(nb)]

    @pl.when(pl.program_id(1) == 0)
    def _():
        _filter_block(pl.program_id(0) == 0, *filter_inputs, cs_ref, kre_ref, kim_ref,
                      *filter_scratch)

    for b, rows in enumerate(blocks):
        ub = u_ref[rows, :].astype(BF16)
        av_ref[2 * b] = _dot(cs_ref[:, 0:lb], ub).astype(av_ref.dtype)
        av_ref[2 * b + 1] = _dot(cs_ref[:, lb:2 * lb], ub).astype(av_ref.dtype)

    for a, rows in enumerate(blocks):
        p = q = None
        for b in range(nb):
            kr, ki = kre_ref[mid + a - b], kim_ref[mid + a - b]
            ur, vi = av_ref[2 * b], av_ref[2 * b + 1]
            tp = kr * ur + ki * vi
            tq = kr * vi - ki * ur
            p = tp if p is None else p + tp
            q = tq if q is None else q + tq
        pq_ref[a, 0:lb, :] = p.astype(BF16)
        pq_ref[a, lb:2 * lb, :] = q.astype(BF16)
        y = _dot(cs_ref[...], pq_ref[a])
        y = y + u_ref[rows, :] * d_ref[...]
        y_ref[rows, :] = (y * gate_ref[rows, :].astype(F32)).astype(BF16)


def _hyena(u, gate, hyena_d, cs, z_pad, w1_pad, b1, w2, b2, w3, b3, freq, w4, deltas, batch, seq):
    dh = hyena_d.shape[1]
    cw = HYENA_COLS
    lb = cs.shape[0]
    nb = seq // lb
    hf = w2.shape[0]
    nblk = dh // cw
    act = pl.BlockSpec((seq, cw), lambda c, b: (b, c))
    full = lambda a: pl.BlockSpec(a.shape, lambda c, b: (0,) * a.ndim)
    return pl.pallas_call(
        _hyena_kernel,
        grid=(nblk, batch),
        in_specs=[
            act, act,
            pl.BlockSpec((1, cw), lambda c, b: (0, c)),
            full(cs),
            full(z_pad), full(w1_pad), full(b1), full(w2), full(b2), full(w3), full(b3), full(freq),
            pl.BlockSpec((hf, cw), lambda c, b: (0, c)),
            pl.BlockSpec((hf, cw), lambda c, b: (0, nblk + c)),
            pl.BlockSpec((1, cw), lambda c, b: (0, c)),
        ],
        out_specs=act,
        out_shape=jax.ShapeDtypeStruct((batch * seq, dh), BF16),
        scratch_shapes=[
            pltpu.VMEM((2 * nb, lb, cw), SPEC_DTYPE),
            pltpu.VMEM((nb, 2 * lb, cw), BF16),
            pltpu.VMEM((2 * nb - 1, lb, cw), SPEC_DTYPE),
            pltpu.VMEM((2 * nb - 1, lb, cw), SPEC_DTYPE),
            pltpu.VMEM((seq, hf), F32),
            pltpu.VMEM((lb, LANES), F32),
            pltpu.VMEM((lb, LANES), F32),
            pltpu.VMEM((lb, LANES), F32),
            pltpu.VMEM((lb, 2 * nb * cw), BF16),
            pltpu.VMEM((lb, 2 * nb * cw), F32),
            pltpu.VMEM((lb, 2 * nb * cw), F32),
        ],
        compiler_params=pltpu.CompilerParams(
            dimension_semantics=("arbitrary", "arbitrary"), vmem_limit_bytes=HYENA_VMEM),
        name="hyena",
    )(u, gate, hyena_d, cs, z_pad, w1_pad, b1, w2, b2, w3, b3, freq, w4, w4, deltas)


def _merge_kernel(yh_ref, yp_ref, g0_ref, g1_ref, x_ref, wh_ref, wp_ref, wo_ref, gf_ref, o_ref):
    out_h = _dot(yh_ref[...], wh_ref[...])
    out_p = _dot(yp_ref[...], wp_ref[...])
    merged = (jax.nn.sigmoid(g0_ref[...].astype(F32)) * out_h
              + jax.nn.sigmoid(g1_ref[...].astype(F32)) * out_p)
    res = x_ref[...] + _dot(merged.astype(BF16), wo_ref[...])
    o_ref[...] = _rms_scale(res) * gf_ref[...]


def _merge(yh, yp, gates, x2, wh_bf, wp_bf, wo_bf, g_final, tm=1024):
    n_tok, d = x2.shape
    tile = lambda col: pl.BlockSpec((tm, d), lambda i: (i, col))
    resident = lambda a: pl.BlockSpec(a.shape, lambda i: (0, 0), pipeline_mode=pl.Buffered(1))
    return pl.pallas_call(
        _merge_kernel,
        grid=(n_tok // tm,),
        in_specs=[
            tile(0), tile(0), tile(0), tile(1), tile(0),
            resident(wh_bf), resident(wp_bf), resident(wo_bf),
            pl.BlockSpec((1, d), lambda i: (0, 0)),
        ],
        out_specs=tile(0),
        out_shape=jax.ShapeDtypeStruct((n_tok, d), F32),
        compiler_params=pltpu.CompilerParams(
            dimension_semantics=("parallel",), vmem_limit_bytes=MERGE_VMEM),
        name="merge",
    )(yh, yp, gates, gates, x2, wh_bf, wp_bf, wo_bf, g_final)


def _filter_features(seq):
    t = jnp.linspace(0.0, 1.0, seq, dtype=F32)[:, None]
    bands = (FILTER_EMB - 1) // 2
    w = 2.0 * math.pi * jnp.arange(seq, dtype=F32) / seq
    f = jnp.linspace(1e-4, bands - 1, bands, dtype=F32)
    ang = w[:, None] * f[None, :]
    return jnp.concatenate([t, jnp.cos(ang), -jnp.sin(ang)], axis=-1)


def kernel(x, g_norm, w_in, b_in, conv_w, conv_b, filt_w1, filt_b1, filt_w2, filt_b2, filt_w3, filt_b3, filt_w4, filt_freq, hyena_d, w_hyena_out, pool_w, pool_b, pool_scale, w_pool_out, w_out, g_final):
    batch, seq, d = x.shape
    dh = hyena_d.shape[1]
    dp = pool_b.shape[1]
    assert g_norm.shape[0] == 1, "single-layer block"
    assert d == dh == dp and conv_b.shape[1] == 3 * dh and w_in.shape[2] == 4 * dh + 2 * dp + 2 * d
    assert dh % HYENA_COLS == 0 and seq % ROW_CHUNK == 0
    assert seq % CONV_BLOCKS == 0

    cs = _transform_matrices(seq // CONV_BLOCKS)
    z_pad = jnp.pad(_filter_features(seq), ((0, 0), (0, LANES - FILTER_EMB)))
    w1_pad = jnp.pad(filt_w1[0], ((0, LANES - FILTER_EMB), (0, 0)))
    deltas = jnp.abs(jnp.linspace(MIN_DECAY, MAX_DECAY, dh, dtype=F32))[None, :]
    x2 = x.reshape(batch * seq, d)
    u, gate, yp, gates = _inproj(x2, g_norm, w_in[0].astype(BF16), b_in, conv_w[0], conv_b,
                                 pool_w[0].astype(BF16), pool_b, pool_scale, seq)
    yh = _hyena(u, gate, hyena_d, cs, z_pad, w1_pad, filt_b1, filt_w2[0], filt_b2, filt_w3[0],
                filt_b3, filt_freq, filt_w4[0], deltas, batch, seq)
    out = _merge(yh, yp, gates, x2, w_hyena_out[0].astype(BF16), w_pool_out[0].astype(BF16),
                 w_out[0].astype(BF16), g_final[None, :])
    return out.reshape(batch, seq, d)
```

```python
import functools
import math

import jax
import jax.numpy as jnp
import numpy as np
from jax import lax
from jax.experimental import pallas as pl
from jax.experimental.pallas import tpu as pltpu

FILTER_EMB = 33
FAST_DECAY_PCT = 0.3
SLOW_DECAY_PCT = 1.5
DECAY_TARGET = 1e-2
MAX_DECAY = math.log(DECAY_TARGET) / FAST_DECAY_PCT
MIN_DECAY = math.log(DECAY_TARGET) / SLOW_DECAY_PCT
POOL_WINDOWS = (2, 4, 8, 16)
NORM_EPS = 1e-6

LANES = 128
SUBLANES = 8
MXU_DIM = 256
HYENA_COLS = MXU_DIM
ROW_CHUNK = 1024
MLP_ROWS = 256
CONV_BLOCKS = 4
MIB = 1024 * 1024
INPROJ_VMEM = 55 * MIB
HYENA_VMEM = 36 * MIB
MERGE_VMEM = 47 * MIB

F32 = jnp.float32
BF16 = jnp.bfloat16
SPEC_DTYPE = BF16


def _dot(a, b):
    return jnp.dot(a, b, preferred_element_type=F32)


def _dot_split(a, b):
    a_hi, b_hi = a.astype(BF16), b.astype(BF16)
    a_lo = (a - a_hi.astype(F32)).astype(BF16)
    b_lo = (b - b_hi.astype(F32)).astype(BF16)
    return _dot(a_hi, b_hi) + _dot(a_lo, b_hi) + _dot(a_hi, b_lo)


def _rms_scale(x):
    return x * lax.rsqrt(jnp.mean(x * x, axis=-1, keepdims=True) + NORM_EPS)


def _transform_matrices(lb):
    period = 8 * lb
    odd = 2 * np.arange(lb, dtype=np.int64) + 1
    ang = ((odd[:, None] * odd[None, :]) % period) * (2.0 * math.pi / period)
    table = np.concatenate([np.cos(ang), np.sin(ang)], axis=1).astype(np.float32)
    return jnp.asarray(table).astype(BF16)


def _filter_block(first, z_ref, w1_ref, b1_ref, w2_ref, b2_ref, w3_ref, b3_ref, fr_ref,
                  w4f_ref, w4b_ref, dl_ref, cs_ref, kre_ref, kim_ref,
                  h_ref, cphi_ref, sphi_ref, sgn_ref, kk_ref, xr_ref, xi_ref):
    seq = z_ref.shape[0]
    lb = cs_ref.shape[0]
    nb = seq // lb
    cb = dl_ref.shape[1]
    scale = 1.0 / lb

    @pl.when(first)
    def _():
        fr = fr_ref[...]

        def mlp(i, carry):
            rows = pl.ds(pl.multiple_of(i * MLP_ROWS, MLP_ROWS), MLP_ROWS)
            h = jnp.sin(fr * (_dot_split(z_ref[rows, :], w1_ref[...]) + b1_ref[...]))
            h = jnp.sin(fr * (_dot_split(h, w2_ref[...]) + b2_ref[...]))
            h_ref[rows, :] = jnp.sin(fr * (_dot_split(h, w3_ref[...]) + b3_ref[...]))
            return carry

        lax.fori_loop(0, seq // MLP_ROWS, mlp, 0)
        f_idx = lax.broadcasted_iota(jnp.int32, cphi_ref.shape, 0)
        phi = (f_idx.astype(F32) + 0.5) * (math.pi / (2 * lb))
        cphi_ref[...] = jnp.cos(phi) * scale
        sphi_ref[...] = jnp.sin(phi) * scale
        sgn_ref[...] = (1 - 2 * (f_idx & 1)).astype(F32)

    h = h_ref[...]
    decay = jnp.exp(-z_ref[:, 0:1] * dl_ref[...])
    taps = (_dot_split(h, w4f_ref[...]) * decay, _dot_split(h, w4b_ref[...]) * decay)
    taps = tuple(k.astype(BF16) for k in taps)
    for side, k in enumerate(taps):
        for s_ in range(nb):
            c0 = (side * nb + s_) * cb
            kk_ref[:, c0:c0 + cb] = k[s_ * lb:(s_ + 1) * lb]

    group = 2 * cb
    tile = lambda ref, width: jnp.concatenate([ref[...]] * (width // LANES), axis=1)
    c_phi, s_phi = tile(cphi_ref, group), tile(sphi_ref, group)
    for g in range(2 * nb * cb // group):
        cols = slice(g * group, (g + 1) * group)
        kk = kk_ref[:, cols]
        a = _dot(cs_ref[:, 0:lb], kk)
        b = _dot(cs_ref[:, lb:2 * lb], kk)
        xr_ref[:, cols] = c_phi * a + s_phi * b
        xi_ref[:, cols] = s_phi * a - c_phi * b

    sgn = tile(sgn_ref, cb)

    def seg(ref, side, s_):
        c0 = (side * nb + s_) * cb
        return ref[:, c0:c0 + cb]

    def causal_block(side, delta):
        r0 = (delta - 1) * lb
        first = taps[side][r0:r0 + 1].astype(F32) * scale
        re = seg(xr_ref, side, delta) - sgn * seg(xi_ref, side, delta - 1)
        im = seg(xi_ref, side, delta) + sgn * (seg(xr_ref, side, delta - 1) - first)
        return re, im

    mid = nb - 1
    out = kre_ref.dtype
    kre_ref[mid] = (seg(xr_ref, 0, 0) + seg(xr_ref, 1, 0)).astype(out)
    kim_ref[mid] = (seg(xi_ref, 0, 0) - seg(xi_ref, 1, 0)).astype(out)
    for delta in range(1, nb):
        re, im = causal_block(0, delta)
        kre_ref[mid + delta] = re.astype(out)
        kim_ref[mid + delta] = im.astype(out)
        re, im = causal_block(1, delta)
        kre_ref[mid - delta] = re.astype(out)
        kim_ref[mid - delta] = (-im).astype(out)


def _conv3(p, w, bias, lo, n):
    n_rows = p.shape[0]
    assert 1 <= lo and lo + n < n_rows
    dn = pltpu.roll(p, 1, 0)
    up = pltpu.roll(p, n_rows - 1, 0)
    out = dn * w[0:1] + p * w[1:2] + up * w[2:3] + bias
    return out[lo:lo + n]


def _window_mean_minus_self(p, win, lo, n, at_start, at_end):
    n_rows = p.shape[0]
    half = win // 2
    sl = SUBLANES
    assert lo >= win and lo + n + win <= n_rows and half <= sl
    s = p
    w = 1
    while w < win:
        s = s + pltpu.roll(s, w, 0)
        w *= 2
    if half > 1:
        s = pltpu.roll(s, n_rows - (half - 1), 0)
    total = s[lo:lo + n]
    edge = lax.broadcasted_iota(jnp.int32, (sl, p.shape[1]), 0)
    cnt_first = jnp.where(at_start, jnp.minimum(edge + half, win), win).astype(F32)
    cnt_last = jnp.where(at_end, jnp.minimum(sl - edge + half, win), win).astype(F32)
    mean = jnp.concatenate([total[0:sl] / cnt_first,
                            total[sl:n - sl] * (1.0 / win),
                            total[n - sl:] / cnt_last], axis=0)
    return mean - p[lo:lo + n]


def _inproj_kernel(xp_ref, x_ref, xn_ref, g_ref, w_ref, b_ref, cw_ref, cb_ref,
                   pw_ref, pb_ref, ps_ref, u_ref, gate_ref, yp_ref, p_ref, *, chunks_per_seq):
    i = pl.program_id(0)
    rows = x_ref.shape[0]
    halo = xp_ref.shape[0]
    dh = u_ref.shape[1]
    cb = MXU_DIM
    at_start = i % chunks_per_seq == 0
    at_end = i % chunks_per_seq == chunks_per_seq - 1

    xw = jnp.concatenate([xp_ref[...], x_ref[...], xn_ref[...]], axis=0)
    h = (_rms_scale(xw) * g_ref[...]).astype(BF16)

    h_main = h[halo:halo + rows]

    def project(rows_h, c0):
        return _dot(rows_h, w_ref[:, c0:c0 + cb]) + b_ref[:, c0:c0 + cb]

    def conv_input(c0):
        p = project(h, c0)
        return jnp.concatenate(
            [jnp.where(at_start, 0.0, p[0:halo]), p[halo:halo + rows],
             jnp.where(at_end, 0.0, p[halo + rows:])], axis=0)

    def conv(c0):
        return _conv3(conv_input(c0), cw_ref[:, c0:c0 + cb], cb_ref[:, c0:c0 + cb], halo, rows)

    dp = yp_ref.shape[1]
    assert dp == len(POOL_WINDOWS) * cb
    def hyena_group(c):
        lanes = slice(c * cb, (c + 1) * cb)
        u_ref[:, lanes] = conv(dh + c * cb) * conv(2 * dh + c * cb)
        z = project(h_main, 3 * dh + c * cb)
        gate_ref[:, lanes] = (conv(c * cb) * jax.nn.silu(z)).astype(BF16)

    def pool_group(g, win):
        lanes = slice(g * cb, (g + 1) * cb)
        pooled = _window_mean_minus_self(conv_input(4 * dh + g * cb), win, halo, rows,
                                         at_start, at_end)
        y = _dot(pooled.astype(BF16), pw_ref[g]) + pb_ref[:, lanes]
        z = project(h_main, 4 * dh + dp + g * cb)
        yp_ref[:, lanes] = (y * ps_ref[:, lanes] * jax.nn.silu(z)).astype(BF16)

    for c in range(dh // cb):
        hyena_group(c)
    for g, win in enumerate(POOL_WINDOWS):
        pool_group(g, win)

    for c0 in range(0, p_ref.shape[1], cb):
        p_ref[:, c0:c0 + cb] = project(h_main, 4 * dh + 2 * dp + c0).astype(BF16)


def _inproj(x2, g_norm, w_in_bf, b_in, conv_w, conv_b, pool_w_bf, pool_b, pool_scale, seq):
    n_tok, d = x2.shape
    cols = w_in_bf.shape[1]
    dh = conv_b.shape[1] // 3
    dp = pool_b.shape[1]
    n_gate = cols - 4 * dh - 2 * dp
    rows = ROW_CHUNK
    halo = 2 * SUBLANES
    per_halo = rows // halo
    last_halo = n_tok // halo - 1
    full = lambda a: pl.BlockSpec(a.shape, lambda i: (0,) * a.ndim)
    tile = lambda width: pl.BlockSpec((rows, width), lambda i: (i, 0))
    return pl.pallas_call(
        functools.partial(_inproj_kernel, chunks_per_seq=seq // rows),
        grid=(n_tok // rows,),
        in_specs=[
            pl.BlockSpec((halo, d), lambda i: (jnp.maximum(i * per_halo - 1, 0), 0)),
            tile(d),
            pl.BlockSpec((halo, d), lambda i: (jnp.minimum((i + 1) * per_halo, last_halo), 0)),
            full(g_norm),
            pl.BlockSpec(w_in_bf.shape, lambda i: (0, 0), pipeline_mode=pl.Buffered(1)),
            full(b_in), full(conv_w), full(conv_b),
            full(pool_w_bf), full(pool_b), full(pool_scale),
        ],
        out_specs=[tile(dh), tile(dh), tile(dp), tile(n_gate)],
        out_shape=[
            jax.ShapeDtypeStruct((n_tok, dh), F32),
            jax.ShapeDtypeStruct((n_tok, dh), BF16),
            jax.ShapeDtypeStruct((n_tok, dp), BF16),
            jax.ShapeDtypeStruct((n_tok, n_gate), BF16),
        ],
        compiler_params=pltpu.CompilerParams(
            dimension_semantics=("parallel",), vmem_limit_bytes=INPROJ_VMEM),
        name="inproj",
    )(x2, x2, x2, g_norm, w_in_bf, b_in, conv_w, conv_b, pool_w_bf, pool_b, pool_scale)


def _hyena_kernel(u_ref, gate_ref, d_ref, cs_ref, *rest):
    filter_inputs, (y_ref, av_ref, pq_ref, kre_ref, kim_ref), filter_scratch = (
        rest[:11], rest[11:16], rest[16:])
    seq = u_ref.shape[0]
    lb = cs_ref.shape[0]
    nb = seq // lb
    mid = nb - 1
    blocks = [slice(b * lb, (b + 1) * lb) for b in range(nb)]

    @pl.when(pl.program_id(1) == 0)
    def _():
        _filter_block(pl.program_id(0) == 0, *filter_inputs, cs_ref, kre_ref, kim_ref,
                      *filter_scratch)

    for b, rows in enumerate(blocks):
        ub = u_ref[rows, :].astype(BF16)
        av_ref[2 * b] = _dot(cs_ref[:, 0:lb], ub).astype(av_ref.dtype)
        av_ref[2 * b + 1] = _dot(cs_ref[:, lb:2 * lb], ub).astype(av_ref.dtype)

    for a, rows in enumerate(blocks):
        p = q = None
        for b in range(nb):
            kr, ki = kre_ref[mid + a - b], kim_ref[mid + a - b]
            ur, vi = av_ref[2 * b], av_ref[2 * b + 1]
            tp = kr * ur + ki * vi
            tq = kr * vi - ki * ur
            p = tp if p is None else p + tp
            q = tq if q is None else q + tq
        pq_ref[a, 0:lb, :] = p.astype(BF16)
        pq_ref[a, lb:2 * lb, :] = q.astype(BF16)
        y = _dot(cs_ref[...], pq_ref[a])
        y = y + u_ref[rows, :] * d_ref[...]
        y_ref[rows, :] = (y * gate_ref[rows, :].astype(F32)).astype(BF16)


def _hyena(u, gate, hyena_d, cs, z_pad, w1_pad, b1, w2, b2, w3, b3, freq, w4, deltas, batch, seq):
    dh = hyena_d.shape[1]
    cw = HYENA_COLS
    lb = cs.shape[0]
    nb = seq // lb
    hf = w2.shape[0]
    nblk = dh // cw
    act = pl.BlockSpec((seq, cw), lambda c, b: (b, c))
    full = lambda a: pl.BlockSpec(a.shape, lambda c, b: (0,) * a.ndim)
    return pl.pallas_call(
        _hyena_kernel,
        grid=(nblk, batch),
        in_specs=[
            act, act,
            pl.BlockSpec((1, cw), lambda c, b: (0, c)),
            full(cs),
            full(z_pad), full(w1_pad), full(b1), full(w2), full(b2), full(w3), full(b3), full(freq),
            pl.BlockSpec((hf, cw), lambda c, b: (0, c)),
            pl.BlockSpec((hf, cw), lambda c, b: (0, nblk + c)),
            pl.BlockSpec((1, cw), lambda c, b: (0, c)),
        ],
        out_specs=act,
        out_shape=jax.ShapeDtypeStruct((batch * seq, dh), BF16),
        scratch_shapes=[
            pltpu.VMEM((2 * nb, lb, cw), SPEC_DTYPE),
            pltpu.VMEM((nb, 2 * lb, cw), BF16),
            pltpu.VMEM((2 * nb - 1, lb, cw), SPEC_DTYPE),
            pltpu.VMEM((2 * nb - 1, lb, cw), SPEC_DTYPE),
            pltpu.VMEM((seq, hf), F32),
            pltpu.VMEM((lb, LANES), F32),
            pltpu.VMEM((lb, LANES), F32),
            pltpu.VMEM((lb, LANES), F32),
            pltpu.VMEM((lb, 2 * nb * cw), BF16),
            pltpu.VMEM((lb, 2 * nb * cw), F32),
            pltpu.VMEM((lb, 2 * nb * cw), F32),
        ],
        compiler_params=pltpu.CompilerParams(
            dimension_semantics=("arbitrary", "arbitrary"), vmem_limit_bytes=HYENA_VMEM),
        name="hyena",
    )(u, gate, hyena_d, cs, z_pad, w1_pad, b1, w2, b2, w3, b3, freq, w4, w4, deltas)


def _merge_kernel(yh_ref, yp_ref, g0_ref, g1_ref, x_ref, wh_ref, wp_ref, wo_ref, gf_ref, o_ref):
    out_h = _dot(yh_ref[...], wh_ref[...])
    out_p = _dot(yp_ref[...], wp_ref[...])
    merged = (jax.nn.sigmoid(g0_ref[...].astype(F32)) * out_h
              + jax.nn.sigmoid(g1_ref[...].astype(F32)) * out_p)
    res = x_ref[...] + _dot(merged.astype(BF16), wo_ref[...])
    o_ref[...] = _rms_scale(res) * gf_ref[...]


def _merge(yh, yp, gates, x2, wh_bf, wp_bf, wo_bf, g_final, tm=1024):
    n_tok, d = x2.shape
    tile = lambda col: pl.BlockSpec((tm, d), lambda i: (i, col))
    resident = lambda a: pl.BlockSpec(a.shape, lambda i: (0, 0), pipeline_mode=pl.Buffered(1))
    return pl.pallas_call(
        _merge_kernel,
        grid=(n_tok // tm,),
        in_specs=[
            tile(0), tile(0), tile(0), tile(1), tile(0),
            resident(wh_bf), resident(wp_bf), resident(wo_bf),
            pl.BlockSpec((1, d), lambda i: (0, 0)),
        ],
        out_specs=tile(0),
        out_shape=jax.ShapeDtypeStruct((n_tok, d), F32),
        compiler_params=pltpu.CompilerParams(
            dimension_semantics=("parallel",), vmem_limit_bytes=MERGE_VMEM),
        name="merge",
    )(yh, yp, gates, gates, x2, wh_bf, wp_bf, wo_bf, g_final)


def _filter_features(seq):
    t = jnp.linspace(0.0, 1.0, seq, dtype=F32)[:, None]
    bands = (FILTER_EMB - 1) // 2
    w = 2.0 * math.pi * jnp.arange(seq, dtype=F32) / seq
    f = jnp.linspace(1e-4, bands - 1, bands, dtype=F32)
    ang = w[:, None] * f[None, :]
    return jnp.concatenate([t, jnp.cos(ang), -jnp.sin(ang)], axis=-1)


def kernel(x, g_norm, w_in, b_in, conv_w, conv_b, filt_w1, filt_b1, filt_w2, filt_b2, filt_w3, filt_b3, filt_w4, filt_freq, hyena_d, w_hyena_out, pool_w, pool_b, pool_scale, w_pool_out, w_out, g_final):
    batch, seq, d = x.shape
    dh = hyena_d.shape[1]
    dp = pool_b.shape[1]
    assert g_norm.shape[0] == 1, "single-layer block"
    assert d == dh == dp and conv_b.shape[1] == 3 * dh and w_in.shape[2] == 4 * dh + 2 * dp + 2 * d
    assert dh % HYENA_COLS == 0 and seq % ROW_CHUNK == 0
    assert seq % CONV_BLOCKS == 0

    cs = _transform_matrices(seq // CONV_BLOCKS)
    z_pad = jnp.pad(_filter_features(seq), ((0, 0), (0, LANES - FILTER_EMB)))
    w1_pad = jnp.pad(filt_w1[0], ((0, LANES - FILTER_EMB), (0, 0)))
    deltas = jnp.abs(jnp.linspace(MIN_DECAY, MAX_DECAY, dh, dtype=F32))[None, :]
    x2 = x.reshape(batch * seq, d)
    u, gate, yp, gates = _inproj(x2, g_norm, w_in[0].astype(BF16), b_in, conv_w[0], conv_b,
                                 pool_w[0].astype(BF16), pool_b, pool_scale, seq)
    yh = _hyena(u, gate, hyena_d, cs, z_pad, w1_pad, filt_b1, filt_w2[0], filt_b2, filt_w3[0],
                filt_b3, filt_freq, filt_w4[0], deltas, batch, seq)
    out = _merge(yh, yp, gates, x2, w_hyena_out[0].astype(BF16), w_pool_out[0].astype(BF16),
                 w_out[0].astype(BF16), g_final[None, :])
    return out.reshape(batch, seq, d)
```

```python
import functools
import math

import jax
import jax.numpy as jnp
import numpy as np
from jax import lax
from jax.experimental import pallas as pl
from jax.experimental.pallas import tpu as pltpu

FILTER_EMB = 33
FAST_DECAY_PCT = 0.3
SLOW_DECAY_PCT = 1.5
DECAY_TARGET = 1e-2
MAX_DECAY = math.log(DECAY_TARGET) / FAST_DECAY_PCT
MIN_DECAY = math.log(DECAY_TARGET) / SLOW_DECAY_PCT
POOL_WINDOWS = (2, 4, 8, 16)
NORM_EPS = 1e-6

LANES = 128
SUBLANES = 8
MXU_DIM = 256
HYENA_COLS = MXU_DIM
ROW_CHUNK = 1024
MLP_ROWS = 256
CONV_BLOCKS = 4
MIB = 1024 * 1024
INPROJ_VMEM = 55 * MIB
HYENA_VMEM = 36 * MIB
MERGE_VMEM = 47 * MIB

F32 = jnp.float32
BF16 = jnp.bfloat16
SPEC_DTYPE = BF16


def _dot(a, b):
    return jnp.dot(a, b, preferred_element_type=F32)


def _dot_split(a, b):
    a_hi, b_hi = a.astype(BF16), b.astype(BF16)
    a_lo = (a - a_hi.astype(F32)).astype(BF16)
    b_lo = (b - b_hi.astype(F32)).astype(BF16)
    return _dot(a_hi, b_hi) + _dot(a_lo, b_hi) + _dot(a_hi, b_lo)


def _rms_scale(x):
    return x * lax.rsqrt(jnp.mean(x * x, axis=-1, keepdims=True) + NORM_EPS)


def _transform_matrices(lb):
    period = 8 * lb
    odd = 2 * np.arange(lb, dtype=np.int64) + 1
    ang = ((odd[:, None] * odd[None, :]) % period) * (2.0 * math.pi / period)
    table = np.concatenate([np.cos(ang), np.sin(ang)], axis=1).astype(np.float32)
    return jnp.asarray(table).astype(BF16)


def _filter_block(first, z_ref, w1_ref, b1_ref, w2_ref, b2_ref, w3_ref, b3_ref, fr_ref,
                  w4f_ref, w4b_ref, dl_ref, cs_ref, kre_ref, kim_ref,
                  h_ref, cphi_ref, sphi_ref, sgn_ref, kk_ref, xr_ref, xi_ref):
    seq = z_ref.shape[0]
    lb = cs_ref.shape[0]
    nb = seq // lb
    cb = dl_ref.shape[1]
    scale = 1.0 / lb

    @pl.when(first)
    def _():
        fr = fr_ref[...]

        def mlp(i, carry):
            rows = pl.ds(pl.multiple_of(i * MLP_ROWS, MLP_ROWS), MLP_ROWS)
            h = jnp.sin(fr * (_dot_split(z_ref[rows, :], w1_ref[...]) + b1_ref[...]))
            h = jnp.sin(fr * (_dot_split(h, w2_ref[...]) + b2_ref[...]))
            h_ref[rows, :] = jnp.sin(fr * (_dot_split(h, w3_ref[...]) + b3_ref[...]))
            return carry

        lax.fori_loop(0, seq // MLP_ROWS, mlp, 0)
        f_idx = lax.broadcasted_iota(jnp.int32, cphi_ref.shape, 0)
        phi = (f_idx.astype(F32) + 0.5) * (math.pi / (2 * lb))
        cphi_ref[...] = jnp.cos(phi) * scale
        sphi_ref[...] = jnp.sin(phi) * scale
        sgn_ref[...] = (1 - 2 * (f_idx & 1)).astype(F32)

    h = h_ref[...]
    decay = jnp.exp(-z_ref[:, 0:1] * dl_ref[...])
    taps = (_dot_split(h, w4f_ref[...]) * decay, _dot_split(h, w4b_ref[...]) * decay)
    taps = tuple(k.astype(BF16) for k in taps)
    for side, k in enumerate(taps):
        for s_ in range(nb):
            c0 = (side * nb + s_) * cb
            kk_ref[:, c0:c0 + cb] = k[s_ * lb:(s_ + 1) * lb]

    group = 2 * cb
    tile = lambda ref, width: jnp.concatenate([ref[...]] * (width // LANES), axis=1)
    c_phi, s_phi = tile(cphi_ref, group), tile(sphi_ref, group)
    for g in range(2 * nb * cb // group):
        cols = slice(g * group, (g + 1) * group)
        kk = kk_ref[:, cols]
        a = _dot(cs_ref[:, 0:lb], kk)
        b = _dot(cs_ref[:, lb:2 * lb], kk)
        xr_ref[:, cols] = c_phi * a + s_phi * b
        xi_ref[:, cols] = s_phi * a - c_phi * b

    sgn = tile(sgn_ref, cb)

    def seg(ref, side, s_):
        c0 = (side * nb + s_) * cb
        return ref[:, c0:c0 + cb]

    def causal_block(side, delta):
        r0 = (delta - 1) * lb
        first = taps[side][r0:r0 + 1].astype(F32) * scale
        re = seg(xr_ref, side, delta) - sgn * seg(xi_ref, side, delta - 1)
        im = seg(xi_ref, side, delta) + sgn * (seg(xr_ref, side, delta - 1) - first)
        return re, im

    mid = nb - 1
    out = kre_ref.dtype
    kre_ref[mid] = (seg(xr_ref, 0, 0) + seg(xr_ref, 1, 0)).astype(out)
    kim_ref[mid] = (seg(xi_ref, 0, 0) - seg(xi_ref, 1, 0)).astype(out)
    for delta in range(1, nb):
        re, im = causal_block(0, delta)
        kre_ref[mid + delta] = re.astype(out)
        kim_ref[mid + delta] = im.astype(out)
        re, im = causal_block(1, delta)
        kre_ref[mid - delta] = re.astype(out)
        kim_ref[mid - delta] = (-im).astype(out)


def _conv3(p, w, bias, lo, n):
    n_rows = p.shape[0]
    assert 1 <= lo and lo + n < n_rows
    dn = pltpu.roll(p, 1, 0)
    up = pltpu.roll(p, n_rows - 1, 0)
    out = dn * w[0:1] + p * w[1:2] + up * w[2:3] + bias
    return out[lo:lo + n]


def _window_mean_minus_self(p, win, lo, n, at_start, at_end):
    n_rows = p.shape[0]
    half = win // 2
    sl = SUBLANES
    assert lo >= win and lo + n + win <= n_rows and half <= sl
    s = p
    w = 1
    while w < win:
        s = s + pltpu.roll(s, w, 0)
        w *= 2
    if half > 1:
        s = pltpu.roll(s, n_rows - (half - 1), 0)
    total = s[lo:lo + n]
    edge = lax.broadcasted_iota(jnp.int32, (sl, p.shape[1]), 0)
    cnt_first = jnp.where(at_start, jnp.minimum(edge + half, win), win).astype(F32)
    cnt_last = jnp.where(at_end, jnp.minimum(sl - edge + half, win), win).astype(F32)
    mean = jnp.concatenate([total[0:sl] / cnt_first,
                            total[sl:n - sl] * (1.0 / win),
                            total[n - sl:] / cnt_last], axis=0)
    return mean - p[lo:lo + n]


def _inproj_kernel(xp_ref, x_ref, xn_ref, g_ref, w_ref, b_ref, cw_ref, cb_ref,
                   pw_ref, pb_ref, ps_ref, u_ref, gate_ref, yp_ref, p_ref, *, chunks_per_seq):
    i = pl.program_id(0)
    rows = x_ref.shape[0]
    halo = xp_ref.shape[0]
    dh = u_ref.shape[1]
    cb = MXU_DIM
    at_start = i % chunks_per_seq == 0
    at_end = i % chunks_per_seq == chunks_per_seq - 1

    xw = jnp.concatenate([xp_ref[...], x_ref[...], xn_ref[...]], axis=0)
    h = (_rms_scale(xw) * g_ref[...]).astype(BF16)

    h_main = h[halo:halo + rows]

    def project(rows_h, c0):
        return _dot(rows_h, w_ref[:, c0:c0 + cb]) + b_ref[:, c0:c0 + cb]

    def conv_input(c0):
        p = project(h, c0)
        return jnp.concatenate(
            [jnp.where(at_start, 0.0, p[0:halo]), p[halo:halo + rows],
             jnp.where(at_end, 0.0, p[halo + rows:])], axis=0)

    def conv(c0):
        return _conv3(conv_input(c0), cw_ref[:, c0:c0 + cb], cb_ref[:, c0:c0 + cb], halo, rows)

    dp = yp_ref.shape[1]
    assert dp == len(POOL_WINDOWS) * cb
    def hyena_group(c):
        lanes = slice(c * cb, (c + 1) * cb)
        u_ref[:, lanes] = conv(dh + c * cb) * conv(2 * dh + c * cb)
        z = project(h_main, 3 * dh + c * cb)
        gate_ref[:, lanes] = (conv(c * cb) * jax.nn.silu(z)).astype(BF16)

    def pool_group(g, win):
        lanes = slice(g * cb, (g + 1) * cb)
        pooled = _window_mean_minus_self(conv_input(4 * dh + g * cb), win, halo, rows,
                                         at_start, at_end)
        y = _dot(pooled.astype(BF16), pw_ref[g]) + pb_ref[:, lanes]
        z = project(h_main, 4 * dh + dp + g * cb)
        yp_ref[:, lanes] = (y * ps_ref[:, lanes] * jax.nn.silu(z)).astype(BF16)

    for c in range(dh // cb):
        hyena_group(c)
    for g, win in enumerate(POOL_WINDOWS):
        pool_group(g, win)

    for c0 in range(0, p_ref.shape[1], cb):
        p_ref[:, c0:c0 + cb] = project(h_main, 4 * dh + 2 * dp + c0).astype(BF16)


def _inproj(x2, g_norm, w_in_bf, b_in, conv_w, conv_b, pool_w_bf, pool_b, pool_scale, seq):
    n_tok, d = x2.shape
    cols = w_in_bf.shape[1]
    dh = conv_b.shape[1] // 3
    dp = pool_b.shape[1]
    n_gate = cols - 4 * dh - 2 * dp
    rows = ROW_CHUNK
    halo = 2 * SUBLANES
    per_halo = rows // halo
    last_halo = n_tok // halo - 1
    full = lambda a: pl.BlockSpec(a.shape, lambda i: (0,) * a.ndim)
    tile = lambda width: pl.BlockSpec((rows, width), lambda i: (i, 0))
    return pl.pallas_call(
        functools.partial(_inproj_kernel, chunks_per_seq=seq // rows),
        grid=(n_tok // rows,),
        in_specs=[
            pl.BlockSpec((halo, d), lambda i: (jnp.maximum(i * per_halo - 1, 0), 0)),
            tile(d),
            pl.BlockSpec((halo, d), lambda i: (jnp.minimum((i + 1) * per_halo, last_halo), 0)),
            full(g_norm),
            pl.BlockSpec(w_in_bf.shape, lambda i: (0, 0), pipeline_mode=pl.Buffered(1)),
            full(b_in), full(conv_w), full(conv_b),
            full(pool_w_bf), full(pool_b), full(pool_scale),
        ],
        out_specs=[tile(dh), tile(dh), tile(dp), tile(n_gate)],
        out_shape=[
            jax.ShapeDtypeStruct((n_tok, dh), F32),
            jax.ShapeDtypeStruct((n_tok, dh), BF16),
            jax.ShapeDtypeStruct((n_tok, dp), BF16),
            jax.ShapeDtypeStruct((n_tok, n_gate), BF16),
        ],
        compiler_params=pltpu.CompilerParams(
            dimension_semantics=("parallel",), vmem_limit_bytes=INPROJ_VMEM),
        name="inproj",
    )(x2, x2, x2, g_norm, w_in_bf, b_in, conv_w, conv_b, pool_w_bf, pool_b, pool_scale)


def _hyena_kernel(u_ref, gate_ref, d_ref, cs_ref, *rest):
    filter_inputs, (y_ref, av_ref, pq_ref, kre_ref, kim_ref), filter_scratch = (
        rest[:11], rest[11:16], rest[16:])
    seq = u_ref.shape[0]
    lb = cs_ref.shape[0]
    nb = seq // lb
    mid = nb - 1
    blocks = [slice(b * lb, (b + 1) * lb) for b in range(nb)]

    @pl.when(pl.program_id(1) == 0)
    def _():
        _filter_block(pl.program_id(0) == 0, *filter_inputs, cs_ref, kre_ref, kim_ref,
                      *filter_scratch)

    for b, rows in enumerate(blocks):
        ub = u_ref[rows, :].astype(BF16)
        av_ref[2 * b] = _dot(cs_ref[:, 0:lb], ub).astype(av_ref.dtype)
        av_ref[2 * b + 1] = _dot(cs_ref[:, lb:2 * lb], ub).astype(av_ref.dtype)

    for a, rows in enumerate(blocks):
        p = q = None
        for b in range(nb):
            kr, ki = kre_ref[mid + a - b], kim_ref[mid + a - b]
            ur, vi = av_ref[2 * b], av_ref[2 * b + 1]
            tp = kr * ur + ki * vi
            tq = kr * vi - ki * ur
            p = tp if p is None else p + tp
            q = tq if q is None else q + tq
        pq_ref[a, 0:lb, :] = p.astype(BF16)
        pq_ref[a, lb:2 * lb, :] = q.astype(BF16)
        y = _dot(cs_ref[...], pq_ref[a])
        y = y + u_ref[rows, :] * d_ref[...]
        y_ref[rows, :] = (y * gate_ref[rows, :].astype(F32)).astype(BF16)


def _hyena(u, gate, hyena_d, cs, z_pad, w1_pad, b1, w2, b2, w3, b3, freq, w4, deltas, batch, seq):
    dh = hyena_d.shape[1]
    cw = HYENA_COLS
    lb = cs.shape[0]
    nb = seq // lb
    hf = w2.shape[0]
    nblk = dh // cw
    act = pl.BlockSpec((seq, cw), lambda c, b: (b, c))
    full = lambda a: pl.BlockSpec(a.shape, lambda c, b: (0,) * a.ndim)
    return pl.pallas_call(
        _hyena_kernel,
        grid=(nblk, batch),
        in_specs=[
            act, act,
            pl.BlockSpec((1, cw), lambda c, b: (0, c)),
            full(cs),
            full(z_pad), full(w1_pad), full(b1), full(w2), full(b2), full(w3), full(b3), full(freq),
            pl.BlockSpec((hf, cw), lambda c, b: (0, c)),
            pl.BlockSpec((hf, cw), lambda c, b: (0, nblk + c)),
            pl.BlockSpec((1, cw), lambda c, b: (0, c)),
        ],
        out_specs=act,
        out_shape=jax.ShapeDtypeStruct((batch * seq, dh), BF16),
        scratch_shapes=[
            pltpu.VMEM((2 * nb, lb, cw), SPEC_DTYPE),
            pltpu.VMEM((nb, 2 * lb, cw), BF16),
            pltpu.VMEM((2 * nb - 1, lb, cw), SPEC_DTYPE),
            pltpu.VMEM((2 * nb - 1, lb, cw), SPEC_DTYPE),
            pltpu.VMEM((seq, hf), F32),
            pltpu.VMEM((lb, LANES), F32),
            pltpu.VMEM((lb, LANES), F32),
            pltpu.VMEM((lb, LANES), F32),
            pltpu.VMEM((lb, 2 * nb * cw), BF16),
            pltpu.VMEM((lb, 2 * nb * cw), F32),
            pltpu.VMEM((lb, 2 * nb * cw), F32),
        ],
        compiler_params=pltpu.CompilerParams(
            dimension_semantics=("arbitrary", "arbitrary"), vmem_limit_bytes=HYENA_VMEM),
        name="hyena",
    )(u, gate, hyena_d, cs, z_pad, w1_pad, b1, w2, b2, w3, b3, freq, w4, w4, deltas)


def _merge_kernel(yh_ref, yp_ref, g0_ref, g1_ref, x_ref, wh_ref, wp_ref, wo_ref, gf_ref, o_ref):
    out_h = _dot(yh_ref[...], wh_ref[...])
    out_p = _dot(yp_ref[...], wp_ref[...])
    merged = (jax.nn.sigmoid(g0_ref[...].astype(F32)) * out_h
              + jax.nn.sigmoid(g1_ref[...].astype(F32)) * out_p)
    res = x_ref[...] + _dot(merged.astype(BF16), wo_ref[...])
    o_ref[...] = _rms_scale(res) * gf_ref[...]


def _merge(yh, yp, gates, x2, wh_bf, wp_bf, wo_bf, g_final, tm=1024):
    n_tok, d = x2.shape
    tile = lambda col: pl.BlockSpec((tm, d), lambda i: (i, col))
    resident = lambda a: pl.BlockSpec(a.shape, lambda i: (0, 0), pipeline_mode=pl.Buffered(1))
    return pl.pallas_call(
        _merge_kernel,
        grid=(n_tok // tm,),
        in_specs=[
            tile(0), tile(0), tile(0), tile(1), tile(0),
            resident(wh_bf), resident(wp_bf), resident(wo_bf),
            pl.BlockSpec((1, d), lambda i: (0, 0)),
        ],
        out_specs=tile(0),
        out_shape=jax.ShapeDtypeStruct((n_tok, d), F32),
        compiler_params=pltpu.CompilerParams(
            dimension_semantics=("parallel",), vmem_limit_bytes=MERGE_VMEM),
        name="merge",
    )(yh, yp, gates, gates, x2, wh_bf, wp_bf, wo_bf, g_final)


def _filter_features(seq):
    t = np.linspace(0.0, 1.0, seq)[:, None]
    bands = (FILTER_EMB - 1) // 2
    w = 2.0 * math.pi * np.arange(seq) / seq
    f = np.linspace(1e-4, bands - 1, bands)
    ang = w[:, None] * f[None, :]
    z = np.concatenate([t, np.cos(ang), -np.sin(ang)], axis=-1)
    return jnp.asarray(np.pad(z, ((0, 0), (0, LANES - FILTER_EMB))).astype(np.float32))


def kernel(x, g_norm, w_in, b_in, conv_w, conv_b, filt_w1, filt_b1, filt_w2, filt_b2, filt_w3, filt_b3, filt_w4, filt_freq, hyena_d, w_hyena_out, pool_w, pool_b, pool_scale, w_pool_out, w_out, g_final):
    batch, seq, d = x.shape
    dh = hyena_d.shape[1]
    dp = pool_b.shape[1]
    assert g_norm.shape[0] == 1, "single-layer block"
    assert d == dh == dp and conv_b.shape[1] == 3 * dh and w_in.shape[2] == 4 * dh + 2 * dp + 2 * d
    assert dh % HYENA_COLS == 0 and seq % ROW_CHUNK == 0
    assert seq % CONV_BLOCKS == 0

    cs = _transform_matrices(seq // CONV_BLOCKS)
    z_pad = _filter_features(seq)
    w1_pad = jnp.pad(filt_w1[0], ((0, LANES - FILTER_EMB), (0, 0)))
    deltas = jnp.asarray(np.abs(np.linspace(MIN_DECAY, MAX_DECAY, dh))[None, :].astype(np.float32))
    x2 = x.reshape(batch * seq, d)
    u, gate, yp, gates = _inproj(x2, g_norm, w_in[0].astype(BF16), b_in, conv_w[0], conv_b,
                                 pool_w[0].astype(BF16), pool_b, pool_scale, seq)
    yh = _hyena(u, gate, hyena_d, cs, z_pad, w1_pad, filt_b1, filt_w2[0], filt_b2, filt_w3[0],
                filt_b3, filt_freq, filt_w4[0], deltas, batch, seq)
    out = _merge(yh, yp, gates, x2, w_hyena_out[0].astype(BF16), w_pool_out[0].astype(BF16),
                 w_out[0].astype(BF16), g_final[None, :])
    return out.reshape(batch, seq, d)
```
